```python
import jax, jax.numpy as jnp
from jax import lax
import numpy as np

D_MODEL = 1024
BATCH = 2
SEQ = 16384
DEPTH = 2

GRID_W = 64
CTX_LEN = 256
CHUNK = 64
CONV_W = 5
POOL_WINDOWS = (2, 4, 8, 16)
POOL_DIM = D_MODEL // 4
POOL_GROUP = POOL_DIM // len(POOL_WINDOWS)
SSD_DIM = 3 * D_MODEL // 8
SSD_HEAD_DIM = 64
SSD_HEADS = SSD_DIM // SSD_HEAD_DIM
SSD_GROUPS = 2
SSD_STATE = 128
SSD_BC = SSD_GROUPS * SSD_STATE
GDN_DIM = 3 * D_MODEL // 8
GDN_HEAD_DIM = 64
GDN_HEADS = GDN_DIM // GDN_HEAD_DIM
D_MIX = POOL_DIM + SSD_DIM + GDN_DIM
IN_SPLITS = (POOL_DIM, SSD_DIM, SSD_DIM + 2 * SSD_BC, 2 * SSD_HEADS, 3 * GDN_DIM, GDN_DIM, 2 * GDN_HEADS, 2 * GDN_HEADS)
IN_DIM = sum(IN_SPLITS)
N_EXPERTS = 16
EC_CAPACITY = 2
EXPERT_FF = D_MODEL // 2
EPS = 1e-6
F32 = jnp.float32

kernel_name = "hybrid_pool_ssd_gdn_ec_moe_dit"


def rms_norm(x, g):
    x32 = x.astype(F32)
    y = x32 * lax.rsqrt(jnp.mean(x32 * x32, axis=-1, keepdims=True) + EPS) * g.astype(F32)
    return y.astype(x.dtype)


def l2_norm(a):
    return a * lax.rsqrt(jnp.sum(a * a, axis=-1, keepdims=True) + EPS)


def modulate(h, shift, scale):
    return h * (1 + scale[:, None]) + shift[:, None]


def flip_t(a):
    return jnp.flip(a, axis=1)


def to_column_major(u):
    b, T, C = u.shape
    rows = T // GRID_W
    return u.reshape(b, rows, GRID_W, C).transpose(0, 2, 1, 3).reshape(b, T, C)


def to_raster(u):
    b, T, C = u.shape
    rows = T // GRID_W
    return u.reshape(b, GRID_W, rows, C).transpose(0, 2, 1, 3).reshape(b, T, C)


def to_chunks(a):
    b, T = a.shape[:2]
    a = a.reshape((b, T // CHUNK, CHUNK) + a.shape[2:])
    return jnp.moveaxis(a, 2, 3)


def dwconv_centred(u, w):
    k, ch = w.shape
    return lax.conv_general_dilated(u, w[:, None, :], window_strides=(1,), padding=[(k // 2, k // 2)],
                                    dimension_numbers=('NWC', 'WIO', 'NWC'), feature_group_count=ch)


def pool_branch(u, pool_w, pool_scale):
    b, T, _ = u.shape
    u32 = u.astype(F32)
    csum = jnp.concatenate([jnp.zeros((b, 1, POOL_DIM), F32), jnp.cumsum(u32, axis=1)], axis=1)
    t = jnp.arange(T)
    groups = []
    for gi, win in enumerate(POOL_WINDOWS):
        lo = jnp.clip(t - win // 2, 0, T)
        hi = jnp.clip(t + win // 2, 0, T)
        sl = slice(gi * POOL_GROUP, (gi + 1) * POOL_GROUP)
        cg = csum[:, :, sl]
        mean = (cg[:, hi] - cg[:, lo]) / (hi - lo).astype(F32)[None, :, None]
        groups.append(mean - u32[:, :, sl])
    d = jnp.stack(groups, axis=2)
    y = jnp.einsum('btgc,gcd->btgd', d, pool_w.astype(F32)).reshape(b, T, POOL_DIM) * pool_scale.astype(F32)
    return y.astype(u.dtype)


def ssd_chunked(xdt, log_a, bm, cm, s0):
    b, T, H, P = xdt.shape
    xc, bc, cc = to_chunks(xdt), to_chunks(bm), to_chunks(cm)
    ac = jnp.cumsum(to_chunks(log_a), axis=-1)
    lower = jnp.tril(jnp.ones((CHUNK, CHUNK), bool))
    decay = jnp.exp(jnp.where(lower, ac[..., :, None] - ac[..., None, :], -jnp.inf))
    y_intra = jnp.einsum('bchij,bchjp->bchip', jnp.einsum('bchis,bchjs->bchij', cc, bc) * decay, xc)
    states = jnp.einsum('bchjs,bchjp->bchsp', bc * jnp.exp(ac[..., -1:] - ac)[..., None], xc)
    chunk_decay = jnp.exp(ac[..., -1])

    def step(s, inp):
        st, dc = inp
        return s * dc[..., None, None] + st, s

    s_fin, s_start = lax.scan(step, s0, (jnp.moveaxis(states, 1, 0), jnp.moveaxis(chunk_decay, 1, 0)))
    s_start = jnp.moveaxis(s_start, 0, 1)
    y_inter = jnp.einsum('bchis,bchsp->bchip', cc * jnp.exp(ac)[..., None], s_start)
    y = jnp.moveaxis(y_intra + y_inter, 2, 3).reshape(b, T, H, P)
    return y, s_fin


def gdn_chunked(q, k, v, g, beta, s0):
    b, T, H, _ = q.shape
    V = v.shape[-1]
    qc, kc, vc = to_chunks(q), to_chunks(k), to_chunks(v)
    gc = jnp.cumsum(to_chunks(g), axis=-1)
    bc = to_chunks(beta)
    lower = jnp.tril(jnp.ones((CHUNK, CHUNK), bool))
    strict = jnp.tril(jnp.ones((CHUNK, CHUNK), bool), -1)
    decay = jnp.exp(jnp.where(lower, gc[..., :, None] - gc[..., None, :], -jnp.inf))
    kb = kc * bc[..., None]
    xm = jnp.where(strict, jnp.einsum('bnhik,bnhjk->bnhij', kb, kc) * decay, 0.0)
    eye = jnp.eye(CHUNK, dtype=xm.dtype)
    tm = lax.linalg.triangular_solve(xm + eye, jnp.broadcast_to(eye, xm.shape), left_side=True, lower=True,
                                     unit_diagonal=True)
    u = jnp.einsum('bnhij,bnhjv->bnhiv', tm, vc * bc[..., None])
    w = jnp.einsum('bnhij,bnhjk->bnhik', tm, kb * jnp.exp(gc)[..., None])
    attn = jnp.einsum('bnhik,bnhjk->bnhij', qc, kc) * decay
    qg = qc * jnp.exp(gc)[..., None]
    kend = kc * jnp.exp(gc[..., -1:] - gc)[..., None]
    gend = jnp.exp(gc[..., -1])

    def step(s, inp):
        u_i, w_i, a_i, qg_i, ke_i, ge_i = inp
        v_new = u_i - jnp.einsum('bhlk,bhkv->bhlv', w_i, s)
        o_i = jnp.einsum('bhlk,bhkv->bhlv', qg_i, s) + jnp.einsum('bhij,bhjv->bhiv', a_i, v_new)
        s = s * ge_i[..., None, None] + jnp.einsum('bhlk,bhlv->bhkv', ke_i, v_new)
        return s, o_i

    xs = (jnp.moveaxis(u, 1, 0), jnp.moveaxis(w, 1, 0), jnp.moveaxis(attn, 1, 0),
          jnp.moveaxis(qg, 1, 0), jnp.moveaxis(kend, 1, 0), jnp.moveaxis(gend, 1, 0))
    s_fin, o = lax.scan(step, s0, xs)
    o = jnp.moveaxis(jnp.moveaxis(o, 0, 1), 2, 3).reshape(b, T, H, V)
    return o, s_fin


def ssd_stream(z, xbc, dt_raw, conv_w, conv_b, a_log, dt_bias, d_skip, norm_g, s0_f, s0_b):
    b, T, _ = z.shape
    xbc = jax.nn.silu(dwconv_centred(xbc, conv_w) + conv_b).astype(F32)
    xs, bm, cm = jnp.split(xbc, [SSD_DIM, SSD_DIM + SSD_BC], axis=-1)
    xs = xs.reshape(b, T, SSD_HEADS, SSD_HEAD_DIM)
    rep = SSD_HEADS // SSD_GROUPS
    bm = jnp.repeat(bm.reshape(b, T, SSD_GROUPS, SSD_STATE), rep, axis=2)
    cm = jnp.repeat(cm.reshape(b, T, SSD_GROUPS, SSD_STATE), rep, axis=2)
    dt = jax.nn.softplus(dt_raw.astype(F32).reshape(b, T, 2, SSD_HEADS) + dt_bias.astype(F32))
    a = -jnp.exp(a_log.astype(F32))
    y_f, s_f = ssd_chunked(xs * dt[:, :, 0, :, None], dt[:, :, 0] * a[0], bm, cm, s0_f)
    y_b, s_b = ssd_chunked(flip_t(xs * dt[:, :, 1, :, None]), flip_t(dt[:, :, 1] * a[1]), flip_t(bm), flip_t(cm), s0_b)
    y = y_f + flip_t(y_b) + d_skip.astype(F32)[:, None] * xs
    y = y.reshape(b, T, SSD_DIM) * jax.nn.silu(z.astype(F32))
    return rms_norm(y, norm_g).astype(z.dtype), s_f, s_b


def gdn_stream(qkv, gate, a_raw, b_raw, conv_w, a_log, dt_bias, norm_g, s0_f, s0_b):
    b, T, _ = qkv.shape
    qkv = jax.nn.silu(dwconv_centred(qkv, conv_w)).astype(F32)
    q, k, v = [t.reshape(b, T, GDN_HEADS, GDN_HEAD_DIM) for t in jnp.split(qkv, 3, axis=-1)]
    q = l2_norm(q) * (GDN_HEAD_DIM ** -0.5)
    k = l2_norm(k)
    g = -jnp.exp(a_log.astype(F32)) * jax.nn.softplus(a_raw.astype(F32).reshape(b, T, 2, GDN_HEADS) + dt_bias.astype(F32))
    beta = jax.nn.sigmoid(b_raw.astype(F32).reshape(b, T, 2, GDN_HEADS))
    o_f, s_f = gdn_chunked(q, k, v, g[:, :, 0], beta[:, :, 0], s0_f)
    o_b, s_b = gdn_chunked(flip_t(q), flip_t(k), flip_t(v), flip_t(g[:, :, 1]), flip_t(beta[:, :, 1]), s0_b)
    o = rms_norm(o_f + flip_t(o_b), norm_g).reshape(b, T, GDN_DIM) * jax.nn.silu(gate.astype(F32))
    return o.astype(gate.dtype), s_f, s_b


def hybrid_mixer(h_ctx, h_lat, w_in, w_out, pool_w, pool_scale, ssd_conv_w, ssd_conv_b, ssd_a_log, ssd_dt_bias,
                 ssd_d, ssd_norm_g, gdn_conv_w, gdn_a_log, gdn_dt_bias, gdn_norm_g, with_ctx_out):
    b = h_lat.shape[0]
    cut = np.cumsum(IN_SPLITS)[:-1].tolist()
    pc = jnp.split(h_ctx @ w_in, cut, axis=-1)
    pl = jnp.split(h_lat @ w_in, cut, axis=-1)
    ssd_p = (ssd_conv_w, ssd_conv_b, ssd_a_log, ssd_dt_bias, ssd_d, ssd_norm_g)
    gdn_p = (gdn_conv_w, gdn_a_log, gdn_dt_bias, gdn_norm_g)
    zs = jnp.zeros((b, SSD_HEADS, SSD_STATE, SSD_HEAD_DIM), F32)
    zg = jnp.zeros((b, GDN_HEADS, GDN_HEAD_DIM, GDN_HEAD_DIM), F32)
    s_ctx, ssd_sf, ssd_sb = ssd_stream(pc[1], pc[2], pc[3], *ssd_p, zs, zs)
    g_ctx, gdn_sf, gdn_sb = gdn_stream(pc[4], pc[5], pc[6], pc[7], *gdn_p, zg, zg)
    s_lat, _, _ = ssd_stream(pl[1], pl[2], pl[3], *ssd_p, ssd_sf, ssd_sb)
    g_lat, _, _ = gdn_stream(to_column_major(pl[4]), to_column_major(pl[5]), to_column_major(pl[6]),
                             to_column_major(pl[7]), *gdn_p, gdn_sf, gdn_sb)
    g_lat = to_raster(g_lat)
    lat = jnp.concatenate([pool_branch(pl[0], pool_w, pool_scale), s_lat, g_lat], axis=-1) @ w_out
    if not with_ctx_out:
        return None, lat
    ctx = jnp.concatenate([pool_branch(pc[0], pool_w, pool_scale), s_ctx, g_ctx], axis=-1) @ w_out
    return ctx, lat


def expert_choice_ffn(h, router_w, w_gate, w_up, w_down):
    b, T, D = h.shape
    cap = EC_CAPACITY * T // N_EXPERTS
    aff = jax.nn.softmax(jnp.einsum('btd,de->bte', h.astype(F32), router_w.astype(F32)), axis=-1)
    gate, idx = lax.top_k(jnp.swapaxes(aff, 1, 2), cap)
    xe = jax.vmap(lambda hb, ib: hb[ib])(h, idx)
    hid = jax.nn.silu(jnp.einsum('becd,edf->becf', xe, w_gate)) * jnp.einsum('becd,edf->becf', xe, w_up)
    ye = jnp.einsum('becf,efd->becd', hid, w_down) * gate[..., None].astype(h.dtype)
    return jax.vmap(lambda ib, yb: jnp.zeros((T, D), yb.dtype).at[ib.reshape(-1)].add(yb.reshape(-1, D)))(idx, ye)


def setup_inputs(seed: int = 0) -> dict:
    key = jax.random.key(seed)
    ks = iter(jax.random.split(key, 40))

    def nrm(shape, s):
        return jax.random.normal(next(ks), shape, F32) * s

    def gain(shape):
        return 1.0 + nrm(shape, 0.05)

    def dt_bias(shape):
        dt = jnp.exp(jax.random.uniform(next(ks), shape, F32, np.log(1e-3), np.log(1e-1)))
        return dt + jnp.log(-jnp.expm1(-dt))

    def a_log(shape):
        return jnp.log(jax.random.uniform(next(ks), shape, F32, 1.0, 16.0))

    L = DEPTH
    return {
        "x": nrm((BATCH, SEQ, D_MODEL), 1.0),
        "c": nrm((BATCH, D_MODEL), 1.0),
        "ctx": nrm((BATCH, CTX_LEN, D_MODEL), 1.0),
        "c_ctx": nrm((D_MODEL,), 1.0),
        "norm1_g": gain((L, D_MODEL)),
        "norm2_g": gain((L, D_MODEL)),
        "ada_w": nrm((L, D_MODEL, 6 * D_MODEL), 0.5 * D_MODEL ** -0.5),
        "ada_b": nrm((L, 6 * D_MODEL), 0.02),
        "w_in": nrm((L, D_MODEL, IN_DIM), D_MODEL ** -0.5),
        "w_out": nrm((L, D_MIX, D_MODEL), D_MIX ** -0.5),
        "pool_w": nrm((L, len(POOL_WINDOWS), POOL_GROUP, POOL_GROUP), POOL_GROUP ** -0.5),
        "pool_scale": gain((L, POOL_DIM)),
        "ssd_conv_w": nrm((L, CONV_W, SSD_DIM + 2 * SSD_BC), CONV_W ** -0.5),
        "ssd_conv_b": nrm((L, SSD_DIM + 2 * SSD_BC), 0.02),
        "ssd_a_log": a_log((L, 2, SSD_HEADS)),
        "ssd_dt_bias": dt_bias((L, 2, SSD_HEADS)),
        "ssd_d": gain((L, SSD_HEADS)),
        "ssd_norm_g": gain((L, SSD_DIM)),
        "gdn_conv_w": nrm((L, CONV_W, 3 * GDN_DIM), CONV_W ** -0.5),
        "gdn_a_log": a_log((L, 2, GDN_HEADS)),
        "gdn_dt_bias": dt_bias((L, 2, GDN_HEADS)),
        "gdn_norm_g": gain((L, GDN_HEAD_DIM)),
        "router_w": nrm((L, D_MODEL, N_EXPERTS), D_MODEL ** -0.5),
        "exp_w_gate": nrm((L, N_EXPERTS, D_MODEL, EXPERT_FF), D_MODEL ** -0.5),
        "exp_w_up": nrm((L, N_EXPERTS, D_MODEL, EXPERT_FF), D_MODEL ** -0.5),
        "exp_w_down": nrm((L, N_EXPERTS, EXPERT_FF, D_MODEL), EXPERT_FF ** -0.5),
        "final_norm_g": gain((D_MODEL,)),
    }


def reference(x, c, ctx, c_ctx, norm1_g, norm2_g, ada_w, ada_b, w_in, w_out, pool_w, pool_scale, ssd_conv_w,
              ssd_conv_b, ssd_a_log, ssd_dt_bias, ssd_d, ssd_norm_g, gdn_conv_w, gdn_a_log, gdn_dt_bias, gdn_norm_g,
              router_w, exp_w_gate, exp_w_up, exp_w_down, final_norm_g):
    sc = jax.nn.silu(c)
    scc = jax.nn.silu(c_ctx)[None]
    for l in range(DEPTH):
        last = l == DEPTH - 1
        m_lat = jnp.split(sc @ ada_w[l] + ada_b[l], 6, axis=-1)
        m_ctx = jnp.split(scc @ ada_w[l] + ada_b[l], 6, axis=-1)
        h_lat = modulate(rms_norm(x, norm1_g[l]), m_lat[0], m_lat[1])
        h_ctx = modulate(rms_norm(ctx, norm1_g[l]), m_ctx[0], m_ctx[1])
        mix_ctx, mix_lat = hybrid_mixer(h_ctx, h_lat, w_in[l], w_out[l], pool_w[l], pool_scale[l], ssd_conv_w[l],
                                        ssd_conv_b[l], ssd_a_log[l], ssd_dt_bias[l], ssd_d[l], ssd_norm_g[l],
                                        gdn_conv_w[l], gdn_a_log[l], gdn_dt_bias[l], gdn_norm_g[l], not last)
        x = x + m_lat[2][:, None] * mix_lat
        h_lat = modulate(rms_norm(x, norm2_g[l]), m_lat[3], m_lat[4])
        x = x + m_lat[5][:, None] * expert_choice_ffn(h_lat, router_w[l], exp_w_gate[l], exp_w_up[l], exp_w_down[l])
        if not last:
            ctx = ctx + m_ctx[2][:, None] * mix_ctx
            h_ctx = modulate(rms_norm(ctx, norm2_g[l]), m_ctx[3], m_ctx[4])
            ctx = ctx + m_ctx[5][:, None] * expert_choice_ffn(h_ctx, router_w[l], exp_w_gate[l], exp_w_up[l],
                                                               exp_w_down[l])
    return rms_norm(x, final_norm_g)
```

```python
import functools

import numpy as np
import jax
import jax.numpy as jnp
from jax import lax
from jax.experimental import pallas as pl
from jax.experimental.pallas import tpu as pltpu

F32 = jnp.float32
BF16 = jnp.bfloat16

GRID_W = 64
CHUNK = 64
CONV_W = 5
POOL_WINDOWS = (2, 4, 8, 16)
SSD_HEAD_DIM = 64
SSD_GROUPS = 2
SSD_STATE = 128
GDN_HEAD_DIM = 64
N_EXPERTS = 16
EC_CAPACITY = 2
EPS = 1e-6
LANE = 128


def _round_up(n, m):
    return (n + m - 1) // m * m


def _inproj_kernel(x_ref, sc_ref, sh_ref, g_ref, w_ref, o_ref):
    x = x_ref[0]
    ms = jnp.mean(x * x, axis=-1, keepdims=True)
    h = x * lax.rsqrt(ms + EPS) * g_ref[...]
    h = h * (1.0 + sc_ref[0]) + sh_ref[0]
    o_ref[0] = jnp.dot(h.astype(BF16), w_ref[...], preferred_element_type=F32)


def _inproj(x, scale, shift, g, w, *, column_major):
    b, t, d = x.shape
    n = w.shape[1]
    if column_major:
        rows = t // GRID_W
        xv = x.reshape(b, rows, GRID_W * d)
        grid = (b, GRID_W)
        x_spec = pl.BlockSpec((1, rows, d), lambda i, j: (i, 0, j))
        tm = rows
    else:
        tm = min(t, 512)
        xv = x
        grid = (b, t // tm)
        x_spec = pl.BlockSpec((1, tm, d), lambda i, j: (i, j, 0))
    return pl.pallas_call(
        _inproj_kernel,
        grid=grid,
        in_specs=[
            x_spec,
            pl.BlockSpec((1, 1, d), lambda i, j: (i, 0, 0)),
            pl.BlockSpec((1, 1, d), lambda i, j: (i, 0, 0)),
            pl.BlockSpec((1, d), lambda i, j: (0, 0)),
            pl.BlockSpec((d, n), lambda i, j: (0, 0)),
        ],
        out_specs=pl.BlockSpec((1, tm, n), lambda i, j: (i, j, 0)),
        out_shape=jax.ShapeDtypeStruct((b, t, n), F32),
        compiler_params=pltpu.CompilerParams(
            dimension_semantics=("arbitrary", "arbitrary"), vmem_limit_bytes=48 * 1024 * 1024),
        name="inproj",
    )(xv, scale, shift, g, w)


def _outproj_kernel(a_ref, x_ref, gate_ref, w_ref, o_ref):
    y = jnp.dot(a_ref[0].astype(BF16), w_ref[...], preferred_element_type=F32)
    o_ref[0] = x_ref[0] + gate_ref[0] * y


def _outproj(a, x, gate, w):
    b, t, d = x.shape
    k = a.shape[-1]
    tm = min(t, 512)
    return pl.pallas_call(
        _outproj_kernel,
        grid=(b, t // tm),
        in_specs=[
            pl.BlockSpec((1, tm, k), lambda i, j: (i, j, 0)),
            pl.BlockSpec((1, tm, d), lambda i, j: (i, j, 0)),
            pl.BlockSpec((1, 1, d), lambda i, j: (i, 0, 0)),
            pl.BlockSpec((k, d), lambda i, j: (0, 0)),
        ],
        out_specs=pl.BlockSpec((1, tm, d), lambda i, j: (i, j, 0)),
        out_shape=jax.ShapeDtypeStruct((b, t, d), F32),
        compiler_params=pltpu.CompilerParams(
            dimension_semantics=("arbitrary", "arbitrary"), vmem_limit_bytes=48 * 1024 * 1024),
        name="outproj",
    )(a, x, gate, w)


def _ffn_kernel(x_ref, gate_ref, wg_ref, wu_ref, wd_ref, o_ref):
    x = x_ref[0].astype(BF16)
    hg = jnp.dot(x, wg_ref[0], preferred_element_type=F32)
    hu = jnp.dot(x, wu_ref[0], preferred_element_type=F32)
    hid = (hg * jax.nn.sigmoid(hg)) * hu
    y = jnp.dot(hid.astype(BF16), wd_ref[0], preferred_element_type=F32)
    o_ref[0] = y * gate_ref[0]


def _expert_ffn(xe, gate, wg, wu, wd):
    be, c, d = xe.shape
    e, _, f = wg.shape
    tc = min(c, 512)
    return pl.pallas_call(
        _ffn_kernel,
        grid=(be, c // tc),
        in_specs=[
            pl.BlockSpec((1, tc, d), lambda i, j: (i, j, 0)),
            pl.BlockSpec((1, tc, 1), lambda i, j: (i, j, 0)),
            pl.BlockSpec((1, d, f), lambda i, j: (i % e, 0, 0)),
            pl.BlockSpec((1, d, f), lambda i, j: (i % e, 0, 0)),
            pl.BlockSpec((1, f, d), lambda i, j: (i % e, 0, 0)),
        ],
        out_specs=pl.BlockSpec((1, tc, d), lambda i, j: (i, j, 0)),
        out_shape=jax.ShapeDtypeStruct((be, c, d), F32),
        compiler_params=pltpu.CompilerParams(
            dimension_semantics=("arbitrary", "arbitrary"), vmem_limit_bytes=48 * 1024 * 1024),
        name="expert_ffn",
    )(xe, gate, wg, wu, wd)


def _rms_norm(x, g):
    return x * lax.rsqrt(jnp.mean(x * x, axis=-1, keepdims=True) + EPS) * g


def _l2_norm(a):
    return a * lax.rsqrt(jnp.sum(a * a, axis=-1, keepdims=True) + EPS)


def _flip_t(a):
    return jnp.flip(a, axis=1)


def _to_raster(u):
    b, t, c = u.shape
    rows = t // GRID_W
    return u.reshape(b, GRID_W, rows, c).transpose(0, 2, 1, 3).reshape(b, t, c)


def _to_chunks(a):
    b, t = a.shape[:2]
    a = a.reshape((b, t // CHUNK, CHUNK) + a.shape[2:])
    return jnp.moveaxis(a, 2, 3)


def _dwconv(u, w):
    k, ch = w.shape
    return lax.conv_general_dilated(u, w[:, None, :], window_strides=(1,), padding=[(k // 2, k // 2)],
                                    dimension_numbers=('NWC', 'WIO', 'NWC'), feature_group_count=ch)


def _pool_branch(u, pool_w, pool_scale):
    b, t, pool_dim = u.shape
    pg = pool_dim // len(POOL_WINDOWS)
    csum = jnp.concatenate([jnp.zeros((b, 1, pool_dim), F32), jnp.cumsum(u, axis=1)], axis=1)
    tt = jnp.arange(t)
    groups = []
    for gi, win in enumerate(POOL_WINDOWS):
        lo = jnp.clip(tt - win // 2, 0, t)
        hi = jnp.clip(tt + win // 2, 0, t)
        sl = slice(gi * pg, (gi + 1) * pg)
        cg = csum[:, :, sl]
        mean = (cg[:, hi] - cg[:, lo]) / (hi - lo).astype(F32)[None, :, None]
        groups.append(mean - u[:, :, sl])
    dd = jnp.stack(groups, axis=2)
    return jnp.einsum('btgc,gcd->btgd', dd, pool_w).reshape(b, t, pool_dim) * pool_scale


def _ssd_chunked(xdt, log_a, bm, cm, s0):
    b, t, h, p = xdt.shape
    xc, bc, cc = _to_chunks(xdt), _to_chunks(bm), _to_chunks(cm)
    ac = jnp.cumsum(_to_chunks(log_a), axis=-1)
    lower = jnp.tril(jnp.ones((CHUNK, CHUNK), bool))
    decay = jnp.exp(jnp.where(lower, ac[..., :, None] - ac[..., None, :], -jnp.inf))
    y_intra = jnp.einsum('bchij,bchjp->bchip', jnp.einsum('bchis,bchjs->bchij', cc, bc) * decay, xc)
    states = jnp.einsum('bchjs,bchjp->bchsp', bc * jnp.exp(ac[..., -1:] - ac)[..., None], xc)
    chunk_decay = jnp.exp(ac[..., -1])

    def step(s, inp):
        st, dc = inp
        return s * dc[..., None, None] + st, s

    s_fin, s_start = lax.scan(step, s0, (jnp.moveaxis(states, 1, 0), jnp.moveaxis(chunk_decay, 1, 0)))
    s_start = jnp.moveaxis(s_start, 0, 1)
    y_inter = jnp.einsum('bchis,bchsp->bchip', cc * jnp.exp(ac)[..., None], s_start)
    y = jnp.moveaxis(y_intra + y_inter, 2, 3).reshape(b, t, h, p)
    return y, s_fin


def _gdn_chunked(q, k, v, g, beta, s0):
    b, t, h, _ = q.shape
    vd = v.shape[-1]
    qc, kc, vc = _to_chunks(q), _to_chunks(k), _to_chunks(v)
    gc = jnp.cumsum(_to_chunks(g), axis=-1)
    bc = _to_chunks(beta)
    lower = jnp.tril(jnp.ones((CHUNK, CHUNK), bool))
    strict = jnp.tril(jnp.ones((CHUNK, CHUNK), bool), -1)
    decay = jnp.exp(jnp.where(lower, gc[..., :, None] - gc[..., None, :], -jnp.inf))
    kb = kc * bc[..., None]
    xm = jnp.where(strict, jnp.einsum('bnhik,bnhjk->bnhij', kb, kc) * decay, 0.0)
    eye = jnp.eye(CHUNK, dtype=xm.dtype)
    tm = lax.linalg.triangular_solve(xm + eye, jnp.broadcast_to(eye, xm.shape), left_side=True, lower=True,
                                     unit_diagonal=True)
    u = jnp.einsum('bnhij,bnhjv->bnhiv', tm, vc * bc[..., None])
    w = jnp.einsum('bnhij,bnhjk->bnhik', tm, kb * jnp.exp(gc)[..., None])
    attn = jnp.einsum('bnhik,bnhjk->bnhij', qc, kc) * decay
    qg = qc * jnp.exp(gc)[..., None]
    kend = kc * jnp.exp(gc[..., -1:] - gc)[..., None]
    gend = jnp.exp(gc[..., -1])

    def step(s, inp):
        u_i, w_i, a_i, qg_i, ke_i, ge_i = inp
        v_new = u_i - jnp.einsum('bhlk,bhkv->bhlv', w_i, s)
        o_i = jnp.einsum('bhlk,bhkv->bhlv', qg_i, s) + jnp.einsum('bhij,bhjv->bhiv', a_i, v_new)
        s = s * ge_i[..., None, None] + jnp.einsum('bhlk,bhlv->bhkv', ke_i, v_new)
        return s, o_i

    xs = (jnp.moveaxis(u, 1, 0), jnp.moveaxis(w, 1, 0), jnp.moveaxis(attn, 1, 0),
          jnp.moveaxis(qg, 1, 0), jnp.moveaxis(kend, 1, 0), jnp.moveaxis(gend, 1, 0))
    s_fin, o = lax.scan(step, s0, xs)
    o = jnp.moveaxis(jnp.moveaxis(o, 0, 1), 2, 3).reshape(b, t, h, vd)
    return o, s_fin


def _ssd_stream(z, xbc, dt_raw, conv_w, conv_b, a_log, dt_bias, d_skip, norm_g, s0_f, s0_b):
    b, t, ssd_dim = z.shape
    heads = ssd_dim // SSD_HEAD_DIM
    ssd_bc = SSD_GROUPS * SSD_STATE
    xbc = jax.nn.silu(_dwconv(xbc, conv_w) + conv_b)
    xs, bm, cm = jnp.split(xbc, [ssd_dim, ssd_dim + ssd_bc], axis=-1)
    xs = xs.reshape(b, t, heads, SSD_HEAD_DIM)
    rep = heads // SSD_GROUPS
    bm = jnp.repeat(bm.reshape(b, t, SSD_GROUPS, SSD_STATE), rep, axis=2)
    cm = jnp.repeat(cm.reshape(b, t, SSD_GROUPS, SSD_STATE), rep, axis=2)
    dt = jax.nn.softplus(dt_raw.reshape(b, t, 2, heads) + dt_bias)
    a = -jnp.exp(a_log)
    y_f, s_f = _ssd_chunked(xs * dt[:, :, 0, :, None], dt[:, :, 0] * a[0], bm, cm, s0_f)
    y_b, s_b = _ssd_chunked(_flip_t(xs * dt[:, :, 1, :, None]), _flip_t(dt[:, :, 1] * a[1]), _flip_t(bm),
                            _flip_t(cm), s0_b)
    y = y_f + _flip_t(y_b) + d_skip[:, None] * xs
    y = y.reshape(b, t, ssd_dim) * jax.nn.silu(z)
    return _rms_norm(y, norm_g), s_f, s_b


def _gdn_stream(qkv, gate, a_raw, b_raw, conv_w, a_log, dt_bias, norm_g, s0_f, s0_b):
    b, t, _ = qkv.shape
    gdn_dim = gate.shape[-1]
    heads = gdn_dim // GDN_HEAD_DIM
    qkv = jax.nn.silu(_dwconv(qkv, conv_w))
    q, k, v = [u.reshape(b, t, heads, GDN_HEAD_DIM) for u in jnp.split(qkv, 3, axis=-1)]
    q = _l2_norm(q) * (GDN_HEAD_DIM ** -0.5)
    k = _l2_norm(k)
    g = -jnp.exp(a_log) * jax.nn.softplus(a_raw.reshape(b, t, 2, heads) + dt_bias)
    beta = jax.nn.sigmoid(b_raw.reshape(b, t, 2, heads))
    o_f, s_f = _gdn_chunked(q, k, v, g[:, :, 0], beta[:, :, 0], s0_f)
    o_b, s_b = _gdn_chunked(_flip_t(q), _flip_t(k), _flip_t(v), _flip_t(g[:, :, 1]), _flip_t(beta[:, :, 1]), s0_b)
    o = _rms_norm(o_f + _flip_t(o_b), norm_g).reshape(b, t, gdn_dim) * jax.nn.silu(gate)
    return o, s_f, s_b


def _expert_choice_ffn(h, router_w, wg, wu, wd):
    b, t, d = h.shape
    cap = EC_CAPACITY * t // N_EXPERTS
    aff = jax.nn.softmax(jnp.einsum('btd,de->bte', h, router_w), axis=-1)
    gate, idx = lax.top_k(jnp.swapaxes(aff, 1, 2), cap)
    xe = jax.vmap(lambda hb, ib: hb[ib])(h, idx)
    ye = _expert_ffn(xe.reshape(b * N_EXPERTS, cap, d), gate.reshape(b * N_EXPERTS, cap, 1), wg, wu, wd)
    ye = ye.reshape(b, N_EXPERTS, cap, d)
    return jax.vmap(lambda ib, yb: jnp.zeros((t, d), yb.dtype).at[ib.reshape(-1)].add(yb.reshape(-1, d)))(idx, ye)


def kernel(x, c, ctx, c_ctx, norm1_g, norm2_g, ada_w, ada_b, w_in, w_out, pool_w, pool_scale, ssd_conv_w, ssd_conv_b, ssd_a_log, ssd_dt_bias, ssd_d, ssd_norm_g, gdn_conv_w, gdn_a_log, gdn_dt_bias, gdn_norm_g, router_w, exp_w_gate, exp_w_up, exp_w_down, final_norm_g):
    depth, d, _ = w_in.shape
    b, t, _ = x.shape
    pool_dim = pool_scale.shape[-1]
    ssd_dim = ssd_norm_g.shape[-1]
    ssd_heads = ssd_dim // SSD_HEAD_DIM
    ssd_bc = SSD_GROUPS * SSD_STATE
    gdn_dim = gdn_conv_w.shape[-1] // 3
    gdn_heads = gdn_dim // GDN_HEAD_DIM
    splits = (pool_dim, ssd_dim, ssd_dim + 2 * ssd_bc, 2 * ssd_heads, 3 * gdn_dim, gdn_dim, 2 * gdn_heads,
              2 * gdn_heads)
    cut = np.cumsum(splits).tolist()
    n_r = cut[3]
    n_g = cut[7] - cut[3]
    n_rp, n_gp = _round_up(n_r, LANE), _round_up(n_g, LANE)

    sc = jax.nn.silu(c)
    scc = jax.nn.silu(c_ctx)[None]
    for l in range(depth):
        last = l == depth - 1
        w_r = jnp.pad(w_in[l][:, :n_r], ((0, 0), (0, n_rp - n_r))).astype(BF16)
        w_g = jnp.pad(w_in[l][:, n_r:], ((0, 0), (0, n_gp - n_g))).astype(BF16)
        w_o = w_out[l].astype(BF16)
        wg, wu, wd = exp_w_gate[l].astype(BF16), exp_w_up[l].astype(BF16), exp_w_down[l].astype(BF16)
        m_lat = jnp.split(sc @ ada_w[l] + ada_b[l], 6, axis=-1)
        m_ctx = jnp.split(scc @ ada_w[l] + ada_b[l], 6, axis=-1)
        g1 = norm1_g[l][None]

        def project(xx, mm, column_major):
            pr = _inproj(xx, mm[1][:, None], mm[0][:, None], g1, w_r, column_major=False)
            pg = _inproj(xx, mm[1][:, None], mm[0][:, None], g1, w_g, column_major=column_major)
            pr = jnp.split(pr[..., :n_r], cut[:3], axis=-1)
            pg = jnp.split(pg[..., :n_g], [cc - n_r for cc in cut[4:7]], axis=-1)
            return pr, pg

        ssd_p = (ssd_conv_w[l], ssd_conv_b[l], ssd_a_log[l], ssd_dt_bias[l], ssd_d[l], ssd_norm_g[l])
        gdn_p = (gdn_conv_w[l], gdn_a_log[l], gdn_dt_bias[l], gdn_norm_g[l])
        zs = jnp.zeros((b, ssd_heads, SSD_STATE, SSD_HEAD_DIM), F32)
        zg = jnp.zeros((b, gdn_heads, GDN_HEAD_DIM, GDN_HEAD_DIM), F32)

        ctx_b = jnp.broadcast_to(ctx, (b,) + ctx.shape[1:])
        mcb = [jnp.broadcast_to(m, (b, d)) for m in m_ctx]
        cr, cg = project(ctx_b, mcb, False)
        lr, lg = project(x, m_lat, True)

        s_ctx, ssd_sf, ssd_sb = _ssd_stream(cr[1], cr[2], cr[3], *ssd_p, zs, zs)
        g_ctx, gdn_sf, gdn_sb = _gdn_stream(cg[0], cg[1], cg[2], cg[3], *gdn_p, zg, zg)
        s_lat, _, _ = _ssd_stream(lr[1], lr[2], lr[3], *ssd_p, ssd_sf, ssd_sb)
        g_lat, _, _ = _gdn_stream(lg[0], lg[1], lg[2], lg[3], *gdn_p, gdn_sf, gdn_sb)
        g_lat = _to_raster(g_lat)
        a_lat = jnp.concatenate([_pool_branch(lr[0], pool_w[l], pool_scale[l]), s_lat, g_lat], axis=-1)
        x = _outproj(a_lat, x, m_lat[2][:, None], w_o)
        h2 = _rms_norm(x, norm2_g[l]) * (1 + m_lat[4][:, None]) + m_lat[3][:, None]
        x = x + m_lat[5][:, None] * _expert_choice_ffn(h2, router_w[l], wg, wu, wd)
        if not last:
            a_ctx = jnp.concatenate([_pool_branch(cr[0], pool_w[l], pool_scale[l]), s_ctx, g_ctx], axis=-1)
            ctx = _outproj(a_ctx, ctx_b, mcb[2][:, None], w_o)
            h2c = _rms_norm(ctx, norm2_g[l]) * (1 + mcb[4][:, None]) + mcb[3][:, None]
            ctx = ctx + mcb[5][:, None] * _expert_choice_ffn(h2c, router_w[l], wg, wu, wd)
    return _rms_norm(x, final_norm_g)
```

```python
import functools

import numpy as np
import jax
import jax.numpy as jnp
from jax import lax
from jax.experimental import pallas as pl
from jax.experimental.pallas import tpu as pltpu

F32 = jnp.float32
BF16 = jnp.bfloat16

GRID_W = 64
CHUNK = 64
CONV_W = 5
POOL_WINDOWS = (2, 4, 8, 16)
SSD_HEAD_DIM = 64
SSD_GROUPS = 2
SSD_STATE = 128
GDN_HEAD_DIM = 64
N_EXPERTS = 16
EC_CAPACITY = 2
EPS = 1e-6
LANE = 128
HALO = 8
NEG = -1e30
VMEM_LIMIT = 48 * 1024 * 1024


def _round_up(n, m):
    return (n + m - 1) // m * m


def _params():
    return pltpu.CompilerParams(dimension_semantics=("arbitrary", "arbitrary"), vmem_limit_bytes=VMEM_LIMIT)


def _inproj_kernel(x_ref, sc_ref, sh_ref, g_ref, w_ref, *o_refs):
    x = x_ref[0]
    ms = jnp.mean(x * x, axis=-1, keepdims=True)
    h = x * lax.rsqrt(ms + EPS) * g_ref[...]
    h = h * (1.0 + sc_ref[0]) + sh_ref[0]
    y = jnp.dot(h.astype(BF16), w_ref[...], preferred_element_type=F32)
    off = 0
    for o_ref in o_refs:
        n = o_ref.shape[-1]
        o_ref[0] = y[:, off:off + n]
        off += n


def _inproj(x, scale, shift, g, w, widths, *, column_major):
    b, t, d = x.shape
    if column_major:
        rows = t // GRID_W
        xv = x.reshape(b, rows, GRID_W * d)
        grid = (b, GRID_W)
        x_spec = pl.BlockSpec((1, rows, d), lambda i, j: (i, 0, j))
        tm = rows
    else:
        tm = min(t, 512)
        xv = x
        grid = (b, t // tm)
        x_spec = pl.BlockSpec((1, tm, d), lambda i, j: (i, j, 0))
    return pl.pallas_call(
        _inproj_kernel,
        grid=grid,
        in_specs=[
            x_spec,
            pl.BlockSpec((1, 1, d), lambda i, j: (i, 0, 0)),
            pl.BlockSpec((1, 1, d), lambda i, j: (i, 0, 0)),
            pl.BlockSpec((1, d), lambda i, j: (0, 0)),
            pl.BlockSpec((d, w.shape[1]), lambda i, j: (0, 0)),
        ],
        out_specs=[pl.BlockSpec((1, tm, n), lambda i, j: (i, j, 0)) for n in widths],
        out_shape=[jax.ShapeDtypeStruct((b, t, n), F32) for n in widths],
        compiler_params=_params(),
        name="inproj",
    )(xv, scale, shift, g, w)


def _outproj_kernel(a_ref, x_ref, gate_ref, w_ref, o_ref):
    y = jnp.dot(a_ref[0].astype(BF16), w_ref[...], preferred_element_type=F32)
    o_ref[0] = x_ref[0] + gate_ref[0] * y


def _outproj(a, x, gate, w):
    b, t, d = x.shape
    k = a.shape[-1]
    tm = min(t, 512)
    return pl.pallas_call(
        _outproj_kernel,
        grid=(b, t // tm),
        in_specs=[
            pl.BlockSpec((1, tm, k), lambda i, j: (i, j, 0)),
            pl.BlockSpec((1, tm, d), lambda i, j: (i, j, 0)),
            pl.BlockSpec((1, 1, d), lambda i, j: (i, 0, 0)),
            pl.BlockSpec((k, d), lambda i, j: (0, 0)),
        ],
        out_specs=pl.BlockSpec((1, tm, d), lambda i, j: (i, j, 0)),
        out_shape=jax.ShapeDtypeStruct((b, t, d), F32),
        compiler_params=_params(),
        name="outproj",
    )(a, x, gate, w)


def _ffn_kernel(x_ref, gate_ref, wg_ref, wu_ref, wd_ref, o_ref):
    x = x_ref[0].astype(BF16)
    hg = jnp.dot(x, wg_ref[0], preferred_element_type=F32)
    hu = jnp.dot(x, wu_ref[0], preferred_element_type=F32)
    hid = (hg * jax.nn.sigmoid(hg)) * hu
    y = jnp.dot(hid.astype(BF16), wd_ref[0], preferred_element_type=F32)
    o_ref[0] = y * gate_ref[0]


def _expert_ffn(xe, gate, wg, wu, wd):
    be, c, d = xe.shape
    e, _, f = wg.shape
    tc = min(c, 512)
    return pl.pallas_call(
        _ffn_kernel,
        grid=(be, c // tc),
        in_specs=[
            pl.BlockSpec((1, tc, d), lambda i, j: (i, j, 0)),
            pl.BlockSpec((1, tc, 1), lambda i, j: (i, j, 0)),
            pl.BlockSpec((1, d, f), lambda i, j: (i % e, 0, 0)),
            pl.BlockSpec((1, d, f), lambda i, j: (i % e, 0, 0)),
            pl.BlockSpec((1, f, d), lambda i, j: (i % e, 0, 0)),
        ],
        out_specs=pl.BlockSpec((1, tc, d), lambda i, j: (i, j, 0)),
        out_shape=jax.ShapeDtypeStruct((be, c, d), F32),
        compiler_params=_params(),
        name="expert_ffn",
    )(xe, gate, wg, wu, wd)


def _split3(a):
    hi = a.astype(BF16)
    r1 = a - hi.astype(F32)
    mid = r1.astype(BF16)
    lo = (r1 - mid.astype(F32)).astype(BF16)
    return hi, mid, lo


def _cumsum_both(la, incl):
    parts = _split3(la)
    cs_col = sum(jnp.dot(incl, p, preferred_element_type=F32) for p in parts)
    cs_row = sum(lax.dot_general(p, incl, (((0,), (1,)), ((), ())), preferred_element_type=F32) for p in parts)
    return cs_col, cs_row


def _ssd_kernel(*refs, chunk, rev, final, heads):
    if final:
        (xbc_ref, prev_ref, next_ref, dt_ref, cw_ref, cb_ref, dtb_ref, a_ref, dsk_ref, s0_ref,
         yb_ref, z_ref, ng_ref, y_ref, sfin_ref, s_ref, ext_ref) = refs
    else:
        (xbc_ref, prev_ref, next_ref, dt_ref, cw_ref, cb_ref, dtb_ref, a_ref, dsk_ref, s0_ref,
         y_ref, sfin_ref, s_ref, ext_ref) = refs
    L = chunk
    i = pl.program_id(1)
    nc = pl.num_programs(1)
    j = nc - 1 - i if rev else i
    hd = SSD_HEAD_DIM
    ssd_dim = heads * hd
    rep = heads // SSD_GROUPS

    @pl.when(i == 0)
    def _():
        s_ref[...] = s0_ref[0]

    ext_ref[0:HALO, :] = jnp.where(j > 0, prev_ref[0], 0.0)
    ext_ref[HALO:HALO + L, :] = xbc_ref[0]
    ext_ref[HALO + L:, :] = jnp.where(j < nc - 1, next_ref[0], 0.0)
    base = HALO - CONV_W // 2
    acc = cb_ref[...] + cw_ref[0:1, :] * ext_ref[base:base + L, :]
    for k in range(1, CONV_W):
        acc = acc + cw_ref[k:k + 1, :] * ext_ref[base + k:base + k + L, :]
    act = acc * jax.nn.sigmoid(acc)
    xs = act[:, :ssd_dim]
    bmat = act[:, ssd_dim:ssd_dim + SSD_GROUPS * SSD_STATE]
    cmat = act[:, ssd_dim + SSD_GROUPS * SSD_STATE:]

    dtv = jax.nn.softplus(dt_ref[0] + dtb_ref[...])
    la = dtv * a_ref[...]
    row = lax.broadcasted_iota(jnp.int32, (L, L), 0)
    col = lax.broadcasted_iota(jnp.int32, (L, L), 1)
    mask = (row <= col) if rev else (row >= col)
    cs_col, cs_row = _cumsum_both(la, mask.astype(BF16))
    last = 0 if rev else L - 1

    gmats = []
    for g in range(SSD_GROUPS):
        cg = cmat[:, g * SSD_STATE:(g + 1) * SSD_STATE].astype(BF16)
        bg = bmat[:, g * SSD_STATE:(g + 1) * SSD_STATE].astype(BF16)
        gmats.append(lax.dot_general(cg, bg, (((1,), (1,)), ((), ())), preferred_element_type=F32))

    ys = []
    for h in range(heads):
        g = h // rep
        ci = (heads if rev else 0) + h
        csc = cs_col[:, ci:ci + 1]
        csr = cs_row[ci:ci + 1, :]
        tot = csc[last:last + 1, :]
        dec = jnp.exp(jnp.where(mask, csc - csr, NEG))
        xs_h = xs[:, h * hd:(h + 1) * hd]
        xdt = (xs_h * dtv[:, ci:ci + 1]).astype(BF16)
        b_g = bmat[:, g * SSD_STATE:(g + 1) * SSD_STATE]
        c_g = cmat[:, g * SSD_STATE:(g + 1) * SSD_STATE]
        s_h = s_ref[h]
        y_h = jnp.dot((gmats[g] * dec).astype(BF16), xdt, preferred_element_type=F32)
        y_h = y_h + jnp.dot((c_g * jnp.exp(csc)).astype(BF16), s_h.astype(BF16), preferred_element_type=F32)
        local = lax.dot_general((b_g * jnp.exp(tot - csc)).astype(BF16), xdt, (((0,), (0,)), ((), ())),
                                preferred_element_type=F32)
        s_ref[h] = s_h * jnp.exp(tot) + local
        if final:
            y_h = y_h + dsk_ref[:, h * hd:(h + 1) * hd] * xs_h
        ys.append(y_h)
    y = jnp.concatenate(ys, axis=-1)
    if final:
        y = y + yb_ref[0]
        z = z_ref[0]
        y = y * (z * jax.nn.sigmoid(z))
        y = y * lax.rsqrt(jnp.mean(y * y, axis=-1, keepdims=True) + EPS) * ng_ref[...]
    y_ref[0] = y

    @pl.when(i == nc - 1)
    def _():
        sfin_ref[0] = s_ref[...]


def _ssd_pass(xbc, dt, cw, cb, dtb, a, dsk, s0, *, rev, final_inputs=None):
    b, t, width = xbc.shape
    heads = s0.shape[1]
    ssd_dim = heads * SSD_HEAD_DIM
    L = min(t, 256)
    nc = t // L
    hb = L // HALO
    final = final_inputs is not None

    def cidx(i):
        return nc - 1 - i if rev else i

    in_specs = [
        pl.BlockSpec((1, L, width), lambda bi, i: (bi, cidx(i), 0)),
        pl.BlockSpec((1, HALO, width), lambda bi, i: (bi, jnp.maximum(cidx(i) * hb - 1, 0), 0)),
        pl.BlockSpec((1, HALO, width), lambda bi, i: (bi, jnp.minimum((cidx(i) + 1) * hb, t // HALO - 1), 0)),
        pl.BlockSpec((1, L, LANE), lambda bi, i: (bi, cidx(i), 0)),
        pl.BlockSpec((8, width), lambda bi, i: (0, 0)),
        pl.BlockSpec((1, width), lambda bi, i: (0, 0)),
        pl.BlockSpec((1, LANE), lambda bi, i: (0, 0)),
        pl.BlockSpec((1, LANE), lambda bi, i: (0, 0)),
        pl.BlockSpec((1, ssd_dim), lambda bi, i: (0, 0)),
        pl.BlockSpec((1, heads, SSD_STATE, SSD_HEAD_DIM), lambda bi, i: (bi, 0, 0, 0)),
    ]
    args = [xbc, xbc, xbc, dt, cw, cb, dtb, a, dsk, s0]
    if final:
        yb, z, ng = final_inputs
        in_specs += [
            pl.BlockSpec((1, L, ssd_dim), lambda bi, i: (bi, cidx(i), 0)),
            pl.BlockSpec((1, L, ssd_dim), lambda bi, i: (bi, cidx(i), 0)),
            pl.BlockSpec((1, ssd_dim), lambda bi, i: (0, 0)),
        ]
        args += [yb, z, ng]
    return pl.pallas_call(
        functools.partial(_ssd_kernel, chunk=L, rev=rev, final=final, heads=heads),
        grid=(b, nc),
        in_specs=in_specs,
        out_specs=[
            pl.BlockSpec((1, L, ssd_dim), lambda bi, i: (bi, cidx(i), 0)),
            pl.BlockSpec((1, heads, SSD_STATE, SSD_HEAD_DIM), lambda bi, i: (bi, 0, 0, 0)),
        ],
        out_shape=[
            jax.ShapeDtypeStruct((b, t, ssd_dim), F32),
            jax.ShapeDtypeStruct((b, heads, SSD_STATE, SSD_HEAD_DIM), F32),
        ],
        scratch_shapes=[
            pltpu.VMEM((heads, SSD_STATE, SSD_HEAD_DIM), F32),
            pltpu.VMEM((L + 2 * HALO, width), F32),
        ],
        compiler_params=_params(),
        name="ssd_bwd" if rev else "ssd_fwd",
    )(*args)


def _ssd_stream(z, xbc, dt, conv_w, conv_b, a_log, dt_bias, d_skip, norm_g, s0_f, s0_b):
    heads = s0_f.shape[1]
    cw = jnp.pad(conv_w, ((0, 8 - CONV_W), (0, 0)))
    cb = conv_b[None]
    dtb = jnp.pad(dt_bias.reshape(1, -1), ((0, 0), (0, LANE - 2 * heads)))
    a = jnp.pad(-jnp.exp(a_log).reshape(1, -1), ((0, 0), (0, LANE - 2 * heads)))
    dsk = jnp.repeat(d_skip, SSD_HEAD_DIM)[None]
    yb, s_b = _ssd_pass(xbc, dt, cw, cb, dtb, a, dsk, s0_b, rev=True)
    y, s_f = _ssd_pass(xbc, dt, cw, cb, dtb, a, dsk, s0_f, rev=False, final_inputs=(yb, z, norm_g[None]))
    return y, s_f, s_b


def _group_sum(a, ones_bd):
    hi = a.astype(BF16)
    lo = (a - hi.astype(F32)).astype(BF16)
    return (jnp.dot(hi, ones_bd, preferred_element_type=F32) + jnp.dot(lo, ones_bd, preferred_element_type=F32))


def _unit_tri_inverse(xm):
    n = xm.shape[0]
    eye = (lax.broadcasted_iota(jnp.int32, (n, n), 0) == lax.broadcasted_iota(jnp.int32, (n, n), 1)).astype(F32)
    r = eye - xm
    p = xm
    k = 1
    while 2 * k < n:
        pb = p.astype(BF16)
        p = jnp.dot(pb, pb, preferred_element_type=F32)
        r = r + jnp.dot(r.astype(BF16), p.astype(BF16), preferred_element_type=F32)
        k *= 2
    return r


def _gdn_kernel(*refs, block, rev, final, heads):
    if final:
        (qkv_ref, prev_ref, next_ref, ab_ref, cw_ref, dtb_ref, a_ref, s0_ref, ob_ref, gate_ref, ng_ref,
         o_ref, sfin_ref, s_ref, ext_ref, q_ref, k_ref, v_ref, g_ref, beta_ref) = refs
    else:
        (qkv_ref, prev_ref, next_ref, ab_ref, cw_ref, dtb_ref, a_ref, s0_ref,
         o_ref, sfin_ref, s_ref, ext_ref, q_ref, k_ref, v_ref, g_ref, beta_ref) = refs
    TB = block
    L = CHUNK
    hd = GDN_HEAD_DIM
    dim = heads * hd
    i = pl.program_id(1)
    nb = pl.num_programs(1)
    j = nb - 1 - i if rev else i

    @pl.when(i == 0)
    def _():
        s_ref[...] = s0_ref[0]

    ext_ref[0:HALO, :] = jnp.where(j > 0, prev_ref[0], 0.0)
    ext_ref[HALO:HALO + TB, :] = qkv_ref[0]
    ext_ref[HALO + TB:, :] = jnp.where(j < nb - 1, next_ref[0], 0.0)
    base = HALO - CONV_W // 2
    acc = cw_ref[0:1, :] * ext_ref[base:base + TB, :]
    for kk in range(1, CONV_W):
        acc = acc + cw_ref[kk:kk + 1, :] * ext_ref[base + kk:base + kk + TB, :]
    act = acc * jax.nn.sigmoid(acc)
    q = act[:, :dim]
    k = act[:, dim:2 * dim]
    v = act[:, 2 * dim:]
    ri = lax.broadcasted_iota(jnp.int32, (dim, dim), 0) // hd
    ci_ = lax.broadcasted_iota(jnp.int32, (dim, dim), 1) // hd
    ones_bd = (ri == ci_).astype(BF16)
    q_ref[...] = q * lax.rsqrt(_group_sum(q * q, ones_bd) + EPS) * (hd ** -0.5)
    k_ref[...] = k * lax.rsqrt(_group_sum(k * k, ones_bd) + EPS)
    v_ref[...] = v
    ab = ab_ref[0]
    g_ref[...] = a_ref[...] * jax.nn.softplus(ab + dtb_ref[...])
    beta_ref[...] = jax.nn.sigmoid(ab)

    row = lax.broadcasted_iota(jnp.int32, (L, L), 0)
    col = lax.broadcasted_iota(jnp.int32, (L, L), 1)
    incl = (row <= col) if rev else (row >= col)
    strict = (row < col) if rev else (row > col)
    incl_bf = incl.astype(BF16)
    last = 0 if rev else L - 1
    nchunk = TB // L

    def chunk_step(cc, carry):
        c = (nchunk - 1 - cc) if rev else cc
        r0 = pl.multiple_of(c * L, L)
        gq = q_ref[pl.ds(r0, L), :]
        gk = k_ref[pl.ds(r0, L), :]
        gv = v_ref[pl.ds(r0, L), :]
        gg = g_ref[pl.ds(r0, L), :]
        gb = beta_ref[pl.ds(r0, L), :]
        cs_col, cs_row = _cumsum_both(gg, incl_bf)
        outs = []
        for h in range(heads):
            ci = (heads if rev else 0) + h
            csc = cs_col[:, ci:ci + 1]
            csr = cs_row[ci:ci + 1, :]
            tot = csc[last:last + 1, :]
            dec = jnp.exp(jnp.where(incl, csc - csr, NEG))
            bcol = gb[:, 2 * heads + ci:2 * heads + ci + 1]
            qh = gq[:, h * hd:(h + 1) * hd]
            kh = gk[:, h * hd:(h + 1) * hd]
            vh = gv[:, h * hd:(h + 1) * hd]
            khb = kh.astype(BF16)
            qk_kk = lax.dot_general(jnp.concatenate([qh.astype(BF16), khb], axis=0), khb,
                                    (((1,), (1,)), ((), ())), preferred_element_type=F32)
            attn = qk_kk[:L] * dec
            xm = jnp.where(strict, bcol * qk_kk[L:] * dec, 0.0)
            tm = _unit_tri_inverse(xm)
            egc = jnp.exp(csc)
            rhs = jnp.concatenate([vh * bcol, kh * (bcol * egc)], axis=1).astype(BF16)
            uw = jnp.dot(tm.astype(BF16), rhs, preferred_element_type=F32)
            s_h = s_ref[h]
            wq = jnp.concatenate([uw[:, hd:], qh * egc], axis=0).astype(BF16)
            rs = jnp.dot(wq, s_h.astype(BF16), preferred_element_type=F32)
            v_new = uw[:, :hd] - rs[:L]
            vnb = v_new.astype(BF16)
            o_h = rs[L:] + jnp.dot(attn.astype(BF16), vnb, preferred_element_type=F32)
            kend = (kh * jnp.exp(tot - csc)).astype(BF16)
            s_ref[h] = s_h * jnp.exp(tot) + lax.dot_general(kend, vnb, (((0,), (0,)), ((), ())),
                                                           preferred_element_type=F32)
            outs.append(o_h)
        o_ref[0, pl.ds(r0, L), :] = jnp.concatenate(outs, axis=-1)
        return carry

    lax.fori_loop(0, nchunk, chunk_step, 0)

    if final:
        o = o_ref[0] + ob_ref[0]
        ms = _group_sum(o * o, ones_bd) * (1.0 / hd)
        gate = gate_ref[0]
        o_ref[0] = o * lax.rsqrt(ms + EPS) * ng_ref[...] * (gate * jax.nn.sigmoid(gate))

    @pl.when(i == nb - 1)
    def _():
        sfin_ref[0] = s_ref[...]


def _gdn_pass(qkv, ab, cw, dtb, a, s0, *, rev, final_inputs=None):
    b, t, width = qkv.shape
    heads = s0.shape[1]
    dim = heads * GDN_HEAD_DIM
    TB = min(t, 256)
    nb = t // TB
    hb = TB // HALO
    final = final_inputs is not None

    def bidx(i):
        return nb - 1 - i if rev else i

    in_specs = [
        pl.BlockSpec((1, TB, width), lambda bi, i: (bi, bidx(i), 0)),
        pl.BlockSpec((1, HALO, width), lambda bi, i: (bi, jnp.maximum(bidx(i) * hb - 1, 0), 0)),
        pl.BlockSpec((1, HALO, width), lambda bi, i: (bi, jnp.minimum((bidx(i) + 1) * hb, t // HALO - 1), 0)),
        pl.BlockSpec((1, TB, LANE), lambda bi, i: (bi, bidx(i), 0)),
        pl.BlockSpec((8, width), lambda bi, i: (0, 0)),
        pl.BlockSpec((1, LANE), lambda bi, i: (0, 0)),
        pl.BlockSpec((1, LANE), lambda bi, i: (0, 0)),
        pl.BlockSpec((1, heads, GDN_HEAD_DIM, GDN_HEAD_DIM), lambda bi, i: (bi, 0, 0, 0)),
    ]
    args = [qkv, qkv, qkv, ab, cw, dtb, a, s0]
    if final:
        ob, gate, ng = final_inputs
        in_specs += [
            pl.BlockSpec((1, TB, dim), lambda bi, i: (bi, bidx(i), 0)),
            pl.BlockSpec((1, TB, dim), lambda bi, i: (bi, bidx(i), 0)),
            pl.BlockSpec((1, dim), lambda bi, i: (0, 0)),
        ]
        args += [ob, gate, ng]
    return pl.pallas_call(
        functools.partial(_gdn_kernel, block=TB, rev=rev, final=final, heads=heads),
        grid=(b, nb),
        in_specs=in_specs,
        out_specs=[
            pl.BlockSpec((1, TB, dim), lambda bi, i: (bi, bidx(i), 0)),
            pl.BlockSpec((1, heads, GDN_HEAD_DIM, GDN_HEAD_DIM), lambda bi, i: (bi, 0, 0, 0)),
        ],
        out_shape=[
            jax.ShapeDtypeStruct((b, t, dim), F32),
            jax.ShapeDtypeStruct((b, heads, GDN_HEAD_DIM, GDN_HEAD_DIM), F32),
        ],
        scratch_shapes=[
            pltpu.VMEM((heads, GDN_HEAD_DIM, GDN_HEAD_DIM), F32),
            pltpu.VMEM((TB + 2 * HALO, width), F32),
            pltpu.VMEM((TB, dim), F32),
            pltpu.VMEM((TB, dim), F32),
            pltpu.VMEM((TB, dim), F32),
            pltpu.VMEM((TB, LANE), F32),
            pltpu.VMEM((TB, LANE), F32),
        ],
        compiler_params=_params(),
        name="gdn_bwd" if rev else "gdn_fwd",
    )(*args)


def _gdn_stream(qkv, gate, ab, conv_w, a_log, dt_bias, norm_g, s0_f, s0_b):
    heads = s0_f.shape[1]
    cw = jnp.pad(conv_w, ((0, 8 - CONV_W), (0, 0)))
    dtb = jnp.pad(dt_bias.reshape(1, -1), ((0, 0), (0, LANE - 2 * heads)))
    a = jnp.pad(-jnp.exp(a_log).reshape(1, -1), ((0, 0), (0, LANE - 2 * heads)))
    ng = jnp.tile(norm_g, heads)[None]
    ob, s_b = _gdn_pass(qkv, ab, cw, dtb, a, s0_b, rev=True)
    o, s_f = _gdn_pass(qkv, ab, cw, dtb, a, s0_f, rev=False, final_inputs=(ob, gate, ng))
    return o, s_f, s_b


def _rms_norm(x, g):
    return x * lax.rsqrt(jnp.mean(x * x, axis=-1, keepdims=True) + EPS) * g


def _to_raster(u):
    b, t, c = u.shape
    rows = t // GRID_W
    return u.reshape(b, GRID_W, rows, c).transpose(0, 2, 1, 3).reshape(b, t, c)


def _pool_branch(u, pool_w, pool_scale):
    b, t, pool_dim = u.shape
    pg = pool_dim // len(POOL_WINDOWS)
    csum = jnp.concatenate([jnp.zeros((b, 1, pool_dim), F32), jnp.cumsum(u, axis=1)], axis=1)
    tt = jnp.arange(t)
    groups = []
    for gi, win in enumerate(POOL_WINDOWS):
        lo = jnp.clip(tt - win // 2, 0, t)
        hi = jnp.clip(tt + win // 2, 0, t)
        sl = slice(gi * pg, (gi + 1) * pg)
        cg = csum[:, :, sl]
        mean = (cg[:, hi] - cg[:, lo]) / (hi - lo).astype(F32)[None, :, None]
        groups.append(mean - u[:, :, sl])
    dd = jnp.stack(groups, axis=2)
    return jnp.einsum('btgc,gcd->btgd', dd, pool_w).reshape(b, t, pool_dim) * pool_scale


def _expert_choice_ffn(h, router_w, wg, wu, wd):
    b, t, d = h.shape
    cap = EC_CAPACITY * t // N_EXPERTS
    aff = jax.nn.softmax(jnp.einsum('btd,de->bte', h, router_w), axis=-1)
    gate, idx = lax.top_k(jnp.swapaxes(aff, 1, 2), cap)
    xe = jax.vmap(lambda hb, ib: hb[ib])(h, idx)
    ye = _expert_ffn(xe.reshape(b * N_EXPERTS, cap, d), gate.reshape(b * N_EXPERTS, cap, 1), wg, wu, wd)
    ye = ye.reshape(b, N_EXPERTS, cap, d)
    return jax.vmap(lambda ib, yb: jnp.zeros((t, d), yb.dtype).at[ib.reshape(-1)].add(yb.reshape(-1, d)))(idx, ye)


def kernel(x, c, ctx, c_ctx, norm1_g, norm2_g, ada_w, ada_b, w_in, w_out, pool_w, pool_scale, ssd_conv_w, ssd_conv_b, ssd_a_log, ssd_dt_bias, ssd_d, ssd_norm_g, gdn_conv_w, gdn_a_log, gdn_dt_bias, gdn_norm_g, router_w, exp_w_gate, exp_w_up, exp_w_down, final_norm_g):
    depth, d, _ = w_in.shape
    b, t, _ = x.shape
    pool_dim = pool_scale.shape[-1]
    ssd_dim = ssd_norm_g.shape[-1]
    ssd_heads = ssd_dim // SSD_HEAD_DIM
    ssd_bc = SSD_GROUPS * SSD_STATE
    gdn_dim = gdn_conv_w.shape[-1] // 3
    gdn_heads = gdn_dim // GDN_HEAD_DIM
    splits = (pool_dim, ssd_dim, ssd_dim + 2 * ssd_bc, 2 * ssd_heads, 3 * gdn_dim, gdn_dim, 2 * gdn_heads,
              2 * gdn_heads)
    cut = [0] + np.cumsum(splits).tolist()
    r_widths = (ssd_dim, ssd_dim + 2 * ssd_bc, pool_dim, LANE)
    g_widths = (3 * gdn_dim, gdn_dim, LANE)

    sc = jax.nn.silu(c)
    scc = jax.nn.silu(c_ctx)[None]
    for l in range(depth):
        last = l == depth - 1
        wl = w_in[l]
        seg = [wl[:, cut[i]:cut[i + 1]] for i in range(8)]
        w_r = jnp.concatenate(
            [seg[1], seg[2], seg[0], jnp.pad(seg[3], ((0, 0), (0, LANE - 2 * ssd_heads)))], axis=1).astype(BF16)
        w_g = jnp.concatenate(
            [seg[4], seg[5], jnp.pad(jnp.concatenate([seg[6], seg[7]], axis=1), ((0, 0), (0, LANE - 4 * gdn_heads)))],
            axis=1).astype(BF16)
        w_o = w_out[l].astype(BF16)
        wg, wu, wd = exp_w_gate[l].astype(BF16), exp_w_up[l].astype(BF16), exp_w_down[l].astype(BF16)
        m_lat = jnp.split(sc @ ada_w[l] + ada_b[l], 6, axis=-1)
        m_ctx = [jnp.broadcast_to(m, (b, d)) for m in jnp.split(scc @ ada_w[l] + ada_b[l], 6, axis=-1)]
        g1 = norm1_g[l][None]

        def project(xx, mm, column_major):
            pr = _inproj(xx, mm[1][:, None], mm[0][:, None], g1, w_r, r_widths, column_major=False)
            pg = _inproj(xx, mm[1][:, None], mm[0][:, None], g1, w_g, g_widths, column_major=column_major)
            return pr, pg

        ssd_p = (ssd_conv_w[l], ssd_conv_b[l], ssd_a_log[l], ssd_dt_bias[l], ssd_d[l], ssd_norm_g[l])
        gdn_p = (gdn_conv_w[l], gdn_a_log[l], gdn_dt_bias[l], gdn_norm_g[l])
        zs = jnp.zeros((b, ssd_heads, SSD_STATE, SSD_HEAD_DIM), F32)
        zg = jnp.zeros((b, gdn_heads, GDN_HEAD_DIM, GDN_HEAD_DIM), F32)

        (c_z, c_xbc, c_pool, c_dt), (c_qkv, c_gate, c_ab) = project(ctx, m_ctx, False)
        (l_z, l_xbc, l_pool, l_dt), (l_qkv, l_gate, l_ab) = project(x, m_lat, True)

        s_ctx, ssd_sf, ssd_sb = _ssd_stream(c_z, c_xbc, c_dt, *ssd_p, zs, zs)
        g_ctx, gdn_sf, gdn_sb = _gdn_stream(c_qkv, c_gate, c_ab, *gdn_p, zg, zg)
        s_lat, _, _ = _ssd_stream(l_z, l_xbc, l_dt, *ssd_p, ssd_sf, ssd_sb)
        g_lat, _, _ = _gdn_stream(l_qkv, l_gate, l_ab, *gdn_p, gdn_sf, gdn_sb)
        g_lat = _to_raster(g_lat)
        a_lat = jnp.concatenate([_pool_branch(l_pool, pool_w[l], pool_scale[l]), s_lat, g_lat], axis=-1)
        x = _outproj(a_lat, x, m_lat[2][:, None], w_o)
        h2 = _rms_norm(x, norm2_g[l]) * (1 + m_lat[4][:, None]) + m_lat[3][:, None]
        x = x + m_lat[5][:, None] * _expert_choice_ffn(h2, router_w[l], wg, wu, wd)
        if not last:
            a_ctx = jnp.concatenate([_pool_branch(c_pool, pool_w[l], pool_scale[l]), s_ctx, g_ctx], axis=-1)
            ctx = _outproj(a_ctx, ctx, m_ctx[2][:, None], w_o)
            h2c = _rms_norm(ctx, norm2_g[l]) * (1 + m_ctx[4][:, None]) + m_ctx[3][:, None]
            ctx = ctx + m_ctx[5][:, None] * _expert_choice_ffn(h2c, router_w[l], wg, wu, wd)
    return _rms_norm(x, final_norm_g)
```

```python
import functools

import numpy as np
import jax
import jax.numpy as jnp
from jax import lax
from jax.experimental import pallas as pl
from jax.experimental.pallas import tpu as pltpu

F32 = jnp.float32
BF16 = jnp.bfloat16

GRID_W = 64
CHUNK = 64
CONV_W = 5
POOL_WINDOWS = (2, 4, 8, 16)
SSD_HEAD_DIM = 64
SSD_GROUPS = 2
SSD_STATE = 128
GDN_HEAD_DIM = 64
N_EXPERTS = 16
EC_CAPACITY = 2
EPS = 1e-6
LANE = 128
HALO = 8
NEG = -1e30
VMEM_LIMIT = 48 * 1024 * 1024


def _round_up(n, m):
    return (n + m - 1) // m * m


def _params():
    return pltpu.CompilerParams(dimension_semantics=("arbitrary", "arbitrary"), vmem_limit_bytes=VMEM_LIMIT)


def _inproj_kernel(x_ref, sc_ref, sh_ref, g_ref, w_ref, *o_refs):
    x = x_ref[0]
    ms = jnp.mean(x * x, axis=-1, keepdims=True)
    h = x * lax.rsqrt(ms + EPS) * g_ref[...]
    h = h * (1.0 + sc_ref[0]) + sh_ref[0]
    y = jnp.dot(h.astype(BF16), w_ref[...], preferred_element_type=F32)
    off = 0
    for o_ref in o_refs:
        n = o_ref.shape[-1]
        o_ref[0] = y[:, off:off + n]
        off += n


def _inproj(x, scale, shift, g, w, widths, *, column_major):
    b, t, d = x.shape
    if column_major:
        rows = t // GRID_W
        xv = x.reshape(b, rows, GRID_W * d)
        grid = (b, GRID_W)
        x_spec = pl.BlockSpec((1, rows, d), lambda i, j: (i, 0, j))
        tm = rows
    else:
        tm = min(t, 512)
        xv = x
        grid = (b, t // tm)
        x_spec = pl.BlockSpec((1, tm, d), lambda i, j: (i, j, 0))
    return pl.pallas_call(
        _inproj_kernel,
        grid=grid,
        in_specs=[
            x_spec,
            pl.BlockSpec((1, 1, d), lambda i, j: (i, 0, 0)),
            pl.BlockSpec((1, 1, d), lambda i, j: (i, 0, 0)),
            pl.BlockSpec((1, d), lambda i, j: (0, 0)),
            pl.BlockSpec((d, w.shape[1]), lambda i, j: (0, 0)),
        ],
        out_specs=[pl.BlockSpec((1, tm, n), lambda i, j: (i, j, 0)) for n in widths],
        out_shape=[jax.ShapeDtypeStruct((b, t, n), F32) for n in widths],
        compiler_params=_params(),
        name="inproj",
    )(xv, scale, shift, g, w)


def _outproj_kernel(a_ref, x_ref, gate_ref, w_ref, o_ref):
    y = jnp.dot(a_ref[0].astype(BF16), w_ref[...], preferred_element_type=F32)
    o_ref[0] = x_ref[0] + gate_ref[0] * y


def _outproj(a, x, gate, w):
    b, t, d = x.shape
    k = a.shape[-1]
    tm = min(t, 512)
    return pl.pallas_call(
        _outproj_kernel,
        grid=(b, t // tm),
        in_specs=[
            pl.BlockSpec((1, tm, k), lambda i, j: (i, j, 0)),
            pl.BlockSpec((1, tm, d), lambda i, j: (i, j, 0)),
            pl.BlockSpec((1, 1, d), lambda i, j: (i, 0, 0)),
            pl.BlockSpec((k, d), lambda i, j: (0, 0)),
        ],
        out_specs=pl.BlockSpec((1, tm, d), lambda i, j: (i, j, 0)),
        out_shape=jax.ShapeDtypeStruct((b, t, d), F32),
        compiler_params=_params(),
        name="outproj",
    )(a, x, gate, w)


def _ffn_kernel(x_ref, gate_ref, wg_ref, wu_ref, wd_ref, o_ref):
    x = x_ref[0].astype(BF16)
    hg = jnp.dot(x, wg_ref[0], preferred_element_type=F32)
    hu = jnp.dot(x, wu_ref[0], preferred_element_type=F32)
    hid = (hg * jax.nn.sigmoid(hg)) * hu
    y = jnp.dot(hid.astype(BF16), wd_ref[0], preferred_element_type=F32)
    o_ref[0] = y * gate_ref[0]


def _expert_ffn(xe, gate, wg, wu, wd):
    be, c, d = xe.shape
    e, _, f = wg.shape
    tc = min(c, 512)
    return pl.pallas_call(
        _ffn_kernel,
        grid=(be, c // tc),
        in_specs=[
            pl.BlockSpec((1, tc, d), lambda i, j: (i, j, 0)),
            pl.BlockSpec((1, tc, 1), lambda i, j: (i, j, 0)),
            pl.BlockSpec((1, d, f), lambda i, j: (i % e, 0, 0)),
            pl.BlockSpec((1, d, f), lambda i, j: (i % e, 0, 0)),
            pl.BlockSpec((1, f, d), lambda i, j: (i % e, 0, 0)),
        ],
        out_specs=pl.BlockSpec((1, tc, d), lambda i, j: (i, j, 0)),
        out_shape=jax.ShapeDtypeStruct((be, c, d), F32),
        compiler_params=_params(),
        name="expert_ffn",
    )(xe, gate, wg, wu, wd)


def _split3(a):
    hi = a.astype(BF16)
    r1 = a - hi.astype(F32)
    mid = r1.astype(BF16)
    lo = (r1 - mid.astype(F32)).astype(BF16)
    return hi, mid, lo


def _cumsum_both(la, incl):
    parts = _split3(la)
    cs_col = sum(jnp.dot(incl, p, preferred_element_type=F32) for p in parts)
    cs_row = sum(lax.dot_general(p, incl, (((0,), (1,)), ((), ())), preferred_element_type=F32) for p in parts)
    return cs_col, cs_row


def _ssd_kernel(*refs, chunk, rev, final, heads):
    if final:
        (xbc_ref, prev_ref, next_ref, dt_ref, cw_ref, cb_ref, dtb_ref, a_ref, dsk_ref, s0_ref,
         yb_ref, z_ref, ng_ref, y_ref, sfin_ref, s_ref, ext_ref) = refs
    else:
        (xbc_ref, prev_ref, next_ref, dt_ref, cw_ref, cb_ref, dtb_ref, a_ref, dsk_ref, s0_ref,
         y_ref, sfin_ref, s_ref, ext_ref) = refs
    L = chunk
    i = pl.program_id(1)
    nc = pl.num_programs(1)
    j = nc - 1 - i if rev else i
    hd = SSD_HEAD_DIM
    ssd_dim = heads * hd
    rep = heads // SSD_GROUPS

    @pl.when(i == 0)
    def _():
        s_ref[...] = s0_ref[0]

    ext_ref[0:HALO, :] = jnp.where(j > 0, prev_ref[0], 0.0)
    ext_ref[HALO:HALO + L, :] = xbc_ref[0]
    ext_ref[HALO + L:, :] = jnp.where(j < nc - 1, next_ref[0], 0.0)
    base = HALO - CONV_W // 2
    acc = cb_ref[...] + cw_ref[0:1, :] * ext_ref[base:base + L, :]
    for k in range(1, CONV_W):
        acc = acc + cw_ref[k:k + 1, :] * ext_ref[base + k:base + k + L, :]
    act = acc * jax.nn.sigmoid(acc)
    xs = act[:, :ssd_dim]
    bmat = act[:, ssd_dim:ssd_dim + SSD_GROUPS * SSD_STATE]
    cmat = act[:, ssd_dim + SSD_GROUPS * SSD_STATE:]

    dtv = jax.nn.softplus(dt_ref[0] + dtb_ref[...])
    la = dtv * a_ref[...]
    row = lax.broadcasted_iota(jnp.int32, (L, L), 0)
    col = lax.broadcasted_iota(jnp.int32, (L, L), 1)
    mask = (row <= col) if rev else (row >= col)
    cs_col, cs_row = _cumsum_both(la, mask.astype(BF16))
    last = 0 if rev else L - 1

    gmats = []
    for g in range(SSD_GROUPS):
        cg = cmat[:, g * SSD_STATE:(g + 1) * SSD_STATE].astype(BF16)
        bg = bmat[:, g * SSD_STATE:(g + 1) * SSD_STATE].astype(BF16)
        gmats.append(lax.dot_general(cg, bg, (((1,), (1,)), ((), ())), preferred_element_type=F32))

    ys = []
    for h in range(heads):
        g = h // rep
        ci = (heads if rev else 0) + h
        csc = cs_col[:, ci:ci + 1]
        csr = cs_row[ci:ci + 1, :]
        tot = csc[last:last + 1, :]
        dec = jnp.exp(jnp.where(mask, csc - csr, NEG))
        xs_h = xs[:, h * hd:(h + 1) * hd]
        xdt = (xs_h * dtv[:, ci:ci + 1]).astype(BF16)
        b_g = bmat[:, g * SSD_STATE:(g + 1) * SSD_STATE]
        c_g = cmat[:, g * SSD_STATE:(g + 1) * SSD_STATE]
        s_h = s_ref[h]
        y_h = jnp.dot((gmats[g] * dec).astype(BF16), xdt, preferred_element_type=F32)
        y_h = y_h + jnp.dot((c_g * jnp.exp(csc)).astype(BF16), s_h.astype(BF16), preferred_element_type=F32)
        local = lax.dot_general((b_g * jnp.exp(tot - csc)).astype(BF16), xdt, (((0,), (0,)), ((), ())),
                                preferred_element_type=F32)
        s_ref[h] = s_h * jnp.exp(tot) + local
        if final:
            y_h = y_h + dsk_ref[:, h * hd:(h + 1) * hd] * xs_h
        ys.append(y_h)
    y = jnp.concatenate(ys, axis=-1)
    if final:
        y = y + yb_ref[0]
        z = z_ref[0]
        y = y * (z * jax.nn.sigmoid(z))
        y = y * lax.rsqrt(jnp.mean(y * y, axis=-1, keepdims=True) + EPS) * ng_ref[...]
    y_ref[0] = y

    @pl.when(i == nc - 1)
    def _():
        sfin_ref[0] = s_ref[...]


def _ssd_pass(xbc, dt, cw, cb, dtb, a, dsk, s0, *, rev, final_inputs=None):
    b, t, width = xbc.shape
    heads = s0.shape[1]
    ssd_dim = heads * SSD_HEAD_DIM
    L = min(t, 256)
    nc = t // L
    hb = L // HALO
    final = final_inputs is not None

    def cidx(i):
        return nc - 1 - i if rev else i

    in_specs = [
        pl.BlockSpec((1, L, width), lambda bi, i: (bi, cidx(i), 0)),
        pl.BlockSpec((1, HALO, width), lambda bi, i: (bi, jnp.maximum(cidx(i) * hb - 1, 0), 0)),
        pl.BlockSpec((1, HALO, width), lambda bi, i: (bi, jnp.minimum((cidx(i) + 1) * hb, t // HALO - 1), 0)),
        pl.BlockSpec((1, L, LANE), lambda bi, i: (bi, cidx(i), 0)),
        pl.BlockSpec((8, width), lambda bi, i: (0, 0)),
        pl.BlockSpec((1, width), lambda bi, i: (0, 0)),
        pl.BlockSpec((1, LANE), lambda bi, i: (0, 0)),
        pl.BlockSpec((1, LANE), lambda bi, i: (0, 0)),
        pl.BlockSpec((1, ssd_dim), lambda bi, i: (0, 0)),
        pl.BlockSpec((1, heads, SSD_STATE, SSD_HEAD_DIM), lambda bi, i: (bi, 0, 0, 0)),
    ]
    args = [xbc, xbc, xbc, dt, cw, cb, dtb, a, dsk, s0]
    if final:
        yb, z, ng = final_inputs
        in_specs += [
            pl.BlockSpec((1, L, ssd_dim), lambda bi, i: (bi, cidx(i), 0)),
            pl.BlockSpec((1, L, ssd_dim), lambda bi, i: (bi, cidx(i), 0)),
            pl.BlockSpec((1, ssd_dim), lambda bi, i: (0, 0)),
        ]
        args += [yb, z, ng]
    return pl.pallas_call(
        functools.partial(_ssd_kernel, chunk=L, rev=rev, final=final, heads=heads),
        grid=(b, nc),
        in_specs=in_specs,
        out_specs=[
            pl.BlockSpec((1, L, ssd_dim), lambda bi, i: (bi, cidx(i), 0)),
            pl.BlockSpec((1, heads, SSD_STATE, SSD_HEAD_DIM), lambda bi, i: (bi, 0, 0, 0)),
        ],
        out_shape=[
            jax.ShapeDtypeStruct((b, t, ssd_dim), F32),
            jax.ShapeDtypeStruct((b, heads, SSD_STATE, SSD_HEAD_DIM), F32),
        ],
        scratch_shapes=[
            pltpu.VMEM((heads, SSD_STATE, SSD_HEAD_DIM), F32),
            pltpu.VMEM((L + 2 * HALO, width), F32),
        ],
        compiler_params=_params(),
        name="ssd_bwd" if rev else "ssd_fwd",
    )(*args)


def _ssd_stream(z, xbc, dt, conv_w, conv_b, a_log, dt_bias, d_skip, norm_g, s0_f, s0_b):
    heads = s0_f.shape[1]
    cw = jnp.pad(conv_w, ((0, 8 - CONV_W), (0, 0)))
    cb = conv_b[None]
    dtb = jnp.pad(dt_bias.reshape(1, -1), ((0, 0), (0, LANE - 2 * heads)))
    a = jnp.pad(-jnp.exp(a_log).reshape(1, -1), ((0, 0), (0, LANE - 2 * heads)))
    dsk = jnp.repeat(d_skip, SSD_HEAD_DIM)[None]
    yb, s_b = _ssd_pass(xbc, dt, cw, cb, dtb, a, dsk, s0_b, rev=True)
    y, s_f = _ssd_pass(xbc, dt, cw, cb, dtb, a, dsk, s0_f, rev=False, final_inputs=(yb, z, norm_g[None]))
    return y, s_f, s_b


def _group_sum(a, ones_bd):
    hi = a.astype(BF16)
    lo = (a - hi.astype(F32)).astype(BF16)
    return (jnp.dot(hi, ones_bd, preferred_element_type=F32) + jnp.dot(lo, ones_bd, preferred_element_type=F32))


def _mm_bf16(a, b):
    return jnp.dot(a.astype(BF16), b.astype(BF16), preferred_element_type=F32)


def _unit_tri_inverse_many(xms, rev):
    n = xms[0].shape[0]
    row = lax.broadcasted_iota(jnp.int32, (n, n), 0)
    col = lax.broadcasted_iota(jnp.int32, (n, n), 1)
    eye = (row == col).astype(F32)
    ds = None
    m, sh = 1, 0
    while m < n:
        same = (row >> (sh + 1)) == (col >> (sh + 1))
        rbit = (row >> sh) & 1
        cbit = (col >> sh) & 1
        sel = same & ((rbit == 0) & (cbit == 1) if rev else (rbit == 1) & (cbit == 0))
        cs = [jnp.where(sel, x, 0.0) for x in xms]
        if ds is None:
            ds = [eye - c for c in cs]
        else:
            es = [_mm_bf16(c, d) for c, d in zip(cs, ds)]
            ds = [d - _mm_bf16(d, e) for d, e in zip(ds, es)]
        m, sh = 2 * m, sh + 1
    return ds


def _gdn_kernel(*refs, block, rev, final, heads):
    if final:
        (qkv_ref, prev_ref, next_ref, ab_ref, cw_ref, dtb_ref, a_ref, s0_ref, ob_ref, gate_ref, ng_ref,
         o_ref, sfin_ref, s_ref, ext_ref) = refs
    else:
        (qkv_ref, prev_ref, next_ref, ab_ref, cw_ref, dtb_ref, a_ref, s0_ref,
         o_ref, sfin_ref, s_ref, ext_ref) = refs
    TB = block
    L = block
    hd = GDN_HEAD_DIM
    dim = heads * hd
    i = pl.program_id(1)
    nb = pl.num_programs(1)
    j = nb - 1 - i if rev else i

    @pl.when(i == 0)
    def _():
        s_ref[...] = s0_ref[0]

    ext_ref[0:HALO, :] = jnp.where(j > 0, prev_ref[0], 0.0)
    ext_ref[HALO:HALO + TB, :] = qkv_ref[0]
    ext_ref[HALO + TB:, :] = jnp.where(j < nb - 1, next_ref[0], 0.0)
    base = HALO - CONV_W // 2
    acc = cw_ref[0:1, :] * ext_ref[base:base + TB, :]
    for kk in range(1, CONV_W):
        acc = acc + cw_ref[kk:kk + 1, :] * ext_ref[base + kk:base + kk + TB, :]
    act = acc * jax.nn.sigmoid(acc)
    q = act[:, :dim]
    k = act[:, dim:2 * dim]
    v = act[:, 2 * dim:]
    ri = lax.broadcasted_iota(jnp.int32, (dim, dim), 0) // hd
    ci_ = lax.broadcasted_iota(jnp.int32, (dim, dim), 1) // hd
    ones_bd = (ri == ci_).astype(BF16)
    q = q * lax.rsqrt(_group_sum(q * q, ones_bd) + EPS) * (hd ** -0.5)
    k = k * lax.rsqrt(_group_sum(k * k, ones_bd) + EPS)
    ab = ab_ref[0]
    gl = a_ref[...] * jax.nn.softplus(ab + dtb_ref[...])
    beta = jax.nn.sigmoid(ab)

    row = lax.broadcasted_iota(jnp.int32, (L, L), 0)
    col = lax.broadcasted_iota(jnp.int32, (L, L), 1)
    incl = (row <= col) if rev else (row >= col)
    strict = (row < col) if rev else (row > col)
    last = 0 if rev else L - 1
    cs_col, cs_row = _cumsum_both(gl, incl.astype(BF16))

    hs = range(heads)
    cis = [(heads if rev else 0) + h for h in hs]
    qh = [q[:, h * hd:(h + 1) * hd] for h in hs]
    kh = [k[:, h * hd:(h + 1) * hd] for h in hs]
    vh = [v[:, h * hd:(h + 1) * hd] for h in hs]
    csc = [cs_col[:, ci:ci + 1] for ci in cis]
    bcol = [beta[:, 2 * heads + ci:2 * heads + ci + 1] for ci in cis]
    qk_kk = [lax.dot_general(jnp.concatenate([qh[h], kh[h]], axis=0).astype(BF16), kh[h].astype(BF16),
                             (((1,), (1,)), ((), ())), preferred_element_type=F32) for h in hs]
    dec = [jnp.exp(jnp.where(incl, csc[h] - cs_row[cis[h]:cis[h] + 1, :], NEG)) for h in hs]
    attn = [(qk_kk[h][:L] * dec[h]).astype(BF16) for h in hs]
    xm = [jnp.where(strict, bcol[h] * qk_kk[h][L:] * dec[h], 0.0) for h in hs]
    tm = _unit_tri_inverse_many(xm, rev)
    egc = [jnp.exp(cc) for cc in csc]
    uw = [jnp.dot(tm[h].astype(BF16),
                  jnp.concatenate([vh[h] * bcol[h], kh[h] * (bcol[h] * egc[h])], axis=1).astype(BF16),
                  preferred_element_type=F32) for h in hs]
    s_old = [s_ref[h] for h in hs]
    rs = [jnp.dot(jnp.concatenate([uw[h][:, hd:], qh[h] * egc[h]], axis=0).astype(BF16), s_old[h].astype(BF16),
                  preferred_element_type=F32) for h in hs]
    vnb = [(uw[h][:, :hd] - rs[h][:L]).astype(BF16) for h in hs]
    outs = [rs[h][L:] + jnp.dot(attn[h], vnb[h], preferred_element_type=F32) for h in hs]
    for h in hs:
        tot = csc[h][last:last + 1, :]
        kend = (kh[h] * jnp.exp(tot - csc[h])).astype(BF16)
        s_ref[h] = s_old[h] * jnp.exp(tot) + lax.dot_general(kend, vnb[h], (((0,), (0,)), ((), ())),
                                                            preferred_element_type=F32)
    o = jnp.concatenate(outs, axis=-1)
    if final:
        o = o + ob_ref[0]
        ms = _group_sum(o * o, ones_bd) * (1.0 / hd)
        gate = gate_ref[0]
        o = o * lax.rsqrt(ms + EPS) * ng_ref[...] * (gate * jax.nn.sigmoid(gate))
    o_ref[0] = o

    @pl.when(i == nb - 1)
    def _():
        sfin_ref[0] = s_ref[...]


def _gdn_pass(qkv, ab, cw, dtb, a, s0, *, rev, final_inputs=None):
    b, t, width = qkv.shape
    heads = s0.shape[1]
    dim = heads * GDN_HEAD_DIM
    TB = min(t, 256)
    nb = t // TB
    hb = TB // HALO
    final = final_inputs is not None

    def bidx(i):
        return nb - 1 - i if rev else i

    in_specs = [
        pl.BlockSpec((1, TB, width), lambda bi, i: (bi, bidx(i), 0)),
        pl.BlockSpec((1, HALO, width), lambda bi, i: (bi, jnp.maximum(bidx(i) * hb - 1, 0), 0)),
        pl.BlockSpec((1, HALO, width), lambda bi, i: (bi, jnp.minimum((bidx(i) + 1) * hb, t // HALO - 1), 0)),
        pl.BlockSpec((1, TB, LANE), lambda bi, i: (bi, bidx(i), 0)),
        pl.BlockSpec((8, width), lambda bi, i: (0, 0)),
        pl.BlockSpec((1, LANE), lambda bi, i: (0, 0)),
        pl.BlockSpec((1, LANE), lambda bi, i: (0, 0)),
        pl.BlockSpec((1, heads, GDN_HEAD_DIM, GDN_HEAD_DIM), lambda bi, i: (bi, 0, 0, 0)),
    ]
    args = [qkv, qkv, qkv, ab, cw, dtb, a, s0]
    if final:
        ob, gate, ng = final_inputs
        in_specs += [
            pl.BlockSpec((1, TB, dim), lambda bi, i: (bi, bidx(i), 0)),
            pl.BlockSpec((1, TB, dim), lambda bi, i: (bi, bidx(i), 0)),
            pl.BlockSpec((1, dim), lambda bi, i: (0, 0)),
        ]
        args += [ob, gate, ng]
    return pl.pallas_call(
        functools.partial(_gdn_kernel, block=TB, rev=rev, final=final, heads=heads),
        grid=(b, nb),
        in_specs=in_specs,
        out_specs=[
            pl.BlockSpec((1, TB, dim), lambda bi, i: (bi, bidx(i), 0)),
            pl.BlockSpec((1, heads, GDN_HEAD_DIM, GDN_HEAD_DIM), lambda bi, i: (bi, 0, 0, 0)),
        ],
        out_shape=[
            jax.ShapeDtypeStruct((b, t, dim), F32),
            jax.ShapeDtypeStruct((b, heads, GDN_HEAD_DIM, GDN_HEAD_DIM), F32),
        ],
        scratch_shapes=[
            pltpu.VMEM((heads, GDN_HEAD_DIM, GDN_HEAD_DIM), F32),
            pltpu.VMEM((TB + 2 * HALO, width), F32),
        ],
        compiler_params=_params(),
        name="gdn_bwd" if rev else "gdn_fwd",
    )(*args)


def _gdn_stream(qkv, gate, ab, conv_w, a_log, dt_bias, norm_g, s0_f, s0_b):
    heads = s0_f.shape[1]
    cw = jnp.pad(conv_w, ((0, 8 - CONV_W), (0, 0)))
    dtb = jnp.pad(dt_bias.reshape(1, -1), ((0, 0), (0, LANE - 2 * heads)))
    a = jnp.pad(-jnp.exp(a_log).reshape(1, -1), ((0, 0), (0, LANE - 2 * heads)))
    ng = jnp.tile(norm_g, heads)[None]
    ob, s_b = _gdn_pass(qkv, ab, cw, dtb, a, s0_b, rev=True)
    o, s_f = _gdn_pass(qkv, ab, cw, dtb, a, s0_f, rev=False, final_inputs=(ob, gate, ng))
    return o, s_f, s_b


def _rms_norm(x, g):
    return x * lax.rsqrt(jnp.mean(x * x, axis=-1, keepdims=True) + EPS) * g


def _to_raster(u):
    b, t, c = u.shape
    rows = t // GRID_W
    return u.reshape(b, GRID_W, rows, c).transpose(0, 2, 1, 3).reshape(b, t, c)


def _pool_branch(u, pool_w, pool_scale):
    b, t, pool_dim = u.shape
    pg = pool_dim // len(POOL_WINDOWS)
    csum = jnp.concatenate([jnp.zeros((b, 1, pool_dim), F32), jnp.cumsum(u, axis=1)], axis=1)
    tt = jnp.arange(t)
    groups = []
    for gi, win in enumerate(POOL_WINDOWS):
        lo = jnp.clip(tt - win // 2, 0, t)
        hi = jnp.clip(tt + win // 2, 0, t)
        sl = slice(gi * pg, (gi + 1) * pg)
        cg = csum[:, :, sl]
        mean = (cg[:, hi] - cg[:, lo]) / (hi - lo).astype(F32)[None, :, None]
        groups.append(mean - u[:, :, sl])
    dd = jnp.stack(groups, axis=2)
    return jnp.einsum('btgc,gcd->btgd', dd, pool_w).reshape(b, t, pool_dim) * pool_scale


def _expert_choice_ffn(h, router_w, wg, wu, wd):
    b, t, d = h.shape
    cap = EC_CAPACITY * t // N_EXPERTS
    aff = jax.nn.softmax(jnp.einsum('btd,de->bte', h, router_w, precision=lax.Precision.HIGHEST), axis=-1)
    gate, idx = lax.top_k(jnp.swapaxes(aff, 1, 2), cap)
    xe = jax.vmap(lambda hb, ib: hb[ib])(h, idx)
    ye = _expert_ffn(xe.reshape(b * N_EXPERTS, cap, d), gate.reshape(b * N_EXPERTS, cap, 1), wg, wu, wd)
    ye = ye.reshape(b, N_EXPERTS, cap, d)
    return jax.vmap(lambda ib, yb: jnp.zeros((t, d), yb.dtype).at[ib.reshape(-1)].add(yb.reshape(-1, d)))(idx, ye)


def kernel(x, c, ctx, c_ctx, norm1_g, norm2_g, ada_w, ada_b, w_in, w_out, pool_w, pool_scale, ssd_conv_w, ssd_conv_b, ssd_a_log, ssd_dt_bias, ssd_d, ssd_norm_g, gdn_conv_w, gdn_a_log, gdn_dt_bias, gdn_norm_g, router_w, exp_w_gate, exp_w_up, exp_w_down, final_norm_g):
    depth, d, _ = w_in.shape
    b, t, _ = x.shape
    pool_dim = pool_scale.shape[-1]
    ssd_dim = ssd_norm_g.shape[-1]
    ssd_heads = ssd_dim // SSD_HEAD_DIM
    ssd_bc = SSD_GROUPS * SSD_STATE
    gdn_dim = gdn_conv_w.shape[-1] // 3
    gdn_heads = gdn_dim // GDN_HEAD_DIM
    splits = (pool_dim, ssd_dim, ssd_dim + 2 * ssd_bc, 2 * ssd_heads, 3 * gdn_dim, gdn_dim, 2 * gdn_heads,
              2 * gdn_heads)
    cut = [0] + np.cumsum(splits).tolist()
    r_widths = (ssd_dim, ssd_dim + 2 * ssd_bc, pool_dim, LANE)
    g_widths = (3 * gdn_dim, gdn_dim, LANE)

    sc = jax.nn.silu(c)
    scc = jax.nn.silu(c_ctx)[None]
    for l in range(depth):
        last = l == depth - 1
        wl = w_in[l]
        seg = [wl[:, cut[i]:cut[i + 1]] for i in range(8)]
        w_r = jnp.concatenate(
            [seg[1], seg[2], seg[0], jnp.pad(seg[3], ((0, 0), (0, LANE - 2 * ssd_heads)))], axis=1).astype(BF16)
        w_g = jnp.concatenate(
            [seg[4], seg[5], jnp.pad(jnp.concatenate([seg[6], seg[7]], axis=1), ((0, 0), (0, LANE - 4 * gdn_heads)))],
            axis=1).astype(BF16)
        w_o = w_out[l].astype(BF16)
        wg, wu, wd = exp_w_gate[l].astype(BF16), exp_w_up[l].astype(BF16), exp_w_down[l].astype(BF16)
        m_lat = jnp.split(sc @ ada_w[l] + ada_b[l], 6, axis=-1)
        m_ctx = [jnp.broadcast_to(m, (b, d)) for m in jnp.split(scc @ ada_w[l] + ada_b[l], 6, axis=-1)]
        g1 = norm1_g[l][None]

        def project(xx, mm, column_major):
            pr = _inproj(xx, mm[1][:, None], mm[0][:, None], g1, w_r, r_widths, column_major=False)
            pg = _inproj(xx, mm[1][:, None], mm[0][:, None], g1, w_g, g_widths, column_major=column_major)
            return pr, pg

        ssd_p = (ssd_conv_w[l], ssd_conv_b[l], ssd_a_log[l], ssd_dt_bias[l], ssd_d[l], ssd_norm_g[l])
        gdn_p = (gdn_conv_w[l], gdn_a_log[l], gdn_dt_bias[l], gdn_norm_g[l])
        zs = jnp.zeros((b, ssd_heads, SSD_STATE, SSD_HEAD_DIM), F32)
        zg = jnp.zeros((b, gdn_heads, GDN_HEAD_DIM, GDN_HEAD_DIM), F32)

        (c_z, c_xbc, c_pool, c_dt), (c_qkv, c_gate, c_ab) = project(ctx, m_ctx, False)
        (l_z, l_xbc, l_pool, l_dt), (l_qkv, l_gate, l_ab) = project(x, m_lat, True)

        s_ctx, ssd_sf, ssd_sb = _ssd_stream(c_z, c_xbc, c_dt, *ssd_p, zs, zs)
        g_ctx, gdn_sf, gdn_sb = _gdn_stream(c_qkv, c_gate, c_ab, *gdn_p, zg, zg)
        s_lat, _, _ = _ssd_stream(l_z, l_xbc, l_dt, *ssd_p, ssd_sf, ssd_sb)
        g_lat, _, _ = _gdn_stream(l_qkv, l_gate, l_ab, *gdn_p, gdn_sf, gdn_sb)
        g_lat = _to_raster(g_lat)
        a_lat = jnp.concatenate([_pool_branch(l_pool, pool_w[l], pool_scale[l]), s_lat, g_lat], axis=-1)
        x = _outproj(a_lat, x, m_lat[2][:, None], w_o)
        h2 = _rms_norm(x, norm2_g[l]) * (1 + m_lat[4][:, None]) + m_lat[3][:, None]
        x = x + m_lat[5][:, None] * _expert_choice_ffn(h2, router_w[l], wg, wu, wd)
        if not last:
            a_ctx = jnp.concatenate([_pool_branch(c_pool, pool_w[l], pool_scale[l]), s_ctx, g_ctx], axis=-1)
            ctx = _outproj(a_ctx, ctx, m_ctx[2][:, None], w_o)
            h2c = _rms_norm(ctx, norm2_g[l]) * (1 + m_ctx[4][:, None]) + m_ctx[3][:, None]
            ctx = ctx + m_ctx[5][:, None] * _expert_choice_ffn(h2c, router_w[l], wg, wu, wd)
    return _rms_norm(x, final_norm_g)
```

```python
import functools

import numpy as np
import jax
import jax.numpy as jnp
from jax import lax
from jax.experimental import pallas as pl
from jax.experimental.pallas import tpu as pltpu

F32 = jnp.float32
BF16 = jnp.bfloat16

GRID_W = 64
CHUNK = 64
CONV_W = 5
POOL_WINDOWS = (2, 4, 8, 16)
SSD_HEAD_DIM = 64
SSD_GROUPS = 2
SSD_STATE = 128
GDN_HEAD_DIM = 64
N_EXPERTS = 16
EC_CAPACITY = 2
EPS = 1e-6
LANE = 128
HALO = 8
NEG = -1e30
PIECE = 8
I32 = jnp.int32
VMEM_LIMIT = 48 * 1024 * 1024


def _round_up(n, m):
    return (n + m - 1) // m * m


def _params():
    return pltpu.CompilerParams(dimension_semantics=("arbitrary", "arbitrary"), vmem_limit_bytes=VMEM_LIMIT)


def _inproj_kernel(x_ref, sc_ref, sh_ref, g_ref, w_ref, *o_refs):
    x = x_ref[0]
    ms = jnp.mean(x * x, axis=-1, keepdims=True)
    h = x * lax.rsqrt(ms + EPS) * g_ref[...]
    h = h * (1.0 + sc_ref[0]) + sh_ref[0]
    y = jnp.dot(h.astype(BF16), w_ref[...], preferred_element_type=F32)
    off = 0
    for o_ref in o_refs:
        n = o_ref.shape[-1]
        o_ref[0] = y[:, off:off + n]
        off += n


def _inproj(x, scale, shift, g, w, widths, *, column_major):
    b, t, d = x.shape
    if column_major:
        rows = t // GRID_W
        xv = x.reshape(b, rows, GRID_W * d)
        grid = (b, GRID_W)
        x_spec = pl.BlockSpec((1, rows, d), lambda i, j: (i, 0, j))
        tm = rows
    else:
        tm = min(t, 512)
        xv = x
        grid = (b, t // tm)
        x_spec = pl.BlockSpec((1, tm, d), lambda i, j: (i, j, 0))
    return pl.pallas_call(
        _inproj_kernel,
        grid=grid,
        in_specs=[
            x_spec,
            pl.BlockSpec((1, 1, d), lambda i, j: (i, 0, 0)),
            pl.BlockSpec((1, 1, d), lambda i, j: (i, 0, 0)),
            pl.BlockSpec((1, d), lambda i, j: (0, 0)),
            pl.BlockSpec((d, w.shape[1]), lambda i, j: (0, 0)),
        ],
        out_specs=[pl.BlockSpec((1, tm, n), lambda i, j: (i, j, 0)) for n in widths],
        out_shape=[jax.ShapeDtypeStruct((b, t, n), F32) for n in widths],
        compiler_params=_params(),
        name="inproj",
    )(xv, scale, shift, g, w)


def _outproj_kernel(*refs, n_in):
    a_refs = refs[:n_in]
    x_ref, gate_ref = refs[n_in:n_in + 2]
    w_refs = refs[n_in + 2:2 * n_in + 2]
    o_ref = refs[2 * n_in + 2]
    y = jnp.dot(a_refs[0][0].astype(BF16), w_refs[0][...], preferred_element_type=F32)
    for a_ref, w_ref in zip(a_refs[1:], w_refs[1:]):
        y = y + jnp.dot(a_ref[0].astype(BF16), w_ref[...], preferred_element_type=F32)
    o_ref[0] = x_ref[0] + gate_ref[0] * y


def _outproj(parts, x, gate, ws):
    b, t, d = x.shape
    tm = min(t, 512)
    n_in = len(parts)
    return pl.pallas_call(
        functools.partial(_outproj_kernel, n_in=n_in),
        grid=(b, t // tm),
        in_specs=(
            [pl.BlockSpec((1, tm, p.shape[-1]), lambda i, j: (i, j, 0)) for p in parts]
            + [pl.BlockSpec((1, tm, d), lambda i, j: (i, j, 0)), pl.BlockSpec((1, 1, d), lambda i, j: (i, 0, 0))]
            + [pl.BlockSpec(w.shape, lambda i, j: (0, 0)) for w in ws]),
        out_specs=pl.BlockSpec((1, tm, d), lambda i, j: (i, j, 0)),
        out_shape=jax.ShapeDtypeStruct((b, t, d), F32),
        compiler_params=_params(),
        name="outproj",
    )(*parts, x, gate, *ws)


def _pool_kernel(u_ref, prev_ref, next_ref, w_ref, sc_ref, o_ref, e0_ref, e1_ref, *, block, seq, group):
    TM = block
    j = pl.program_id(1)
    nb = pl.num_programs(1)
    u = u_ref[0]
    n_ext = TM + 2 * HALO
    e0_ref[0:HALO, :] = jnp.where(j > 0, prev_ref[0], 0.0)
    e0_ref[HALO:HALO + TM, :] = u
    e0_ref[HALO + TM:, :] = jnp.where(j < nb - 1, next_ref[0], 0.0)
    tok = j * TM + lax.broadcasted_iota(jnp.int32, (TM, 1), 0)
    lane = lax.broadcasted_iota(jnp.int32, (1, u.shape[-1]), 1)
    src, dst = e0_ref, e1_ref
    pooled = jnp.zeros_like(u)
    half = 1
    for gi, win in enumerate(POOL_WINDOWS):
        assert win == 2 * half
        lo_r, hi_r = half, n_ext - half
        if half == 1:
            dst[lo_r:hi_r, :] = src[lo_r - 1:hi_r - 1, :] + src[lo_r:hi_r, :]
        else:
            q = half // 2
            dst[lo_r:hi_r, :] = src[lo_r - q:hi_r - q, :] + src[lo_r + q:hi_r + q, :]
        cnt = (jnp.minimum(tok + half, seq) - jnp.maximum(tok - half, 0)).astype(F32)
        mean = dst[HALO:HALO + TM, :] / cnt
        pooled = jnp.where((lane >= gi * group) & (lane < (gi + 1) * group), mean, pooled)
        src, dst = dst, src
        half *= 2
    dd = pooled - u
    y = jnp.dot(dd.astype(BF16), w_ref[...], preferred_element_type=F32)
    o_ref[0] = y * sc_ref[...]


def _pool_branch(u, pool_w, pool_scale):
    b, t, c = u.shape
    ng, pg, _ = pool_w.shape
    tm = min(t, 512)
    hb = tm // HALO
    w_bd = jnp.zeros((c, c), F32)
    for gi in range(ng):
        w_bd = w_bd.at[gi * pg:(gi + 1) * pg, gi * pg:(gi + 1) * pg].set(pool_w[gi])
    return pl.pallas_call(
        functools.partial(_pool_kernel, block=tm, seq=t, group=pg),
        grid=(b, t // tm),
        in_specs=[
            pl.BlockSpec((1, tm, c), lambda i, j: (i, j, 0)),
            pl.BlockSpec((1, HALO, c), lambda i, j: (i, jnp.maximum(j * hb - 1, 0), 0)),
            pl.BlockSpec((1, HALO, c), lambda i, j: (i, jnp.minimum((j + 1) * hb, t // HALO - 1), 0)),
            pl.BlockSpec((c, c), lambda i, j: (0, 0)),
            pl.BlockSpec((1, c), lambda i, j: (0, 0)),
        ],
        out_specs=pl.BlockSpec((1, tm, c), lambda i, j: (i, j, 0)),
        out_shape=jax.ShapeDtypeStruct((b, t, c), F32),
        scratch_shapes=[pltpu.VMEM((tm + 2 * HALO, c), F32), pltpu.VMEM((tm + 2 * HALO, c), F32)],
        compiler_params=_params(),
        name="pool",
    )(u, u, u, w_bd.astype(BF16), pool_scale[None])


def _ffn_kernel(x_ref, gate_ref, wg_ref, wu_ref, wd_ref, o_ref):
    x = x_ref[0].astype(BF16)
    hg = jnp.dot(x, wg_ref[0], preferred_element_type=F32)
    hu = jnp.dot(x, wu_ref[0], preferred_element_type=F32)
    hid = (hg * jax.nn.sigmoid(hg)) * hu
    y = jnp.dot(hid.astype(BF16), wd_ref[0], preferred_element_type=F32)
    o_ref[0] = y * gate_ref[0]


def _expert_ffn(xe, gate, wg, wu, wd):
    be, c, d = xe.shape
    e, _, f = wg.shape
    tc = min(c, 512)
    return pl.pallas_call(
        _ffn_kernel,
        grid=(be, c // tc),
        in_specs=[
            pl.BlockSpec((1, tc, d), lambda i, j: (i, j, 0)),
            pl.BlockSpec((1, tc, 1), lambda i, j: (i, j, 0)),
            pl.BlockSpec((1, d, f), lambda i, j: (i % e, 0, 0)),
            pl.BlockSpec((1, d, f), lambda i, j: (i % e, 0, 0)),
            pl.BlockSpec((1, f, d), lambda i, j: (i % e, 0, 0)),
        ],
        out_specs=pl.BlockSpec((1, tc, d), lambda i, j: (i, j, 0)),
        out_shape=jax.ShapeDtypeStruct((be, c, d), F32),
        compiler_params=_params(),
        name="expert_ffn",
    )(xe, gate, wg, wu, wd)


def _split3(a):
    hi = a.astype(BF16)
    r1 = a - hi.astype(F32)
    mid = r1.astype(BF16)
    lo = (r1 - mid.astype(F32)).astype(BF16)
    return hi, mid, lo


def _cumsum_both(la, incl):
    parts = _split3(la)
    cs_col = sum(jnp.dot(incl, p, preferred_element_type=F32) for p in parts)
    cs_row = sum(lax.dot_general(p, incl, (((0,), (1,)), ((), ())), preferred_element_type=F32) for p in parts)
    return cs_col, cs_row


def _ssd_kernel(*refs, chunk, rev, final, heads):
    if final:
        (xbc_ref, prev_ref, next_ref, dt_ref, cw_ref, cb_ref, dtb_ref, a_ref, dsk_ref, s0_ref,
         yb_ref, z_ref, ng_ref, y_ref, sfin_ref, s_ref, ext_ref) = refs
    else:
        (xbc_ref, prev_ref, next_ref, dt_ref, cw_ref, cb_ref, dtb_ref, a_ref, dsk_ref, s0_ref,
         y_ref, sfin_ref, s_ref, ext_ref) = refs
    L = chunk
    i = pl.program_id(1)
    nc = pl.num_programs(1)
    j = nc - 1 - i if rev else i
    hd = SSD_HEAD_DIM
    ssd_dim = heads * hd
    rep = heads // SSD_GROUPS

    @pl.when(i == 0)
    def _():
        s_ref[...] = s0_ref[0]

    ext_ref[0:HALO, :] = jnp.where(j > 0, prev_ref[0], 0.0)
    ext_ref[HALO:HALO + L, :] = xbc_ref[0]
    ext_ref[HALO + L:, :] = jnp.where(j < nc - 1, next_ref[0], 0.0)
    base = HALO - CONV_W // 2
    acc = cb_ref[...] + cw_ref[0:1, :] * ext_ref[base:base + L, :]
    for k in range(1, CONV_W):
        acc = acc + cw_ref[k:k + 1, :] * ext_ref[base + k:base + k + L, :]
    act = acc * jax.nn.sigmoid(acc)
    xs = act[:, :ssd_dim]
    bmat = act[:, ssd_dim:ssd_dim + SSD_GROUPS * SSD_STATE]
    cmat = act[:, ssd_dim + SSD_GROUPS * SSD_STATE:]

    dtv = jax.nn.softplus(dt_ref[0] + dtb_ref[...])
    la = dtv * a_ref[...]
    row = lax.broadcasted_iota(jnp.int32, (L, L), 0)
    col = lax.broadcasted_iota(jnp.int32, (L, L), 1)
    mask = (row <= col) if rev else (row >= col)
    cs_col, cs_row = _cumsum_both(la, mask.astype(BF16))
    last = 0 if rev else L - 1

    gmats = []
    for g in range(SSD_GROUPS):
        cg = cmat[:, g * SSD_STATE:(g + 1) * SSD_STATE].astype(BF16)
        bg = bmat[:, g * SSD_STATE:(g + 1) * SSD_STATE].astype(BF16)
        gmats.append(lax.dot_general(cg, bg, (((1,), (1,)), ((), ())), preferred_element_type=F32))

    ys = []
    for h in range(heads):
        g = h // rep
        ci = (heads if rev else 0) + h
        csc = cs_col[:, ci:ci + 1]
        csr = cs_row[ci:ci + 1, :]
        tot = csc[last:last + 1, :]
        dec = jnp.exp(jnp.where(mask, csc - csr, NEG))
        xs_h = xs[:, h * hd:(h + 1) * hd]
        xdt = (xs_h * dtv[:, ci:ci + 1]).astype(BF16)
        b_g = bmat[:, g * SSD_STATE:(g + 1) * SSD_STATE]
        c_g = cmat[:, g * SSD_STATE:(g + 1) * SSD_STATE]
        s_h = s_ref[h]
        y_h = jnp.dot((gmats[g] * dec).astype(BF16), xdt, preferred_element_type=F32)
        y_h = y_h + jnp.dot((c_g * jnp.exp(csc)).astype(BF16), s_h.astype(BF16), preferred_element_type=F32)
        local = lax.dot_general((b_g * jnp.exp(tot - csc)).astype(BF16), xdt, (((0,), (0,)), ((), ())),
                                preferred_element_type=F32)
        s_ref[h] = s_h * jnp.exp(tot) + local
        if final:
            y_h = y_h + dsk_ref[:, h * hd:(h + 1) * hd] * xs_h
        ys.append(y_h)
    y = jnp.concatenate(ys, axis=-1)
    if final:
        y = y + yb_ref[0]
        z = z_ref[0]
        y = y * (z * jax.nn.sigmoid(z))
        y = y * lax.rsqrt(jnp.mean(y * y, axis=-1, keepdims=True) + EPS) * ng_ref[...]
    y_ref[0] = y

    @pl.when(i == nc - 1)
    def _():
        sfin_ref[0] = s_ref[...]


def _ssd_pass(xbc, dt, cw, cb, dtb, a, dsk, s0, *, rev, final_inputs=None):
    b, t, width = xbc.shape
    heads = s0.shape[1]
    ssd_dim = heads * SSD_HEAD_DIM
    L = min(t, 256)
    nc = t // L
    hb = L // HALO
    final = final_inputs is not None

    def cidx(i):
        return nc - 1 - i if rev else i

    in_specs = [
        pl.BlockSpec((1, L, width), lambda bi, i: (bi, cidx(i), 0)),
        pl.BlockSpec((1, HALO, width), lambda bi, i: (bi, jnp.maximum(cidx(i) * hb - 1, 0), 0)),
        pl.BlockSpec((1, HALO, width), lambda bi, i: (bi, jnp.minimum((cidx(i) + 1) * hb, t // HALO - 1), 0)),
        pl.BlockSpec((1, L, LANE), lambda bi, i: (bi, cidx(i), 0)),
        pl.BlockSpec((8, width), lambda bi, i: (0, 0)),
        pl.BlockSpec((1, width), lambda bi, i: (0, 0)),
        pl.BlockSpec((1, LANE), lambda bi, i: (0, 0)),
        pl.BlockSpec((1, LANE), lambda bi, i: (0, 0)),
        pl.BlockSpec((1, ssd_dim), lambda bi, i: (0, 0)),
        pl.BlockSpec((1, heads, SSD_STATE, SSD_HEAD_DIM), lambda bi, i: (bi, 0, 0, 0)),
    ]
    args = [xbc, xbc, xbc, dt, cw, cb, dtb, a, dsk, s0]
    if final:
        yb, z, ng = final_inputs
        in_specs += [
            pl.BlockSpec((1, L, ssd_dim), lambda bi, i: (bi, cidx(i), 0)),
            pl.BlockSpec((1, L, ssd_dim), lambda bi, i: (bi, cidx(i), 0)),
            pl.BlockSpec((1, ssd_dim), lambda bi, i: (0, 0)),
        ]
        args += [yb, z, ng]
    return pl.pallas_call(
        functools.partial(_ssd_kernel, chunk=L, rev=rev, final=final, heads=heads),
        grid=(b, nc),
        in_specs=in_specs,
        out_specs=[
            pl.BlockSpec((1, L, ssd_dim), lambda bi, i: (bi, cidx(i), 0)),
            pl.BlockSpec((1, heads, SSD_STATE, SSD_HEAD_DIM), lambda bi, i: (bi, 0, 0, 0)),
        ],
        out_shape=[
            jax.ShapeDtypeStruct((b, t, ssd_dim), F32),
            jax.ShapeDtypeStruct((b, heads, SSD_STATE, SSD_HEAD_DIM), F32),
        ],
        scratch_shapes=[
            pltpu.VMEM((heads, SSD_STATE, SSD_HEAD_DIM), F32),
            pltpu.VMEM((L + 2 * HALO, width), F32),
        ],
        compiler_params=_params(),
        name="ssd_bwd" if rev else "ssd_fwd",
    )(*args)


def _ssd_stream(z, xbc, dt, conv_w, conv_b, a_log, dt_bias, d_skip, norm_g, s0_f, s0_b):
    heads = s0_f.shape[1]
    cw = jnp.pad(conv_w, ((0, 8 - CONV_W), (0, 0)))
    cb = conv_b[None]
    dtb = jnp.pad(dt_bias.reshape(1, -1), ((0, 0), (0, LANE - 2 * heads)))
    a = jnp.pad(-jnp.exp(a_log).reshape(1, -1), ((0, 0), (0, LANE - 2 * heads)))
    dsk = jnp.repeat(d_skip, SSD_HEAD_DIM)[None]
    yb, s_b = _ssd_pass(xbc, dt, cw, cb, dtb, a, dsk, s0_b, rev=True)
    y, s_f = _ssd_pass(xbc, dt, cw, cb, dtb, a, dsk, s0_f, rev=False, final_inputs=(yb, z, norm_g[None]))
    return y, s_f, s_b


def _group_sum(a, ones_bd):
    hi = a.astype(BF16)
    lo = (a - hi.astype(F32)).astype(BF16)
    return (jnp.dot(hi, ones_bd, preferred_element_type=F32) + jnp.dot(lo, ones_bd, preferred_element_type=F32))


def _mm_bf16(a, b):
    return jnp.dot(a.astype(BF16), b.astype(BF16), preferred_element_type=F32)


def _unit_tri_inverse_many(xms, rev):
    n = xms[0].shape[0]
    row = lax.broadcasted_iota(jnp.int32, (n, n), 0)
    col = lax.broadcasted_iota(jnp.int32, (n, n), 1)
    eye = (row == col).astype(F32)
    ds = None
    m, sh = 1, 0
    while m < n:
        same = (row >> (sh + 1)) == (col >> (sh + 1))
        rbit = (row >> sh) & 1
        cbit = (col >> sh) & 1
        sel = same & ((rbit == 0) & (cbit == 1) if rev else (rbit == 1) & (cbit == 0))
        cs = [jnp.where(sel, x, 0.0) for x in xms]
        if ds is None:
            ds = [eye - c for c in cs]
        else:
            es = [_mm_bf16(c, d) for c, d in zip(cs, ds)]
            ds = [d - _mm_bf16(d, e) for d, e in zip(ds, es)]
        m, sh = 2 * m, sh + 1
    return ds


def _gdn_kernel(*refs, block, rev, final, heads):
    if final:
        (qkv_ref, prev_ref, next_ref, ab_ref, cw_ref, dtb_ref, a_ref, s0_ref, ob_ref, gate_ref, ng_ref,
         o_ref, sfin_ref, s_ref, ext_ref) = refs
    else:
        (qkv_ref, prev_ref, next_ref, ab_ref, cw_ref, dtb_ref, a_ref, s0_ref,
         o_ref, sfin_ref, s_ref, ext_ref) = refs
    TB = block
    L = block
    hd = GDN_HEAD_DIM
    dim = heads * hd
    i = pl.program_id(1)
    nb = pl.num_programs(1)
    j = nb - 1 - i if rev else i

    @pl.when(i == 0)
    def _():
        s_ref[...] = s0_ref[0]

    ext_ref[0:HALO, :] = jnp.where(j > 0, prev_ref[0], 0.0)
    ext_ref[HALO:HALO + TB, :] = qkv_ref[0]
    ext_ref[HALO + TB:, :] = jnp.where(j < nb - 1, next_ref[0], 0.0)
    base = HALO - CONV_W // 2
    acc = cw_ref[0:1, :] * ext_ref[base:base + TB, :]
    for kk in range(1, CONV_W):
        acc = acc + cw_ref[kk:kk + 1, :] * ext_ref[base + kk:base + kk + TB, :]
    act = acc * jax.nn.sigmoid(acc)
    q = act[:, :dim]
    k = act[:, dim:2 * dim]
    v = act[:, 2 * dim:]
    ri = lax.broadcasted_iota(jnp.int32, (dim, dim), 0) // hd
    ci_ = lax.broadcasted_iota(jnp.int32, (dim, dim), 1) // hd
    ones_bd = (ri == ci_).astype(BF16)
    q = q * lax.rsqrt(_group_sum(q * q, ones_bd) + EPS) * (hd ** -0.5)
    k = k * lax.rsqrt(_group_sum(k * k, ones_bd) + EPS)
    ab = ab_ref[0]
    gl = a_ref[...] * jax.nn.softplus(ab + dtb_ref[...])
    beta = jax.nn.sigmoid(ab)

    row = lax.broadcasted_iota(jnp.int32, (L, L), 0)
    col = lax.broadcasted_iota(jnp.int32, (L, L), 1)
    incl = (row <= col) if rev else (row >= col)
    strict = (row < col) if rev else (row > col)
    last = 0 if rev else L - 1
    cs_col, cs_row = _cumsum_both(gl, incl.astype(BF16))

    hs = range(heads)
    cis = [(heads if rev else 0) + h for h in hs]
    qh = [q[:, h * hd:(h + 1) * hd] for h in hs]
    kh = [k[:, h * hd:(h + 1) * hd] for h in hs]
    vh = [v[:, h * hd:(h + 1) * hd] for h in hs]
    csc = [cs_col[:, ci:ci + 1] for ci in cis]
    bcol = [beta[:, 2 * heads + ci:2 * heads + ci + 1] for ci in cis]
    qk_kk = [lax.dot_general(jnp.concatenate([qh[h], kh[h]], axis=0).astype(BF16), kh[h].astype(BF16),
                             (((1,), (1,)), ((), ())), preferred_element_type=F32) for h in hs]
    dec = [jnp.exp(jnp.where(incl, csc[h] - cs_row[cis[h]:cis[h] + 1, :], NEG)) for h in hs]
    attn = [(qk_kk[h][:L] * dec[h]).astype(BF16) for h in hs]
    xm = [jnp.where(strict, bcol[h] * qk_kk[h][L:] * dec[h], 0.0) for h in hs]
    tm = _unit_tri_inverse_many(xm, rev)
    egc = [jnp.exp(cc) for cc in csc]
    uw = [jnp.dot(tm[h].astype(BF16),
                  jnp.concatenate([vh[h] * bcol[h], kh[h] * (bcol[h] * egc[h])], axis=1).astype(BF16),
                  preferred_element_type=F32) for h in hs]
    s_old = [s_ref[h] for h in hs]
    rs = [jnp.dot(jnp.concatenate([uw[h][:, hd:], qh[h] * egc[h]], axis=0).astype(BF16), s_old[h].astype(BF16),
                  preferred_element_type=F32) for h in hs]
    vnb = [(uw[h][:, :hd] - rs[h][:L]).astype(BF16) for h in hs]
    outs = [rs[h][L:] + jnp.dot(attn[h], vnb[h], preferred_element_type=F32) for h in hs]
    for h in hs:
        tot = csc[h][last:last + 1, :]
        kend = (kh[h] * jnp.exp(tot - csc[h])).astype(BF16)
        s_ref[h] = s_old[h] * jnp.exp(tot) + lax.dot_general(kend, vnb[h], (((0,), (0,)), ((), ())),
                                                            preferred_element_type=F32)
    o = jnp.concatenate(outs, axis=-1)
    if final:
        o = o + ob_ref[0]
        ms = _group_sum(o * o, ones_bd) * (1.0 / hd)
        gate = gate_ref[0]
        o = o * lax.rsqrt(ms + EPS) * ng_ref[...] * (gate * jax.nn.sigmoid(gate))
    o_ref[0] = o

    @pl.when(i == nb - 1)
    def _():
        sfin_ref[0] = s_ref[...]


def _gdn_pass(qkv, ab, cw, dtb, a, s0, *, rev, block, final_inputs=None, raster_out=False):
    b, t, width = qkv.shape
    heads = s0.shape[1]
    dim = heads * GDN_HEAD_DIM
    TB = block
    nb = t // TB
    hb = TB // HALO
    final = final_inputs is not None

    def bidx(i):
        return nb - 1 - i if rev else i

    in_specs = [
        pl.BlockSpec((1, TB, width), lambda bi, i: (bi, bidx(i), 0)),
        pl.BlockSpec((1, HALO, width), lambda bi, i: (bi, jnp.maximum(bidx(i) * hb - 1, 0), 0)),
        pl.BlockSpec((1, HALO, width), lambda bi, i: (bi, jnp.minimum((bidx(i) + 1) * hb, t // HALO - 1), 0)),
        pl.BlockSpec((1, TB, LANE), lambda bi, i: (bi, bidx(i), 0)),
        pl.BlockSpec((8, width), lambda bi, i: (0, 0)),
        pl.BlockSpec((1, LANE), lambda bi, i: (0, 0)),
        pl.BlockSpec((1, LANE), lambda bi, i: (0, 0)),
        pl.BlockSpec((1, heads, GDN_HEAD_DIM, GDN_HEAD_DIM), lambda bi, i: (bi, 0, 0, 0)),
    ]
    args = [qkv, qkv, qkv, ab, cw, dtb, a, s0]
    if final:
        ob, gate, ng = final_inputs
        in_specs += [
            pl.BlockSpec((1, TB, dim), lambda bi, i: (bi, bidx(i), 0)),
            pl.BlockSpec((1, TB, dim), lambda bi, i: (bi, bidx(i), 0)),
            pl.BlockSpec((1, dim), lambda bi, i: (0, 0)),
        ]
        args += [ob, gate, ng]
    if raster_out:
        assert nb == GRID_W
        o_spec = pl.BlockSpec((1, TB, dim), lambda bi, i: (bi, 0, bidx(i)))
        o_shape = jax.ShapeDtypeStruct((b, TB, GRID_W * dim), F32)
    else:
        o_spec = pl.BlockSpec((1, TB, dim), lambda bi, i: (bi, bidx(i), 0))
        o_shape = jax.ShapeDtypeStruct((b, t, dim), F32)
    o, s_fin = pl.pallas_call(
        functools.partial(_gdn_kernel, block=TB, rev=rev, final=final, heads=heads),
        grid=(b, nb),
        in_specs=in_specs,
        out_specs=[
            o_spec,
            pl.BlockSpec((1, heads, GDN_HEAD_DIM, GDN_HEAD_DIM), lambda bi, i: (bi, 0, 0, 0)),
        ],
        out_shape=[
            o_shape,
            jax.ShapeDtypeStruct((b, heads, GDN_HEAD_DIM, GDN_HEAD_DIM), F32),
        ],
        scratch_shapes=[
            pltpu.VMEM((heads, GDN_HEAD_DIM, GDN_HEAD_DIM), F32),
            pltpu.VMEM((TB + 2 * HALO, width), F32),
        ],
        compiler_params=_params(),
        name="gdn_bwd" if rev else "gdn_fwd",
    )(*args)
    return o.reshape(b, t, dim), s_fin


def _gdn_stream(qkv, gate, ab, conv_w, a_log, dt_bias, norm_g, s0_f, s0_b, *, column_major):
    t = qkv.shape[1]
    heads = s0_f.shape[1]
    cw = jnp.pad(conv_w, ((0, 8 - CONV_W), (0, 0)))
    dtb = jnp.pad(dt_bias.reshape(1, -1), ((0, 0), (0, LANE - 2 * heads)))
    a = jnp.pad(-jnp.exp(a_log).reshape(1, -1), ((0, 0), (0, LANE - 2 * heads)))
    ng = jnp.tile(norm_g, heads)[None]
    rows = t // GRID_W
    fused_raster = column_major and rows <= 256 and rows >= 2 * HALO and rows & (rows - 1) == 0
    block = rows if fused_raster else min(t, 256)
    ob, s_b = _gdn_pass(qkv, ab, cw, dtb, a, s0_b, rev=True, block=block)
    o, s_f = _gdn_pass(qkv, ab, cw, dtb, a, s0_f, rev=False, block=block, final_inputs=(ob, gate, ng),
                       raster_out=fused_raster)
    if column_major and not fused_raster:
        o = _to_raster(o)
    return o, s_f, s_b


def _router_kernel(x_ref, sc_ref, sh_ref, g_ref, rw_ref, h_ref, a_ref):
    x = x_ref[0]
    ms = jnp.mean(x * x, axis=-1, keepdims=True)
    h = x * lax.rsqrt(ms + EPS) * g_ref[...]
    h = h * (1.0 + sc_ref[0]) + sh_ref[0]
    h_ref[0] = h
    hh, hm, _ = _split3(h)
    rw = rw_ref[...]
    rh = rw.astype(BF16)
    rm = (rw - rh.astype(F32)).astype(BF16)
    nt = (((1,), (1,)), ((), ()))
    lg = (lax.dot_general(rh, hh, nt, preferred_element_type=F32)
          + lax.dot_general(rh, hm, nt, preferred_element_type=F32)
          + lax.dot_general(rm, hh, nt, preferred_element_type=F32))
    ex = jnp.exp(lg - jnp.max(lg, axis=0, keepdims=True))
    aff = ex / jnp.sum(ex, axis=0, keepdims=True)
    for k in range(a_ref.shape[1]):
        a_ref[0, k] = aff[:, k * LANE:(k + 1) * LANE]


def _router(x, scale, shift, g, rw_t):
    b, t, d = x.shape
    e = rw_t.shape[0]
    tm = min(t, 1024)
    return pl.pallas_call(
        _router_kernel,
        grid=(b, t // tm),
        in_specs=[
            pl.BlockSpec((1, tm, d), lambda i, j: (i, j, 0)),
            pl.BlockSpec((1, 1, d), lambda i, j: (i, 0, 0)),
            pl.BlockSpec((1, 1, d), lambda i, j: (i, 0, 0)),
            pl.BlockSpec((1, d), lambda i, j: (0, 0)),
            pl.BlockSpec((e, d), lambda i, j: (0, 0)),
        ],
        out_specs=[
            pl.BlockSpec((1, tm, d), lambda i, j: (i, j, 0)),
            pl.BlockSpec((1, tm // LANE, e, LANE), lambda i, j: (i, j, 0, 0)),
        ],
        out_shape=[
            jax.ShapeDtypeStruct((b, t, d), F32),
            jax.ShapeDtypeStruct((b, t // LANE, e, LANE), F32),
        ],
        compiler_params=_params(),
        name="router",
    )(x, scale, shift, g, rw_t)


def _token_prefix(m3, ut, ones, lt):
    e, nb, _ = m3.shape
    m2 = m3.reshape(e * nb, LANE).astype(BF16)
    inb = jnp.dot(m2, ut, preferred_element_type=F32).reshape(e, nb, LANE)
    tot = jnp.dot(m2, ones, preferred_element_type=F32).reshape(e, nb, LANE)
    offs = jnp.stack([jnp.dot(lt, tot[i].astype(BF16), preferred_element_type=F32) for i in range(e)], axis=0)
    return inb, tot, offs


def _select_kernel(a_ref, idx_ref, gate_ref, srow_ref, st8_ref, npc_ref, cs_ref, cum_ref, cnt_ref, *, cap):
    a = a_ref[0]
    e_n, nb, _ = a.shape
    bits = lax.bitcast_convert_type(a, I32)

    def radix(i, prefix):
        cand = prefix | jnp.left_shift(jnp.int32(1), 30 - i)
        cnt = jnp.sum(jnp.sum((bits >= cand).astype(F32), axis=2, keepdims=True), axis=1, keepdims=True)
        return jnp.where(cnt >= cap, cand, prefix)

    thr = lax.fori_loop(0, 31, radix, jnp.zeros((e_n, 1, 1), I32))
    li = lax.broadcasted_iota(I32, (LANE, LANE), 0)
    lj = lax.broadcasted_iota(I32, (LANE, LANE), 1)
    ut = (li < lj).astype(BF16)
    ones = jnp.ones((LANE, LANE), BF16)
    bi = lax.broadcasted_iota(I32, (nb, nb), 0)
    bj = lax.broadcasted_iota(I32, (nb, nb), 1)
    lt = (bj < bi).astype(BF16)

    gt = bits > thr
    eq = bits == thr
    n_gt = jnp.sum(jnp.sum(gt.astype(F32), axis=2, keepdims=True), axis=1, keepdims=True)
    tie_in, _, tie_offs = _token_prefix(eq.astype(F32), ut, ones, lt)
    sel = gt | (eq & (tie_in + tie_offs < cap - n_gt))
    pos_in, cnt, offs = _token_prefix(sel.astype(F32), ut, ones, lt)
    cs_ref[...] = jnp.where(sel, pos_in + 1.0, 0.0)
    cum_ref[...] = offs + cnt
    cnt_ref[...] = cnt

    offs_i = offs.astype(I32)
    cnt_i = cnt.astype(I32)
    st8 = (offs_i >> 3) << 3
    npc = jnp.where(cnt_i > 0, (offs_i + cnt_i - st8 + (PIECE - 1)) >> 3, 0)
    rbase = []
    run = jnp.zeros((nb, LANE), I32)
    for i in range(e_n):
        rbase.append(run)
        run = run + PIECE * npc[i]
    rbase = jnp.stack(rbase, axis=0)
    srow_ref[0] = jnp.where(sel, rbase + pos_in.astype(I32) + offs_i - st8, -1)
    st8_ref[0] = st8
    npc_ref[0] = npc

    jrow = lax.broadcasted_iota(I32, (1, cap), 1).astype(F32)
    sub_nb = lax.broadcasted_iota(I32, (nb, cap), 0).astype(F32)
    sub_l = lax.broadcasted_iota(I32, (LANE, cap), 0).astype(F32)
    tn = (((0,), (0,)), ((), ()))

    def compact(ei, carry):
        cum_col = cum_ref[ei][:, 0:1]
        cnt_col = cnt_ref[ei][:, 0:1]
        ge = cum_col <= jrow
        blk_j = jnp.sum(ge.astype(F32), axis=0, keepdims=True)
        offs_j = jnp.sum(jnp.where(ge, cnt_col, 0.0), axis=0, keepdims=True)
        rank1 = jrow - offs_j + 1.0
        g_t = (sub_nb == blk_j).astype(BF16)
        row_t = lax.dot_general(cs_ref[ei].astype(BF16), g_t, tn, preferred_element_type=F32)
        match = row_t == rank1
        lane_j = jnp.sum(jnp.where(match, sub_l, 0.0), axis=0, keepdims=True)
        idx_ref[0, pl.ds(ei, 1), :] = (blk_j * LANE + lane_j).astype(I32)
        parts = _split3(a_ref[0, ei])
        aff_t = sum(lax.dot_general(p, g_t, tn, preferred_element_type=F32) for p in parts)
        gate_ref[0, pl.ds(ei, 1), :] = jnp.sum(jnp.where(match, aff_t, 0.0), axis=0, keepdims=True)
        return carry

    lax.fori_loop(0, e_n, compact, 0)


def _ec_select(aff_em, cap):
    b, e, nb, _ = aff_em.shape
    big = lambda dt: jax.ShapeDtypeStruct((b, e, nb, LANE), dt)
    spec4 = pl.BlockSpec((1, e, nb, LANE), lambda i: (i, 0, 0, 0))
    spec3 = pl.BlockSpec((1, e, cap), lambda i: (i, 0, 0))
    return pl.pallas_call(
        functools.partial(_select_kernel, cap=cap),
        grid=(b,),
        in_specs=[spec4],
        out_specs=[spec3, spec3, spec4, spec4, spec4],
        out_shape=[jax.ShapeDtypeStruct((b, e, cap), I32), jax.ShapeDtypeStruct((b, e, cap), F32),
                   big(I32), big(I32), big(I32)],
        scratch_shapes=[pltpu.VMEM((e, nb, LANE), F32)] * 3,
        compiler_params=pltpu.CompilerParams(dimension_semantics=("arbitrary",), vmem_limit_bytes=VMEM_LIMIT),
        name="ec_select",
    )(aff_em)


def _combine_kernel(st8_sm, npc_sm, ye_hbm, srow_ref, x_ref, g_ref, fn_ref, o_ref, stage, acc_ref, sem,
                    *, final, n_exp):
    b = pl.program_id(0)
    k = pl.program_id(1)
    nb = pl.num_programs(1)
    step = b * nb + k
    nsteps = pl.num_programs(0) * nb
    slot = step % 2

    def piece_copy(bb, e, src_row, sl, dst_row):
        return pltpu.make_async_copy(ye_hbm.at[bb, e, pl.ds(src_row, PIECE), :],
                                     stage.at[sl, pl.ds(dst_row, PIECE), :], sem.at[sl])

    def issue(st, sl):
        bb = st // nb

        def per_e(e, r):
            s8 = st8_sm[st * n_exp + e]
            n = npc_sm[st * n_exp + e]

            def per_p(p, r2):
                piece_copy(bb, e, pl.multiple_of(s8 + PIECE * p, PIECE), sl, pl.multiple_of(r2, PIECE)).start()
                return r2 + PIECE

            return lax.fori_loop(0, n, per_p, r)

        lax.fori_loop(0, n_exp, per_e, 0)

    @pl.when(step == 0)
    def _():
        issue(step, slot)

    @pl.when(step + 1 < nsteps)
    def _():
        issue(step + 1, 1 - slot)

    npieces = lax.fori_loop(0, n_exp, lambda e, s: s + npc_sm[step * n_exp + e], 0)
    rows = npieces * PIECE

    def wait_one(p, c):
        piece_copy(0, 0, 0, slot, 0).wait()
        return c

    lax.fori_loop(0, npieces, wait_one, 0)
    stage[slot, pl.ds(pl.multiple_of(rows, PIECE), LANE), :] = jnp.zeros((LANE, stage.shape[-1]), F32)

    srow = srow_ref[0, 0]
    acc_ref[...] = jnp.zeros_like(acc_ref)
    riota = lax.broadcasted_iota(I32, (LANE, LANE), 0)
    tn = (((0,), (0,)), ((), ()))

    def chunk(c, carry):
        r0 = pl.multiple_of(c * LANE, LANE)
        rid = riota + r0
        pt = (srow[0:1, :] == rid).astype(F32)
        for e in range(1, n_exp):
            pt = pt + (srow[e:e + 1, :] == rid).astype(F32)
        ptb = pt.astype(BF16)
        st = stage[slot, pl.ds(r0, LANE), :]
        hi = st.astype(BF16)
        lo = (st - hi.astype(F32)).astype(BF16)
        acc_ref[...] += (lax.dot_general(ptb, hi, tn, preferred_element_type=F32)
                         + lax.dot_general(ptb, lo, tn, preferred_element_type=F32))
        return carry

    lax.fori_loop(0, (rows + LANE - 1) // LANE, chunk, 0)
    y = x_ref[0] + g_ref[0] * acc_ref[...]
    if final:
        y = y * lax.rsqrt(jnp.mean(y * y, axis=-1, keepdims=True) + EPS) * fn_ref[...]
    o_ref[0] = y


def _ec_combine(ye, srow_bm, st8, npc, x, gate, final_g=None):
    b, t, d = x.shape
    e = ye.shape[1]
    nb = t // LANE
    final = final_g is not None
    fn = final_g if final else jnp.ones((1, d), F32)
    max_rows = e * (LANE + 2 * PIECE) + LANE
    grid_spec = pltpu.PrefetchScalarGridSpec(
        num_scalar_prefetch=2,
        grid=(b, nb),
        in_specs=[
            pl.BlockSpec(memory_space=pl.ANY),
            pl.BlockSpec((1, 1, e, LANE), lambda i, j, *_: (i, j, 0, 0)),
            pl.BlockSpec((1, LANE, d), lambda i, j, *_: (i, j, 0)),
            pl.BlockSpec((1, 1, d), lambda i, j, *_: (i, 0, 0)),
            pl.BlockSpec((1, d), lambda i, j, *_: (0, 0)),
        ],
        out_specs=pl.BlockSpec((1, LANE, d), lambda i, j, *_: (i, j, 0)),
        scratch_shapes=[
            pltpu.VMEM((2, max_rows, d), F32),
            pltpu.VMEM((LANE, d), F32),
            pltpu.SemaphoreType.DMA((2,)),
        ],
    )
    return pl.pallas_call(
        functools.partial(_combine_kernel, final=final, n_exp=e),
        grid_spec=grid_spec,
        out_shape=jax.ShapeDtypeStruct((b, t, d), F32),
        compiler_params=_params(),
        name="ec_combine",
    )(st8, npc, ye, srow_bm, x, gate, fn)


def _expert_choice_block(x, scale, shift, g2, gate2, router_w, wg, wu, wd, final_g):
    b, t, d = x.shape
    e = router_w.shape[-1]
    cap = EC_CAPACITY * t // e
    h2, aff_bm = _router(x, scale, shift, g2, router_w.T)
    idx, gate, srow, st8, npc = _ec_select(jnp.transpose(aff_bm, (0, 2, 1, 3)), cap)
    xe = jax.vmap(lambda hb, ib: hb[ib])(h2, idx)
    ye = _expert_ffn(xe.reshape(b * e, cap, d), gate.reshape(b * e, cap, 1), wg, wu, wd).reshape(b, e, cap, d)
    srow_bm = jnp.transpose(srow, (0, 2, 1, 3))
    st8_f = jnp.transpose(st8[..., 0], (0, 2, 1)).reshape(-1)
    npc_f = jnp.transpose(npc[..., 0], (0, 2, 1)).reshape(-1)
    return _ec_combine(ye, srow_bm, st8_f, npc_f, x, gate2, final_g)


def _rms_norm(x, g):
    return x * lax.rsqrt(jnp.mean(x * x, axis=-1, keepdims=True) + EPS) * g


def _to_raster(u):
    b, t, c = u.shape
    rows = t // GRID_W
    return u.reshape(b, GRID_W, rows, c).transpose(0, 2, 1, 3).reshape(b, t, c)


def _expert_choice_ffn(h, router_w, wg, wu, wd):
    b, t, d = h.shape
    cap = EC_CAPACITY * t // N_EXPERTS
    aff = jax.nn.softmax(jnp.einsum('btd,de->bte', h, router_w, precision=lax.Precision.HIGHEST), axis=-1)
    gate, idx = lax.top_k(jnp.swapaxes(aff, 1, 2), cap)
    xe = jax.vmap(lambda hb, ib: hb[ib])(h, idx)
    ye = _expert_ffn(xe.reshape(b * N_EXPERTS, cap, d), gate.reshape(b * N_EXPERTS, cap, 1), wg, wu, wd)
    ye = ye.reshape(b, N_EXPERTS, cap, d)
    return jax.vmap(lambda ib, yb: jnp.zeros((t, d), yb.dtype).at[ib.reshape(-1)].add(yb.reshape(-1, d)))(idx, ye)


def kernel(x, c, ctx, c_ctx, norm1_g, norm2_g, ada_w, ada_b, w_in, w_out, pool_w, pool_scale, ssd_conv_w, ssd_conv_b, ssd_a_log, ssd_dt_bias, ssd_d, ssd_norm_g, gdn_conv_w, gdn_a_log, gdn_dt_bias, gdn_norm_g, router_w, exp_w_gate, exp_w_up, exp_w_down, final_norm_g):
    depth, d, _ = w_in.shape
    b, t, _ = x.shape
    pool_dim = pool_scale.shape[-1]
    ssd_dim = ssd_norm_g.shape[-1]
    ssd_heads = ssd_dim // SSD_HEAD_DIM
    ssd_bc = SSD_GROUPS * SSD_STATE
    gdn_dim = gdn_conv_w.shape[-1] // 3
    gdn_heads = gdn_dim // GDN_HEAD_DIM
    splits = (pool_dim, ssd_dim, ssd_dim + 2 * ssd_bc, 2 * ssd_heads, 3 * gdn_dim, gdn_dim, 2 * gdn_heads,
              2 * gdn_heads)
    cut = [0] + np.cumsum(splits).tolist()
    r_widths = (ssd_dim, ssd_dim + 2 * ssd_bc, pool_dim, LANE)
    g_widths = (3 * gdn_dim, gdn_dim, LANE)

    sc = jax.nn.silu(c)
    scc = jax.nn.silu(c_ctx)[None]
    for l in range(depth):
        last = l == depth - 1
        wl = w_in[l]
        seg = [wl[:, cut[i]:cut[i + 1]] for i in range(8)]
        w_r = jnp.concatenate(
            [seg[1], seg[2], seg[0], jnp.pad(seg[3], ((0, 0), (0, LANE - 2 * ssd_heads)))], axis=1).astype(BF16)
        w_g = jnp.concatenate(
            [seg[4], seg[5], jnp.pad(jnp.concatenate([seg[6], seg[7]], axis=1), ((0, 0), (0, LANE - 4 * gdn_heads)))],
            axis=1).astype(BF16)
        w_o = w_out[l].astype(BF16)
        wg, wu, wd = exp_w_gate[l].astype(BF16), exp_w_up[l].astype(BF16), exp_w_down[l].astype(BF16)
        m_lat = jnp.split(sc @ ada_w[l] + ada_b[l], 6, axis=-1)
        m_ctx = [jnp.broadcast_to(m, (b, d)) for m in jnp.split(scc @ ada_w[l] + ada_b[l], 6, axis=-1)]
        g1 = norm1_g[l][None]

        def project(xx, mm, column_major):
            pr = _inproj(xx, mm[1][:, None], mm[0][:, None], g1, w_r, r_widths, column_major=False)
            pg = _inproj(xx, mm[1][:, None], mm[0][:, None], g1, w_g, g_widths, column_major=column_major)
            return pr, pg

        ssd_p = (ssd_conv_w[l], ssd_conv_b[l], ssd_a_log[l], ssd_dt_bias[l], ssd_d[l], ssd_norm_g[l])
        gdn_p = (gdn_conv_w[l], gdn_a_log[l], gdn_dt_bias[l], gdn_norm_g[l])
        zs = jnp.zeros((b, ssd_heads, SSD_STATE, SSD_HEAD_DIM), F32)
        zg = jnp.zeros((b, gdn_heads, GDN_HEAD_DIM, GDN_HEAD_DIM), F32)

        (c_z, c_xbc, c_pool, c_dt), (c_qkv, c_gate, c_ab) = project(ctx, m_ctx, False)
        (l_z, l_xbc, l_pool, l_dt), (l_qkv, l_gate, l_ab) = project(x, m_lat, True)

        s_ctx, ssd_sf, ssd_sb = _ssd_stream(c_z, c_xbc, c_dt, *ssd_p, zs, zs)
        g_ctx, gdn_sf, gdn_sb = _gdn_stream(c_qkv, c_gate, c_ab, *gdn_p, zg, zg, column_major=False)
        s_lat, _, _ = _ssd_stream(l_z, l_xbc, l_dt, *ssd_p, ssd_sf, ssd_sb)
        g_lat, _, _ = _gdn_stream(l_qkv, l_gate, l_ab, *gdn_p, gdn_sf, gdn_sb, column_major=True)
        w_o_parts = [w_o[:pool_dim], w_o[pool_dim:pool_dim + ssd_dim], w_o[pool_dim + ssd_dim:]]
        x = _outproj([_pool_branch(l_pool, pool_w[l], pool_scale[l]), s_lat, g_lat], x, m_lat[2][:, None],
                     w_o_parts)
        fin = final_norm_g[None] if last else None
        if t % 1024 == 0:
            x = _expert_choice_block(x, m_lat[4][:, None], m_lat[3][:, None], norm2_g[l][None], m_lat[5][:, None],
                                     router_w[l], wg, wu, wd, fin)
        else:
            h2 = _rms_norm(x, norm2_g[l]) * (1 + m_lat[4][:, None]) + m_lat[3][:, None]
            x = x + m_lat[5][:, None] * _expert_choice_ffn(h2, router_w[l], wg, wu, wd)
            if last:
                x = _rms_norm(x, final_norm_g)
        if not last:
            ctx = _outproj([_pool_branch(c_pool, pool_w[l], pool_scale[l]), s_ctx, g_ctx], ctx, m_ctx[2][:, None],
                           w_o_parts)
            h2c = _rms_norm(ctx, norm2_g[l]) * (1 + m_ctx[4][:, None]) + m_ctx[3][:, None]
            ctx = ctx + m_ctx[5][:, None] * _expert_choice_ffn(h2c, router_w[l], wg, wu, wd)
    return x
```

```python
import functools

import numpy as np
import jax
import jax.numpy as jnp
from jax import lax
from jax.experimental import pallas as pl
from jax.experimental.pallas import tpu as pltpu

F32 = jnp.float32
BF16 = jnp.bfloat16

GRID_W = 64
CHUNK = 64
CONV_W = 5
POOL_WINDOWS = (2, 4, 8, 16)
SSD_HEAD_DIM = 64
SSD_GROUPS = 2
SSD_STATE = 128
GDN_HEAD_DIM = 64
N_EXPERTS = 16
EC_CAPACITY = 2
EPS = 1e-6
LANE = 128
SUBLANES = 8
HALO = 8
NEG = -1e30
PIECE = 8
I32 = jnp.int32
VMEM_LIMIT = 48 * 1024 * 1024


def _round_up(n, m):
    return (n + m - 1) // m * m


def _params():
    return pltpu.CompilerParams(dimension_semantics=("arbitrary", "arbitrary"), vmem_limit_bytes=VMEM_LIMIT)


def _inproj_kernel(x_ref, sc_ref, sh_ref, g_ref, w_ref, *o_refs):
    x = x_ref[0]
    ms = jnp.mean(x * x, axis=-1, keepdims=True)
    h = x * lax.rsqrt(ms + EPS) * g_ref[...]
    h = h * (1.0 + sc_ref[0]) + sh_ref[0]
    y = jnp.dot(h.astype(BF16), w_ref[...], preferred_element_type=F32)
    off = 0
    for o_ref in o_refs:
        n = o_ref.shape[-1]
        o_ref[0] = y[:, off:off + n]
        off += n


def _inproj(x, scale, shift, g, w, widths, *, column_major):
    b, t, d = x.shape
    if column_major:
        rows = t // GRID_W
        xv = x.reshape(b, rows, GRID_W * d)
        grid = (b, GRID_W)
        x_spec = pl.BlockSpec((1, rows, d), lambda i, j: (i, 0, j))
        tm = rows
    else:
        tm = min(t, 512)
        xv = x
        grid = (b, t // tm)
        x_spec = pl.BlockSpec((1, tm, d), lambda i, j: (i, j, 0))
    return pl.pallas_call(
        _inproj_kernel,
        grid=grid,
        in_specs=[
            x_spec,
            pl.BlockSpec((1, 1, d), lambda i, j: (i, 0, 0)),
            pl.BlockSpec((1, 1, d), lambda i, j: (i, 0, 0)),
            pl.BlockSpec((1, d), lambda i, j: (0, 0)),
            pl.BlockSpec((d, w.shape[1]), lambda i, j: (0, 0)),
        ],
        out_specs=[pl.BlockSpec((1, tm, n), lambda i, j: (i, j, 0)) for n in widths],
        out_shape=[jax.ShapeDtypeStruct((b, t, n), F32) for n in widths],
        compiler_params=_params(),
        name="inproj",
    )(xv, scale, shift, g, w)


def _outproj_kernel(*refs, n_in):
    a_refs = refs[:n_in]
    x_ref, gate_ref = refs[n_in:n_in + 2]
    w_refs = refs[n_in + 2:2 * n_in + 2]
    o_ref = refs[2 * n_in + 2]
    y = jnp.dot(a_refs[0][0].astype(BF16), w_refs[0][...], preferred_element_type=F32)
    for a_ref, w_ref in zip(a_refs[1:], w_refs[1:]):
        y = y + jnp.dot(a_ref[0].astype(BF16), w_ref[...], preferred_element_type=F32)
    o_ref[0] = x_ref[0] + gate_ref[0] * y


def _outproj(parts, x, gate, ws):
    b, t, d = x.shape
    tm = min(t, 512)
    n_in = len(parts)
    return pl.pallas_call(
        functools.partial(_outproj_kernel, n_in=n_in),
        grid=(b, t // tm),
        in_specs=(
            [pl.BlockSpec((1, tm, p.shape[-1]), lambda i, j: (i, j, 0)) for p in parts]
            + [pl.BlockSpec((1, tm, d), lambda i, j: (i, j, 0)), pl.BlockSpec((1, 1, d), lambda i, j: (i, 0, 0))]
            + [pl.BlockSpec(w.shape, lambda i, j: (0, 0)) for w in ws]),
        out_specs=pl.BlockSpec((1, tm, d), lambda i, j: (i, j, 0)),
        out_shape=jax.ShapeDtypeStruct((b, t, d), F32),
        compiler_params=_params(),
        name="outproj",
    )(*parts, x, gate, *ws)


def _pool_kernel(u_ref, prev_ref, next_ref, w_ref, sc_ref, o_ref, e0_ref, e1_ref, *, block, seq, group):
    TM = block
    j = pl.program_id(1)
    nb = pl.num_programs(1)
    u = u_ref[0]
    n_ext = TM + 2 * HALO
    e0_ref[0:HALO, :] = jnp.where(j > 0, prev_ref[0], 0.0)
    e0_ref[HALO:HALO + TM, :] = u
    e0_ref[HALO + TM:, :] = jnp.where(j < nb - 1, next_ref[0], 0.0)
    tok = j * TM + lax.broadcasted_iota(jnp.int32, (TM, 1), 0)
    lane = lax.broadcasted_iota(jnp.int32, (1, u.shape[-1]), 1)
    src, dst = e0_ref, e1_ref
    pooled = jnp.zeros_like(u)
    half = 1
    for gi, win in enumerate(POOL_WINDOWS):
        assert win == 2 * half
        lo_r, hi_r = half, n_ext - half
        if half == 1:
            dst[lo_r:hi_r, :] = src[lo_r - 1:hi_r - 1, :] + src[lo_r:hi_r, :]
        else:
            q = half // 2
            dst[lo_r:hi_r, :] = src[lo_r - q:hi_r - q, :] + src[lo_r + q:hi_r + q, :]
        cnt = (jnp.minimum(tok + half, seq) - jnp.maximum(tok - half, 0)).astype(F32)
        mean = dst[HALO:HALO + TM, :] / cnt
        pooled = jnp.where((lane >= gi * group) & (lane < (gi + 1) * group), mean, pooled)
        src, dst = dst, src
        half *= 2
    dd = pooled - u
    y = jnp.dot(dd.astype(BF16), w_ref[...], preferred_element_type=F32)
    o_ref[0] = y * sc_ref[...]


def _pool_branch(u, pool_w, pool_scale):
    b, t, c = u.shape
    ng, pg, _ = pool_w.shape
    tm = min(t, 512)
    hb = tm // HALO
    w_bd = jnp.zeros((c, c), F32)
    for gi in range(ng):
        w_bd = w_bd.at[gi * pg:(gi + 1) * pg, gi * pg:(gi + 1) * pg].set(pool_w[gi])
    return pl.pallas_call(
        functools.partial(_pool_kernel, block=tm, seq=t, group=pg),
        grid=(b, t // tm),
        in_specs=[
            pl.BlockSpec((1, tm, c), lambda i, j: (i, j, 0)),
            pl.BlockSpec((1, HALO, c), lambda i, j: (i, jnp.maximum(j * hb - 1, 0), 0)),
            pl.BlockSpec((1, HALO, c), lambda i, j: (i, jnp.minimum((j + 1) * hb, t // HALO - 1), 0)),
            pl.BlockSpec((c, c), lambda i, j: (0, 0)),
            pl.BlockSpec((1, c), lambda i, j: (0, 0)),
        ],
        out_specs=pl.BlockSpec((1, tm, c), lambda i, j: (i, j, 0)),
        out_shape=jax.ShapeDtypeStruct((b, t, c), F32),
        scratch_shapes=[pltpu.VMEM((tm + 2 * HALO, c), F32), pltpu.VMEM((tm + 2 * HALO, c), F32)],
        compiler_params=_params(),
        name="pool",
    )(u, u, u, w_bd.astype(BF16), pool_scale[None])


def _ffn_kernel(x_ref, gate_ref, wg_ref, wu_ref, wd_ref, o_ref, wgb_ref, wub_ref, wdb_ref):
    @pl.when(pl.program_id(1) == 0)
    def _():
        wgb_ref[...] = wg_ref[0].astype(BF16)
        wub_ref[...] = wu_ref[0].astype(BF16)
        wdb_ref[...] = wd_ref[0].astype(BF16)

    x = x_ref[0].astype(BF16)
    hg = jnp.dot(x, wgb_ref[...], preferred_element_type=F32)
    hu = jnp.dot(x, wub_ref[...], preferred_element_type=F32)
    hid = (hg * jax.nn.sigmoid(hg)) * hu
    y = jnp.dot(hid.astype(BF16), wdb_ref[...], preferred_element_type=F32)
    o_ref[0] = y * gate_ref[0]


def _expert_ffn(xe, gate, wg, wu, wd):
    be, c, d = xe.shape
    e, _, f = wg.shape
    tc = min(c, 512)
    return pl.pallas_call(
        _ffn_kernel,
        grid=(be, c // tc),
        in_specs=[
            pl.BlockSpec((1, tc, d), lambda i, j: (i, j, 0)),
            pl.BlockSpec((1, tc, 1), lambda i, j: (i, j, 0)),
            pl.BlockSpec((1, d, f), lambda i, j: (i % e, 0, 0)),
            pl.BlockSpec((1, d, f), lambda i, j: (i % e, 0, 0)),
            pl.BlockSpec((1, f, d), lambda i, j: (i % e, 0, 0)),
        ],
        out_specs=pl.BlockSpec((1, tc, d), lambda i, j: (i, j, 0)),
        out_shape=jax.ShapeDtypeStruct((be, c, d), F32),
        scratch_shapes=[pltpu.VMEM((d, f), BF16), pltpu.VMEM((d, f), BF16), pltpu.VMEM((f, d), BF16)],
        compiler_params=_params(),
        name="expert_ffn",
    )(xe, gate, wg, wu, wd)


def _split3(a):
    hi = a.astype(BF16)
    r1 = a - hi.astype(F32)
    mid = r1.astype(BF16)
    lo = (r1 - mid.astype(F32)).astype(BF16)
    return hi, mid, lo


def _cumsum_both(la, incl):
    parts = _split3(la)
    cs_col = sum(jnp.dot(incl, p, preferred_element_type=F32) for p in parts)
    cs_row = sum(lax.dot_general(p, incl, (((0,), (1,)), ((), ())), preferred_element_type=F32) for p in parts)
    return cs_col, cs_row


def _ssd_kernel(*refs, chunk, rev, final, heads):
    if final:
        (xbc_ref, prev_ref, next_ref, dt_ref, cw_ref, cb_ref, dtb_ref, a_ref, dsk_ref, s0_ref,
         yb_ref, z_ref, ng_ref, y_ref, sfin_ref, s_ref, ext_ref) = refs
    else:
        (xbc_ref, prev_ref, next_ref, dt_ref, cw_ref, cb_ref, dtb_ref, a_ref, dsk_ref, s0_ref,
         y_ref, sfin_ref, act_ref, s_ref, ext_ref) = refs
    L = chunk
    i = pl.program_id(1)
    nc = pl.num_programs(1)
    j = nc - 1 - i if rev else i
    hd = SSD_HEAD_DIM
    ssd_dim = heads * hd
    rep = heads // SSD_GROUPS

    @pl.when(i == 0)
    def _():
        s_ref[...] = s0_ref[0]

    if final:
        act = xbc_ref[0]
    else:
        ext_ref[0:HALO, :] = jnp.where(j > 0, prev_ref[0], 0.0)
        ext_ref[HALO:HALO + L, :] = xbc_ref[0]
        ext_ref[HALO + L:, :] = jnp.where(j < nc - 1, next_ref[0], 0.0)
        base = HALO - CONV_W // 2
        acc = cb_ref[...] + cw_ref[0:1, :] * ext_ref[base:base + L, :]
        for k in range(1, CONV_W):
            acc = acc + cw_ref[k:k + 1, :] * ext_ref[base + k:base + k + L, :]
        act = acc * jax.nn.sigmoid(acc)
        act_ref[0] = act
    xs = act[:, :ssd_dim]
    bmat = act[:, ssd_dim:ssd_dim + SSD_GROUPS * SSD_STATE]
    cmat = act[:, ssd_dim + SSD_GROUPS * SSD_STATE:]

    dtv = jax.nn.softplus(dt_ref[0] + dtb_ref[...])
    la = dtv * a_ref[...]
    row = lax.broadcasted_iota(jnp.int32, (L, L), 0)
    col = lax.broadcasted_iota(jnp.int32, (L, L), 1)
    mask = (row <= col) if rev else (row >= col)
    cs_col, cs_row = _cumsum_both(la, mask.astype(BF16))
    last = 0 if rev else L - 1

    gmats = []
    for g in range(SSD_GROUPS):
        cg = cmat[:, g * SSD_STATE:(g + 1) * SSD_STATE].astype(BF16)
        bg = bmat[:, g * SSD_STATE:(g + 1) * SSD_STATE].astype(BF16)
        gmats.append(lax.dot_general(cg, bg, (((1,), (1,)), ((), ())), preferred_element_type=F32))

    ys = []
    for h in range(heads):
        g = h // rep
        ci = (heads if rev else 0) + h
        csc = cs_col[:, ci:ci + 1]
        csr = cs_row[ci:ci + 1, :]
        tot = csc[last:last + 1, :]
        dec = jnp.exp(jnp.where(mask, csc - csr, NEG))
        xs_h = xs[:, h * hd:(h + 1) * hd]
        xdt = (xs_h * dtv[:, ci:ci + 1]).astype(BF16)
        b_g = bmat[:, g * SSD_STATE:(g + 1) * SSD_STATE]
        c_g = cmat[:, g * SSD_STATE:(g + 1) * SSD_STATE]
        s_h = s_ref[h]
        y_h = jnp.dot((gmats[g] * dec).astype(BF16), xdt, preferred_element_type=F32)
        y_h = y_h + jnp.dot((c_g * jnp.exp(csc)).astype(BF16), s_h.astype(BF16), preferred_element_type=F32)
        local = lax.dot_general((b_g * jnp.exp(tot - csc)).astype(BF16), xdt, (((0,), (0,)), ((), ())),
                                preferred_element_type=F32)
        s_ref[h] = s_h * jnp.exp(tot) + local
        if final:
            y_h = y_h + dsk_ref[:, h * hd:(h + 1) * hd] * xs_h
        ys.append(y_h)
    y = jnp.concatenate(ys, axis=-1)
    if final:
        y = y + yb_ref[0]
        z = z_ref[0]
        y = y * (z * jax.nn.sigmoid(z))
        y = y * lax.rsqrt(jnp.mean(y * y, axis=-1, keepdims=True) + EPS) * ng_ref[...]
    y_ref[0] = y

    @pl.when(i == nc - 1)
    def _():
        sfin_ref[0] = s_ref[...]


def _ssd_pass(xbc, dt, cw, cb, dtb, a, dsk, s0, *, rev, final_inputs=None):
    b, t, width = xbc.shape
    heads = s0.shape[1]
    ssd_dim = heads * SSD_HEAD_DIM
    L = min(t, 256)
    nc = t // L
    hb = L // HALO
    final = final_inputs is not None

    def cidx(i):
        return nc - 1 - i if rev else i

    in_specs = [
        pl.BlockSpec((1, L, width), lambda bi, i: (bi, cidx(i), 0)),
        pl.BlockSpec((1, HALO, width), lambda bi, i: (bi, jnp.maximum(cidx(i) * hb - 1, 0), 0)),
        pl.BlockSpec((1, HALO, width), lambda bi, i: (bi, jnp.minimum((cidx(i) + 1) * hb, t // HALO - 1), 0)),
        pl.BlockSpec((1, L, LANE), lambda bi, i: (bi, cidx(i), 0)),
        pl.BlockSpec((8, width), lambda bi, i: (0, 0)),
        pl.BlockSpec((1, width), lambda bi, i: (0, 0)),
        pl.BlockSpec((1, LANE), lambda bi, i: (0, 0)),
        pl.BlockSpec((1, LANE), lambda bi, i: (0, 0)),
        pl.BlockSpec((1, ssd_dim), lambda bi, i: (0, 0)),
        pl.BlockSpec((1, heads, SSD_STATE, SSD_HEAD_DIM), lambda bi, i: (bi, 0, 0, 0)),
    ]
    args = [xbc, xbc, xbc, dt, cw, cb, dtb, a, dsk, s0]
    if final:
        yb, z, ng = final_inputs
        in_specs += [
            pl.BlockSpec((1, L, ssd_dim), lambda bi, i: (bi, cidx(i), 0)),
            pl.BlockSpec((1, L, ssd_dim), lambda bi, i: (bi, cidx(i), 0)),
            pl.BlockSpec((1, ssd_dim), lambda bi, i: (0, 0)),
        ]
        args += [yb, z, ng]
    out_specs = [
        pl.BlockSpec((1, L, ssd_dim), lambda bi, i: (bi, cidx(i), 0)),
        pl.BlockSpec((1, heads, SSD_STATE, SSD_HEAD_DIM), lambda bi, i: (bi, 0, 0, 0)),
    ]
    out_shape = [
        jax.ShapeDtypeStruct((b, t, ssd_dim), F32),
        jax.ShapeDtypeStruct((b, heads, SSD_STATE, SSD_HEAD_DIM), F32),
    ]
    if not final:
        out_specs.append(pl.BlockSpec((1, L, width), lambda bi, i: (bi, cidx(i), 0)))
        out_shape.append(jax.ShapeDtypeStruct((b, t, width), F32))
    return pl.pallas_call(
        functools.partial(_ssd_kernel, chunk=L, rev=rev, final=final, heads=heads),
        grid=(b, nc),
        in_specs=in_specs,
        out_specs=out_specs,
        out_shape=out_shape,
        scratch_shapes=[
            pltpu.VMEM((heads, SSD_STATE, SSD_HEAD_DIM), F32),
            pltpu.VMEM((L + 2 * HALO, width), F32),
        ],
        compiler_params=_params(),
        name="ssd_bwd" if rev else "ssd_fwd",
    )(*args)


def _ssd_stream(z, xbc, dt, conv_w, conv_b, a_log, dt_bias, d_skip, norm_g, s0_f, s0_b):
    heads = s0_f.shape[1]
    cw = jnp.pad(conv_w, ((0, 8 - CONV_W), (0, 0)))
    cb = conv_b[None]
    dtb = jnp.pad(dt_bias.reshape(1, -1), ((0, 0), (0, LANE - 2 * heads)))
    a = jnp.pad(-jnp.exp(a_log).reshape(1, -1), ((0, 0), (0, LANE - 2 * heads)))
    dsk = jnp.repeat(d_skip, SSD_HEAD_DIM)[None]
    yb, s_b, act = _ssd_pass(xbc, dt, cw, cb, dtb, a, dsk, s0_b, rev=True)
    y, s_f = _ssd_pass(act, dt, cw, cb, dtb, a, dsk, s0_f, rev=False, final_inputs=(yb, z, norm_g[None]))
    return y, s_f, s_b


def _group_sum(a, ones_bd):
    hi = a.astype(BF16)
    lo = (a - hi.astype(F32)).astype(BF16)
    return (jnp.dot(hi, ones_bd, preferred_element_type=F32) + jnp.dot(lo, ones_bd, preferred_element_type=F32))


def _mm_bf16(a, b):
    return jnp.dot(a.astype(BF16), b.astype(BF16), preferred_element_type=F32)


def _unit_tri_inverse_many(xms, rev):
    n = xms[0].shape[0]
    row = lax.broadcasted_iota(jnp.int32, (n, n), 0)
    col = lax.broadcasted_iota(jnp.int32, (n, n), 1)
    eye = (row == col).astype(F32)
    ds = None
    m, sh = 1, 0
    while m < n:
        same = (row >> (sh + 1)) == (col >> (sh + 1))
        rbit = (row >> sh) & 1
        cbit = (col >> sh) & 1
        sel = same & ((rbit == 0) & (cbit == 1) if rev else (rbit == 1) & (cbit == 0))
        cs = [jnp.where(sel, x, 0.0) for x in xms]
        if ds is None:
            ds = [eye - c for c in cs]
        elif m < SUBLANES:
            es = [_mm_bf16(c, d) for c, d in zip(cs, ds)]
            ds = [d - _mm_bf16(d, e) for d, e in zip(ds, es)]
        else:
            act = 0 if rev else 1

            def rows(a, which):
                return a.reshape(n // (2 * m), 2, m, n)[:, which].reshape(n // 2, n)

            def merge(keep, new):
                pair = (new, keep) if rev else (keep, new)
                return jnp.stack([p.reshape(n // (2 * m), m, n) for p in pair], axis=1).reshape(n, n)

            es = [_mm_bf16(rows(c, act), d) for c, d in zip(cs, ds)]
            zero = jnp.zeros((n // 2, n), F32)
            ds = [merge(rows(d, 1 - act), rows(d, act) - _mm_bf16(rows(d, act), merge(zero, e)))
                  for d, e in zip(ds, es)]
        m, sh = 2 * m, sh + 1
    return ds


def _gdn_kernel(*refs, block, rev, final, heads):
    if final:
        (qkv_ref, prev_ref, next_ref, ab_ref, cw_ref, dtb_ref, a_ref, s0_ref, ob_ref, gate_ref, ng_ref,
         o_ref, sfin_ref, s_ref, ext_ref) = refs
    else:
        (qkv_ref, prev_ref, next_ref, ab_ref, cw_ref, dtb_ref, a_ref, s0_ref,
         o_ref, sfin_ref, act_ref, s_ref, ext_ref) = refs
    TB = block
    L = block
    hd = GDN_HEAD_DIM
    dim = heads * hd
    i = pl.program_id(1)
    nb = pl.num_programs(1)
    j = nb - 1 - i if rev else i

    @pl.when(i == 0)
    def _():
        s_ref[...] = s0_ref[0]

    ri = lax.broadcasted_iota(jnp.int32, (dim, dim), 0) // hd
    ci_ = lax.broadcasted_iota(jnp.int32, (dim, dim), 1) // hd
    ones_bd = (ri == ci_).astype(BF16)
    if final:
        act = qkv_ref[0]
        q = act[:, :dim]
        k = act[:, dim:2 * dim]
        v = act[:, 2 * dim:]
    else:
        ext_ref[0:HALO, :] = jnp.where(j > 0, prev_ref[0], 0.0)
        ext_ref[HALO:HALO + TB, :] = qkv_ref[0]
        ext_ref[HALO + TB:, :] = jnp.where(j < nb - 1, next_ref[0], 0.0)
        base = HALO - CONV_W // 2
        acc = cw_ref[0:1, :] * ext_ref[base:base + TB, :]
        for kk in range(1, CONV_W):
            acc = acc + cw_ref[kk:kk + 1, :] * ext_ref[base + kk:base + kk + TB, :]
        act = acc * jax.nn.sigmoid(acc)
        q = act[:, :dim]
        k = act[:, dim:2 * dim]
        v = act[:, 2 * dim:]
        q = q * lax.rsqrt(_group_sum(q * q, ones_bd) + EPS) * (hd ** -0.5)
        k = k * lax.rsqrt(_group_sum(k * k, ones_bd) + EPS)
        act_ref[0] = jnp.concatenate([q, k, v], axis=-1)
    ab = ab_ref[0]
    gl = a_ref[...] * jax.nn.softplus(ab + dtb_ref[...])
    beta = jax.nn.sigmoid(ab)

    row = lax.broadcasted_iota(jnp.int32, (L, L), 0)
    col = lax.broadcasted_iota(jnp.int32, (L, L), 1)
    incl = (row <= col) if rev else (row >= col)
    strict = (row < col) if rev else (row > col)
    last = 0 if rev else L - 1
    cs_col, cs_row = _cumsum_both(gl, incl.astype(BF16))

    hs = range(heads)
    cis = [(heads if rev else 0) + h for h in hs]
    qh = [q[:, h * hd:(h + 1) * hd] for h in hs]
    kh = [k[:, h * hd:(h + 1) * hd] for h in hs]
    vh = [v[:, h * hd:(h + 1) * hd] for h in hs]
    csc = [cs_col[:, ci:ci + 1] for ci in cis]
    bcol = [beta[:, 2 * heads + ci:2 * heads + ci + 1] for ci in cis]
    qk_kk = [lax.dot_general(jnp.concatenate([qh[h], kh[h]], axis=0).astype(BF16), kh[h].astype(BF16),
                             (((1,), (1,)), ((), ())), preferred_element_type=F32) for h in hs]
    dec = [jnp.exp(jnp.where(incl, csc[h] - cs_row[cis[h]:cis[h] + 1, :], NEG)) for h in hs]
    attn = [(qk_kk[h][:L] * dec[h]).astype(BF16) for h in hs]
    xm = [jnp.where(strict, bcol[h] * qk_kk[h][L:] * dec[h], 0.0) for h in hs]
    tm = _unit_tri_inverse_many(xm, rev)
    egc = [jnp.exp(cc) for cc in csc]
    uw = [jnp.dot(tm[h].astype(BF16),
                  jnp.concatenate([vh[h] * bcol[h], kh[h] * (bcol[h] * egc[h])], axis=1).astype(BF16),
                  preferred_element_type=F32) for h in hs]
    s_old = [s_ref[h] for h in hs]
    rs = [jnp.dot(jnp.concatenate([uw[h][:, hd:], qh[h] * egc[h]], axis=0).astype(BF16), s_old[h].astype(BF16),
                  preferred_element_type=F32) for h in hs]
    vnb = [(uw[h][:, :hd] - rs[h][:L]).astype(BF16) for h in hs]
    outs = [rs[h][L:] + jnp.dot(attn[h], vnb[h], preferred_element_type=F32) for h in hs]
    for h in hs:
        tot = csc[h][last:last + 1, :]
        kend = (kh[h] * jnp.exp(tot - csc[h])).astype(BF16)
        s_ref[h] = s_old[h] * jnp.exp(tot) + lax.dot_general(kend, vnb[h], (((0,), (0,)), ((), ())),
                                                            preferred_element_type=F32)
    o = jnp.concatenate(outs, axis=-1)
    if final:
        o = o + ob_ref[0]
        ms = _group_sum(o * o, ones_bd) * (1.0 / hd)
        gate = gate_ref[0]
        o = o * lax.rsqrt(ms + EPS) * ng_ref[...] * (gate * jax.nn.sigmoid(gate))
    o_ref[0] = o

    @pl.when(i == nb - 1)
    def _():
        sfin_ref[0] = s_ref[...]


def _gdn_pass(qkv, ab, cw, dtb, a, s0, *, rev, block, final_inputs=None, raster_out=False):
    b, t, width = qkv.shape
    heads = s0.shape[1]
    dim = heads * GDN_HEAD_DIM
    TB = block
    nb = t // TB
    hb = TB // HALO
    final = final_inputs is not None

    def bidx(i):
        return nb - 1 - i if rev else i

    in_specs = [
        pl.BlockSpec((1, TB, width), lambda bi, i: (bi, bidx(i), 0)),
        pl.BlockSpec((1, HALO, width), lambda bi, i: (bi, jnp.maximum(bidx(i) * hb - 1, 0), 0)),
        pl.BlockSpec((1, HALO, width), lambda bi, i: (bi, jnp.minimum((bidx(i) + 1) * hb, t // HALO - 1), 0)),
        pl.BlockSpec((1, TB, LANE), lambda bi, i: (bi, bidx(i), 0)),
        pl.BlockSpec((8, width), lambda bi, i: (0, 0)),
        pl.BlockSpec((1, LANE), lambda bi, i: (0, 0)),
        pl.BlockSpec((1, LANE), lambda bi, i: (0, 0)),
        pl.BlockSpec((1, heads, GDN_HEAD_DIM, GDN_HEAD_DIM), lambda bi, i: (bi, 0, 0, 0)),
    ]
    args = [qkv, qkv, qkv, ab, cw, dtb, a, s0]
    if final:
        ob, gate, ng = final_inputs
        in_specs += [
            pl.BlockSpec((1, TB, dim), lambda bi, i: (bi, bidx(i), 0)),
            pl.BlockSpec((1, TB, dim), lambda bi, i: (bi, bidx(i), 0)),
            pl.BlockSpec((1, dim), lambda bi, i: (0, 0)),
        ]
        args += [ob, gate, ng]
    if raster_out:
        assert nb == GRID_W
        o_spec = pl.BlockSpec((1, TB, dim), lambda bi, i: (bi, 0, bidx(i)))
        o_shape = jax.ShapeDtypeStruct((b, TB, GRID_W * dim), F32)
    else:
        o_spec = pl.BlockSpec((1, TB, dim), lambda bi, i: (bi, bidx(i), 0))
        o_shape = jax.ShapeDtypeStruct((b, t, dim), F32)
    out_specs = [o_spec, pl.BlockSpec((1, heads, GDN_HEAD_DIM, GDN_HEAD_DIM), lambda bi, i: (bi, 0, 0, 0))]
    out_shape = [o_shape, jax.ShapeDtypeStruct((b, heads, GDN_HEAD_DIM, GDN_HEAD_DIM), F32)]
    if not final:
        out_specs.append(pl.BlockSpec((1, TB, width), lambda bi, i: (bi, bidx(i), 0)))
        out_shape.append(jax.ShapeDtypeStruct((b, t, width), F32))
    outs = pl.pallas_call(
        functools.partial(_gdn_kernel, block=TB, rev=rev, final=final, heads=heads),
        grid=(b, nb),
        in_specs=in_specs,
        out_specs=out_specs,
        out_shape=out_shape,
        scratch_shapes=[
            pltpu.VMEM((heads, GDN_HEAD_DIM, GDN_HEAD_DIM), F32),
            pltpu.VMEM((TB + 2 * HALO, width), F32),
        ],
        compiler_params=_params(),
        name="gdn_bwd" if rev else "gdn_fwd",
    )(*args)
    return (outs[0].reshape(b, t, dim),) + tuple(outs[1:])


def _gdn_stream(qkv, gate, ab, conv_w, a_log, dt_bias, norm_g, s0_f, s0_b, *, column_major):
    t = qkv.shape[1]
    heads = s0_f.shape[1]
    cw = jnp.pad(conv_w, ((0, 8 - CONV_W), (0, 0)))
    dtb = jnp.pad(dt_bias.reshape(1, -1), ((0, 0), (0, LANE - 2 * heads)))
    a = jnp.pad(-jnp.exp(a_log).reshape(1, -1), ((0, 0), (0, LANE - 2 * heads)))
    ng = jnp.tile(norm_g, heads)[None]
    rows = t // GRID_W
    fused_raster = column_major and rows <= 256 and rows >= 2 * HALO and rows & (rows - 1) == 0
    block = rows if fused_raster else min(t, 256)
    ob, s_b, act = _gdn_pass(qkv, ab, cw, dtb, a, s0_b, rev=True, block=block)
    o, s_f = _gdn_pass(act, ab, cw, dtb, a, s0_f, rev=False, block=block, final_inputs=(ob, gate, ng),
                       raster_out=fused_raster)
    if column_major and not fused_raster:
        o = _to_raster(o)
    return o, s_f, s_b


def _router_kernel(x_ref, sc_ref, sh_ref, g_ref, rw_ref, h_ref, a_ref):
    x = x_ref[0]
    ms = jnp.mean(x * x, axis=-1, keepdims=True)
    h = x * lax.rsqrt(ms + EPS) * g_ref[...]
    h = h * (1.0 + sc_ref[0]) + sh_ref[0]
    hh, hm, _ = _split3(h)
    h_ref[0] = hh
    rw = rw_ref[...]
    rh = rw.astype(BF16)
    rm = (rw - rh.astype(F32)).astype(BF16)
    nt = (((1,), (1,)), ((), ()))
    lg = (lax.dot_general(rh, hh, nt, preferred_element_type=F32)
          + lax.dot_general(rh, hm, nt, preferred_element_type=F32)
          + lax.dot_general(rm, hh, nt, preferred_element_type=F32))
    ex = jnp.exp(lg - jnp.max(lg, axis=0, keepdims=True))
    aff = ex / jnp.sum(ex, axis=0, keepdims=True)
    for k in range(a_ref.shape[1]):
        a_ref[0, k] = aff[:, k * LANE:(k + 1) * LANE]


def _router(x, scale, shift, g, rw_t):
    b, t, d = x.shape
    e = rw_t.shape[0]
    tm = min(t, 1024)
    return pl.pallas_call(
        _router_kernel,
        grid=(b, t // tm),
        in_specs=[
            pl.BlockSpec((1, tm, d), lambda i, j: (i, j, 0)),
            pl.BlockSpec((1, 1, d), lambda i, j: (i, 0, 0)),
            pl.BlockSpec((1, 1, d), lambda i, j: (i, 0, 0)),
            pl.BlockSpec((1, d), lambda i, j: (0, 0)),
            pl.BlockSpec((e, d), lambda i, j: (0, 0)),
        ],
        out_specs=[
            pl.BlockSpec((1, tm, d), lambda i, j: (i, j, 0)),
            pl.BlockSpec((1, tm // LANE, e, LANE), lambda i, j: (i, j, 0, 0)),
        ],
        out_shape=[
            jax.ShapeDtypeStruct((b, t, d), BF16),
            jax.ShapeDtypeStruct((b, t // LANE, e, LANE), F32),
        ],
        compiler_params=_params(),
        name="router",
    )(x, scale, shift, g, rw_t)


def _token_prefix(m3, ut, ones, lt):
    e, nb, _ = m3.shape
    m2 = m3.reshape(e * nb, LANE).astype(BF16)
    inb = jnp.dot(m2, ut, preferred_element_type=F32).reshape(e, nb, LANE)
    tot = jnp.dot(m2, ones, preferred_element_type=F32).reshape(e, nb, LANE)
    offs = jnp.stack([jnp.dot(lt, tot[i].astype(BF16), preferred_element_type=F32) for i in range(e)], axis=0)
    return inb, tot, offs


def _select_kernel(a_ref, idx_ref, gate_ref, srow_ref, st8_ref, npc_ref, cs_ref, cum_ref, cnt_ref, *, cap):
    a = a_ref[0]
    e_n, nb, _ = a.shape
    bits = lax.bitcast_convert_type(a, I32)

    def radix(i, prefix):
        cand = prefix | jnp.left_shift(jnp.int32(1), 30 - i)
        cnt = jnp.sum(jnp.sum((bits >= cand).astype(F32), axis=2, keepdims=True), axis=1, keepdims=True)
        return jnp.where(cnt >= cap, cand, prefix)

    thr = lax.fori_loop(0, 31, radix, jnp.zeros((e_n, 1, 1), I32))
    li = lax.broadcasted_iota(I32, (LANE, LANE), 0)
    lj = lax.broadcasted_iota(I32, (LANE, LANE), 1)
    ut = (li < lj).astype(BF16)
    ones = jnp.ones((LANE, LANE), BF16)
    bi = lax.broadcasted_iota(I32, (nb, nb), 0)
    bj = lax.broadcasted_iota(I32, (nb, nb), 1)
    lt = (bj < bi).astype(BF16)

    gt = bits > thr
    eq = bits == thr
    n_gt = jnp.sum(jnp.sum(gt.astype(F32), axis=2, keepdims=True), axis=1, keepdims=True)
    tie_in, _, tie_offs = _token_prefix(eq.astype(F32), ut, ones, lt)
    sel = gt | (eq & (tie_in + tie_offs < cap - n_gt))
    pos_in, cnt, offs = _token_prefix(sel.astype(F32), ut, ones, lt)
    cs_ref[...] = jnp.where(sel, pos_in + 1.0, 0.0)
    cum_ref[...] = offs + cnt
    cnt_ref[...] = cnt

    offs_i = offs.astype(I32)
    cnt_i = cnt.astype(I32)
    st8 = (offs_i >> 3) << 3
    npc = jnp.where(cnt_i > 0, (offs_i + cnt_i - st8 + (PIECE - 1)) >> 3, 0)
    rbase = []
    run = jnp.zeros((nb, LANE), I32)
    for i in range(e_n):
        rbase.append(run)
        run = run + PIECE * npc[i]
    rbase = jnp.stack(rbase, axis=0)
    srow_ref[0] = jnp.where(sel, rbase + pos_in.astype(I32) + offs_i - st8, -1)
    st8_ref[0] = st8
    npc_ref[0] = npc

    jrow = lax.broadcasted_iota(I32, (1, cap), 1).astype(F32)
    sub_nb = lax.broadcasted_iota(I32, (nb, cap), 0).astype(F32)
    sub_l = lax.broadcasted_iota(I32, (LANE, cap), 0).astype(F32)
    tn = (((0,), (0,)), ((), ()))

    def compact(ei, carry):
        cum_col = cum_ref[ei][:, 0:1]
        cnt_col = cnt_ref[ei][:, 0:1]
        ge = cum_col <= jrow
        blk_j = jnp.sum(ge.astype(F32), axis=0, keepdims=True)
        offs_j = jnp.sum(jnp.where(ge, cnt_col, 0.0), axis=0, keepdims=True)
        rank1 = jrow - offs_j + 1.0
        g_t = (sub_nb == blk_j).astype(BF16)
        row_t = lax.dot_general(cs_ref[ei].astype(BF16), g_t, tn, preferred_element_type=F32)
        match = row_t == rank1
        lane_j = jnp.sum(jnp.where(match, sub_l, 0.0), axis=0, keepdims=True)
        idx_ref[0, pl.ds(ei, 1), :] = (blk_j * LANE + lane_j).astype(I32)
        parts = _split3(a_ref[0, ei])
        aff_t = sum(lax.dot_general(p, g_t, tn, preferred_element_type=F32) for p in parts)
        gate_ref[0, pl.ds(ei, 1), :] = jnp.sum(jnp.where(match, aff_t, 0.0), axis=0, keepdims=True)
        return carry

    lax.fori_loop(0, e_n, compact, 0)


def _ec_select(aff_em, cap):
    b, e, nb, _ = aff_em.shape
    big = lambda dt: jax.ShapeDtypeStruct((b, e, nb, LANE), dt)
    spec4 = pl.BlockSpec((1, e, nb, LANE), lambda i: (i, 0, 0, 0))
    spec3 = pl.BlockSpec((1, e, cap), lambda i: (i, 0, 0))
    return pl.pallas_call(
        functools.partial(_select_kernel, cap=cap),
        grid=(b,),
        in_specs=[spec4],
        out_specs=[spec3, spec3, spec4, spec4, spec4],
        out_shape=[jax.ShapeDtypeStruct((b, e, cap), I32), jax.ShapeDtypeStruct((b, e, cap), F32),
                   big(I32), big(I32), big(I32)],
        scratch_shapes=[pltpu.VMEM((e, nb, LANE), F32)] * 3,
        compiler_params=pltpu.CompilerParams(dimension_semantics=("arbitrary",), vmem_limit_bytes=VMEM_LIMIT),
        name="ec_select",
    )(aff_em)


def _combine_kernel(st8_sm, npc_sm, ye_hbm, srow_ref, x_ref, g_ref, fn_ref, o_ref, stage, acc_ref, sem,
                    *, final, n_exp):
    b = pl.program_id(0)
    k = pl.program_id(1)
    nb = pl.num_programs(1)
    step = b * nb + k
    nsteps = pl.num_programs(0) * nb
    slot = step % 2

    def piece_copy(bb, e, src_row, sl, dst_row):
        return pltpu.make_async_copy(ye_hbm.at[bb, e, pl.ds(src_row, PIECE), :],
                                     stage.at[sl, pl.ds(dst_row, PIECE), :], sem.at[sl])

    def issue(st, sl):
        bb = st // nb

        def per_e(e, r):
            s8 = st8_sm[st * n_exp + e]
            n = npc_sm[st * n_exp + e]

            def per_p(p, r2):
                piece_copy(bb, e, pl.multiple_of(s8 + PIECE * p, PIECE), sl, pl.multiple_of(r2, PIECE)).start()
                return r2 + PIECE

            return lax.fori_loop(0, n, per_p, r)

        lax.fori_loop(0, n_exp, per_e, 0)

    @pl.when(step == 0)
    def _():
        issue(step, slot)

    @pl.when(step + 1 < nsteps)
    def _():
        issue(step + 1, 1 - slot)

    npieces = lax.fori_loop(0, n_exp, lambda e, s: s + npc_sm[step * n_exp + e], 0)
    rows = npieces * PIECE

    def wait_one(p, c):
        piece_copy(0, 0, 0, slot, 0).wait()
        return c

    lax.fori_loop(0, npieces, wait_one, 0)
    stage[slot, pl.ds(pl.multiple_of(rows, PIECE), LANE), :] = jnp.zeros((LANE, stage.shape[-1]), F32)

    srow = srow_ref[0, 0]
    acc_ref[...] = jnp.zeros_like(acc_ref)
    riota = lax.broadcasted_iota(I32, (LANE, LANE), 0)
    tn = (((0,), (0,)), ((), ()))

    def chunk(c, carry):
        r0 = pl.multiple_of(c * LANE, LANE)
        rid = riota + r0
        pt = (srow[0:1, :] == rid).astype(F32)
        for e in range(1, n_exp):
            pt = pt + (srow[e:e + 1, :] == rid).astype(F32)
        ptb = pt.astype(BF16)
        st = stage[slot, pl.ds(r0, LANE), :]
        hi = st.astype(BF16)
        lo = (st - hi.astype(F32)).astype(BF16)
        acc_ref[...] += lax.dot_general(jnp.concatenate([ptb, ptb], axis=0), jnp.concatenate([hi, lo], axis=0), tn,
                                        preferred_element_type=F32)
        return carry

    lax.fori_loop(0, (rows + LANE - 1) // LANE, chunk, 0)
    y = x_ref[0] + g_ref[0] * acc_ref[...]
    if final:
        y = y * lax.rsqrt(jnp.mean(y * y, axis=-1, keepdims=True) + EPS) * fn_ref[...]
    o_ref[0] = y


def _ec_combine(ye, srow_bm, st8, npc, x, gate, final_g=None):
    b, t, d = x.shape
    e = ye.shape[1]
    nb = t // LANE
    final = final_g is not None
    fn = final_g if final else jnp.ones((1, d), F32)
    max_rows = e * (LANE + 2 * PIECE) + LANE
    grid_spec = pltpu.PrefetchScalarGridSpec(
        num_scalar_prefetch=2,
        grid=(b, nb),
        in_specs=[
            pl.BlockSpec(memory_space=pl.ANY),
            pl.BlockSpec((1, 1, e, LANE), lambda i, j, *_: (i, j, 0, 0)),
            pl.BlockSpec((1, LANE, d), lambda i, j, *_: (i, j, 0)),
            pl.BlockSpec((1, 1, d), lambda i, j, *_: (i, 0, 0)),
            pl.BlockSpec((1, d), lambda i, j, *_: (0, 0)),
        ],
        out_specs=pl.BlockSpec((1, LANE, d), lambda i, j, *_: (i, j, 0)),
        scratch_shapes=[
            pltpu.VMEM((2, max_rows, d), F32),
            pltpu.VMEM((LANE, d), F32),
            pltpu.SemaphoreType.DMA((2,)),
        ],
    )
    return pl.pallas_call(
        functools.partial(_combine_kernel, final=final, n_exp=e),
        grid_spec=grid_spec,
        out_shape=jax.ShapeDtypeStruct((b, t, d), F32),
        compiler_params=_params(),
        name="ec_combine",
    )(st8, npc, ye, srow_bm, x, gate, fn)


def _expert_choice_block(x, scale, shift, g2, gate2, router_w, wg, wu, wd, final_g):
    b, t, d = x.shape
    e = router_w.shape[-1]
    cap = EC_CAPACITY * t // e
    h2, aff_bm = _router(x, scale, shift, g2, router_w.T)
    idx, gate, srow, st8, npc = _ec_select(jnp.transpose(aff_bm, (0, 2, 1, 3)), cap)
    xe = jax.vmap(lambda hb, ib: hb[ib])(h2, idx)
    ye = _expert_ffn(xe.reshape(b * e, cap, d), gate.reshape(b * e, cap, 1), wg, wu, wd).reshape(b, e, cap, d)
    srow_bm = jnp.transpose(srow, (0, 2, 1, 3))
    st8_f = jnp.transpose(st8[..., 0], (0, 2, 1)).reshape(-1)
    npc_f = jnp.transpose(npc[..., 0], (0, 2, 1)).reshape(-1)
    return _ec_combine(ye, srow_bm, st8_f, npc_f, x, gate2, final_g)


def _rms_norm(x, g):
    return x * lax.rsqrt(jnp.mean(x * x, axis=-1, keepdims=True) + EPS) * g


def _to_raster(u):
    b, t, c = u.shape
    rows = t // GRID_W
    return u.reshape(b, GRID_W, rows, c).transpose(0, 2, 1, 3).reshape(b, t, c)


def _expert_choice_ffn(h, router_w, wg, wu, wd):
    b, t, d = h.shape
    cap = EC_CAPACITY * t // N_EXPERTS
    aff = jax.nn.softmax(jnp.einsum('btd,de->bte', h, router_w, precision=lax.Precision.HIGHEST), axis=-1)
    gate, idx = lax.top_k(jnp.swapaxes(aff, 1, 2), cap)
    xe = jax.vmap(lambda hb, ib: hb[ib])(h, idx)
    ye = _expert_ffn(xe.reshape(b * N_EXPERTS, cap, d), gate.reshape(b * N_EXPERTS, cap, 1), wg, wu, wd)
    ye = ye.reshape(b, N_EXPERTS, cap, d)
    return jax.vmap(lambda ib, yb: jnp.zeros((t, d), yb.dtype).at[ib.reshape(-1)].add(yb.reshape(-1, d)))(idx, ye)


def kernel(x, c, ctx, c_ctx, norm1_g, norm2_g, ada_w, ada_b, w_in, w_out, pool_w, pool_scale, ssd_conv_w, ssd_conv_b, ssd_a_log, ssd_dt_bias, ssd_d, ssd_norm_g, gdn_conv_w, gdn_a_log, gdn_dt_bias, gdn_norm_g, router_w, exp_w_gate, exp_w_up, exp_w_down, final_norm_g):
    depth, d, _ = w_in.shape
    b, t, _ = x.shape
    pool_dim = pool_scale.shape[-1]
    ssd_dim = ssd_norm_g.shape[-1]
    ssd_heads = ssd_dim // SSD_HEAD_DIM
    ssd_bc = SSD_GROUPS * SSD_STATE
    gdn_dim = gdn_conv_w.shape[-1] // 3
    gdn_heads = gdn_dim // GDN_HEAD_DIM
    splits = (pool_dim, ssd_dim, ssd_dim + 2 * ssd_bc, 2 * ssd_heads, 3 * gdn_dim, gdn_dim, 2 * gdn_heads,
              2 * gdn_heads)
    cut = [0] + np.cumsum(splits).tolist()
    r_widths = (ssd_dim, ssd_dim + 2 * ssd_bc, pool_dim, LANE)
    g_widths = (3 * gdn_dim, gdn_dim, LANE)

    sc = jax.nn.silu(c)
    scc = jax.nn.silu(c_ctx)[None]
    for l in range(depth):
        last = l == depth - 1
        wl = w_in[l]
        seg = [wl[:, cut[i]:cut[i + 1]] for i in range(8)]
        w_r = jnp.concatenate(
            [seg[1], seg[2], seg[0], jnp.pad(seg[3], ((0, 0), (0, LANE - 2 * ssd_heads)))], axis=1).astype(BF16)
        w_g = jnp.concatenate(
            [seg[4], seg[5], jnp.pad(jnp.concatenate([seg[6], seg[7]], axis=1), ((0, 0), (0, LANE - 4 * gdn_heads)))],
            axis=1).astype(BF16)
        w_o = w_out[l].astype(BF16)
        wg, wu, wd = exp_w_gate[l], exp_w_up[l], exp_w_down[l]
        m_lat = jnp.split(sc @ ada_w[l] + ada_b[l], 6, axis=-1)
        m_ctx = [jnp.broadcast_to(m, (b, d)) for m in jnp.split(scc @ ada_w[l] + ada_b[l], 6, axis=-1)]
        g1 = norm1_g[l][None]

        def project(xx, mm, column_major):
            pr = _inproj(xx, mm[1][:, None], mm[0][:, None], g1, w_r, r_widths, column_major=False)
            pg = _inproj(xx, mm[1][:, None], mm[0][:, None], g1, w_g, g_widths, column_major=column_major)
            return pr, pg

        ssd_p = (ssd_conv_w[l], ssd_conv_b[l], ssd_a_log[l], ssd_dt_bias[l], ssd_d[l], ssd_norm_g[l])
        gdn_p = (gdn_conv_w[l], gdn_a_log[l], gdn_dt_bias[l], gdn_norm_g[l])
        zs = jnp.zeros((b, ssd_heads, SSD_STATE, SSD_HEAD_DIM), F32)
        zg = jnp.zeros((b, gdn_heads, GDN_HEAD_DIM, GDN_HEAD_DIM), F32)

        (c_z, c_xbc, c_pool, c_dt), (c_qkv, c_gate, c_ab) = project(ctx, m_ctx, False)
        (l_z, l_xbc, l_pool, l_dt), (l_qkv, l_gate, l_ab) = project(x, m_lat, True)

        s_ctx, ssd_sf, ssd_sb = _ssd_stream(c_z, c_xbc, c_dt, *ssd_p, zs, zs)
        g_ctx, gdn_sf, gdn_sb = _gdn_stream(c_qkv, c_gate, c_ab, *gdn_p, zg, zg, column_major=False)
        s_lat, _, _ = _ssd_stream(l_z, l_xbc, l_dt, *ssd_p, ssd_sf, ssd_sb)
        g_lat, _, _ = _gdn_stream(l_qkv, l_gate, l_ab, *gdn_p, gdn_sf, gdn_sb, column_major=True)
        w_o_parts = [w_o[:pool_dim], w_o[pool_dim:pool_dim + ssd_dim], w_o[pool_dim + ssd_dim:]]
        x = _outproj([_pool_branch(l_pool, pool_w[l], pool_scale[l]), s_lat, g_lat], x, m_lat[2][:, None],
                     w_o_parts)
        fin = final_norm_g[None] if last else None
        if t % 1024 == 0:
            x = _expert_choice_block(x, m_lat[4][:, None], m_lat[3][:, None], norm2_g[l][None], m_lat[5][:, None],
                                     router_w[l], wg, wu, wd, fin)
        else:
            h2 = _rms_norm(x, norm2_g[l]) * (1 + m_lat[4][:, None]) + m_lat[3][:, None]
            x = x + m_lat[5][:, None] * _expert_choice_ffn(h2, router_w[l], wg, wu, wd)
            if last:
                x = _rms_norm(x, final_norm_g)
        if not last:
            ctx = _outproj([_pool_branch(c_pool, pool_w[l], pool_scale[l]), s_ctx, g_ctx], ctx, m_ctx[2][:, None],
                           w_o_parts)
            h2c = _rms_norm(ctx, norm2_g[l]) * (1 + m_ctx[4][:, None]) + m_ctx[3][:, None]
            ctx = ctx + m_ctx[5][:, None] * _expert_choice_ffn(h2c, router_w[l], wg, wu, wd)
    return x
```

```python
import functools

import numpy as np
import jax
import jax.numpy as jnp
from jax import lax
from jax.experimental import pallas as pl
from jax.experimental.pallas import tpu as pltpu

F32 = jnp.float32
BF16 = jnp.bfloat16

GRID_W = 64
CHUNK = 64
CONV_W = 5
POOL_WINDOWS = (2, 4, 8, 16)
SSD_HEAD_DIM = 64
SSD_GROUPS = 2
SSD_STATE = 128
GDN_HEAD_DIM = 64
N_EXPERTS = 16
EC_CAPACITY = 2
EPS = 1e-6
LANE = 128
SUBLANES = 8
HALO = 8
NEG = -1e30
PIECE = 8
I32 = jnp.int32
VMEM_LIMIT = 48 * 1024 * 1024


def _round_up(n, m):
    return (n + m - 1) // m * m


def _params():
    return pltpu.CompilerParams(dimension_semantics=("arbitrary", "arbitrary"), vmem_limit_bytes=VMEM_LIMIT)


def _inproj_kernel(x_ref, sc_ref, sh_ref, g_ref, w_ref, *o_refs, transpose_grid):
    x = x_ref[0]
    tm = x.shape[0]
    ms = jnp.mean(x * x, axis=-1, keepdims=True)
    h = x * lax.rsqrt(ms + EPS) * g_ref[...]
    h = (h * (1.0 + sc_ref[0]) + sh_ref[0]).astype(BF16)
    if transpose_grid:
        p = lax.broadcasted_iota(jnp.int32, (tm, tm), 0)
        q = lax.broadcasted_iota(jnp.int32, (tm, tm), 1)
        perm = (q == (p % SUBLANES) * GRID_W + p // SUBLANES).astype(BF16)
        h = jnp.dot(perm, h, preferred_element_type=F32).astype(BF16)
    y = jnp.dot(h, w_ref[...], preferred_element_type=F32)
    off = 0
    for o_ref in o_refs:
        n = o_ref.shape[-1]
        if transpose_grid:
            o_ref[0] = y[:, off:off + n].reshape(GRID_W, SUBLANES, n)
        else:
            o_ref[0] = y[:, off:off + n]
        off += n


def _inproj(x, scale, shift, g, w, widths, *, column_major):
    b, t, d = x.shape
    rows = t // GRID_W
    if column_major:
        assert rows % SUBLANES == 0
        tm = SUBLANES * GRID_W
        out_specs = [pl.BlockSpec((1, GRID_W, SUBLANES, n), lambda i, j: (i, 0, j, 0)) for n in widths]
        out_shape = [jax.ShapeDtypeStruct((b, GRID_W, rows, n), F32) for n in widths]
    else:
        tm = min(t, 512)
        out_specs = [pl.BlockSpec((1, tm, n), lambda i, j: (i, j, 0)) for n in widths]
        out_shape = [jax.ShapeDtypeStruct((b, t, n), F32) for n in widths]
    outs = pl.pallas_call(
        functools.partial(_inproj_kernel, transpose_grid=column_major),
        grid=(b, t // tm),
        in_specs=[
            pl.BlockSpec((1, tm, d), lambda i, j: (i, j, 0)),
            pl.BlockSpec((1, 1, d), lambda i, j: (i, 0, 0)),
            pl.BlockSpec((1, 1, d), lambda i, j: (i, 0, 0)),
            pl.BlockSpec((1, d), lambda i, j: (0, 0)),
            pl.BlockSpec((d, w.shape[1]), lambda i, j: (0, 0)),
        ],
        out_specs=out_specs,
        out_shape=out_shape,
        compiler_params=_params(),
        name="inproj",
    )(x, scale, shift, g, w)
    return [o.reshape(b, t, n) for o, n in zip(outs, widths)]


def _outproj_kernel(*refs, n_in):
    a_refs = refs[:n_in]
    x_ref, gate_ref = refs[n_in:n_in + 2]
    w_refs = refs[n_in + 2:2 * n_in + 2]
    o_ref = refs[2 * n_in + 2]
    y = jnp.dot(a_refs[0][0].astype(BF16), w_refs[0][...], preferred_element_type=F32)
    for a_ref, w_ref in zip(a_refs[1:], w_refs[1:]):
        y = y + jnp.dot(a_ref[0].astype(BF16), w_ref[...], preferred_element_type=F32)
    o_ref[0] = x_ref[0] + gate_ref[0] * y


def _outproj(parts, x, gate, ws):
    b, t, d = x.shape
    tm = min(t, 512)
    n_in = len(parts)
    return pl.pallas_call(
        functools.partial(_outproj_kernel, n_in=n_in),
        grid=(b, t // tm),
        in_specs=(
            [pl.BlockSpec((1, tm, p.shape[-1]), lambda i, j: (i, j, 0)) for p in parts]
            + [pl.BlockSpec((1, tm, d), lambda i, j: (i, j, 0)), pl.BlockSpec((1, 1, d), lambda i, j: (i, 0, 0))]
            + [pl.BlockSpec(w.shape, lambda i, j: (0, 0)) for w in ws]),
        out_specs=pl.BlockSpec((1, tm, d), lambda i, j: (i, j, 0)),
        out_shape=jax.ShapeDtypeStruct((b, t, d), F32),
        compiler_params=_params(),
        name="outproj",
    )(*parts, x, gate, *ws)


def _pool_kernel(u_ref, prev_ref, next_ref, w_ref, sc_ref, o_ref, e0_ref, e1_ref, *, block, seq, group):
    TM = block
    j = pl.program_id(1)
    nb = pl.num_programs(1)
    u = u_ref[0]
    n_ext = TM + 2 * HALO
    e0_ref[0:HALO, :] = jnp.where(j > 0, prev_ref[0], 0.0)
    e0_ref[HALO:HALO + TM, :] = u
    e0_ref[HALO + TM:, :] = jnp.where(j < nb - 1, next_ref[0], 0.0)
    tok = j * TM + lax.broadcasted_iota(jnp.int32, (TM, 1), 0)
    lane = lax.broadcasted_iota(jnp.int32, (1, u.shape[-1]), 1)
    src, dst = e0_ref, e1_ref
    pooled = jnp.zeros_like(u)
    half = 1
    for gi, win in enumerate(POOL_WINDOWS):
        assert win == 2 * half
        lo_r, hi_r = half, n_ext - half
        if half == 1:
            dst[lo_r:hi_r, :] = src[lo_r - 1:hi_r - 1, :] + src[lo_r:hi_r, :]
        else:
            q = half // 2
            dst[lo_r:hi_r, :] = src[lo_r - q:hi_r - q, :] + src[lo_r + q:hi_r + q, :]
        cnt = (jnp.minimum(tok + half, seq) - jnp.maximum(tok - half, 0)).astype(F32)
        mean = dst[HALO:HALO + TM, :] / cnt
        pooled = jnp.where((lane >= gi * group) & (lane < (gi + 1) * group), mean, pooled)
        src, dst = dst, src
        half *= 2
    dd = pooled - u
    y = jnp.dot(dd.astype(BF16), w_ref[...], preferred_element_type=F32)
    o_ref[0] = y * sc_ref[...]


def _pool_branch(u, pool_w, pool_scale):
    b, t, c = u.shape
    ng, pg, _ = pool_w.shape
    tm = min(t, 512)
    hb = tm // HALO
    w_bd = jnp.zeros((c, c), F32)
    for gi in range(ng):
        w_bd = w_bd.at[gi * pg:(gi + 1) * pg, gi * pg:(gi + 1) * pg].set(pool_w[gi])
    return pl.pallas_call(
        functools.partial(_pool_kernel, block=tm, seq=t, group=pg),
        grid=(b, t // tm),
        in_specs=[
            pl.BlockSpec((1, tm, c), lambda i, j: (i, j, 0)),
            pl.BlockSpec((1, HALO, c), lambda i, j: (i, jnp.maximum(j * hb - 1, 0), 0)),
            pl.BlockSpec((1, HALO, c), lambda i, j: (i, jnp.minimum((j + 1) * hb, t // HALO - 1), 0)),
            pl.BlockSpec((c, c), lambda i, j: (0, 0)),
            pl.BlockSpec((1, c), lambda i, j: (0, 0)),
        ],
        out_specs=pl.BlockSpec((1, tm, c), lambda i, j: (i, j, 0)),
        out_shape=jax.ShapeDtypeStruct((b, t, c), F32),
        scratch_shapes=[pltpu.VMEM((tm + 2 * HALO, c), F32), pltpu.VMEM((tm + 2 * HALO, c), F32)],
        compiler_params=_params(),
        name="pool",
    )(u, u, u, w_bd.astype(BF16), pool_scale[None])


def _ffn_kernel(x_ref, gate_ref, wg_ref, wu_ref, wd_ref, o_ref, wgb_ref, wub_ref, wdb_ref):
    @pl.when(pl.program_id(1) == 0)
    def _():
        wgb_ref[...] = wg_ref[0].astype(BF16)
        wub_ref[...] = wu_ref[0].astype(BF16)
        wdb_ref[...] = wd_ref[0].astype(BF16)

    x = x_ref[0].astype(BF16)
    hg = jnp.dot(x, wgb_ref[...], preferred_element_type=F32)
    hu = jnp.dot(x, wub_ref[...], preferred_element_type=F32)
    hid = (hg * jax.nn.sigmoid(hg)) * hu
    y = jnp.dot(hid.astype(BF16), wdb_ref[...], preferred_element_type=F32)
    o_ref[0] = y * gate_ref[0]


def _expert_ffn(xe, gate, wg, wu, wd, layer):
    be, c, d = xe.shape
    _, e, _, f = wg.shape
    tc = min(c, 1024)
    return pl.pallas_call(
        _ffn_kernel,
        grid=(be, c // tc),
        in_specs=[
            pl.BlockSpec((1, tc, d), lambda i, j: (i, j, 0)),
            pl.BlockSpec((1, tc, 1), lambda i, j: (i, j, 0)),
            pl.BlockSpec((None, 1, d, f), lambda i, j: (layer, i % e, 0, 0)),
            pl.BlockSpec((None, 1, d, f), lambda i, j: (layer, i % e, 0, 0)),
            pl.BlockSpec((None, 1, f, d), lambda i, j: (layer, i % e, 0, 0)),
        ],
        out_specs=pl.BlockSpec((1, tc, d), lambda i, j: (i, j, 0)),
        out_shape=jax.ShapeDtypeStruct((be, c, d), F32),
        scratch_shapes=[pltpu.VMEM((d, f), BF16), pltpu.VMEM((d, f), BF16), pltpu.VMEM((f, d), BF16)],
        compiler_params=_params(),
        name="expert_ffn",
    )(xe, gate, wg, wu, wd)


def _split3(a):
    hi = a.astype(BF16)
    r1 = a - hi.astype(F32)
    mid = r1.astype(BF16)
    lo = (r1 - mid.astype(F32)).astype(BF16)
    return hi, mid, lo


def _cumsum_both(la, incl):
    parts = _split3(la)
    cs_col = sum(jnp.dot(incl, p, preferred_element_type=F32) for p in parts)
    cs_row = sum(lax.dot_general(p, incl, (((0,), (1,)), ((), ())), preferred_element_type=F32) for p in parts)
    return cs_col, cs_row


def _ssd_kernel(*refs, chunk, rev, final, heads):
    if final:
        (xbc_ref, prev_ref, next_ref, dt_ref, cw_ref, cb_ref, dtb_ref, a_ref, dsk_ref, s0_ref,
         yb_ref, z_ref, ng_ref, y_ref, sfin_ref, s_ref, ext_ref) = refs
    else:
        (xbc_ref, prev_ref, next_ref, dt_ref, cw_ref, cb_ref, dtb_ref, a_ref, dsk_ref, s0_ref,
         y_ref, sfin_ref, act_ref, s_ref, ext_ref) = refs
    L = chunk
    i = pl.program_id(1)
    nc = pl.num_programs(1)
    j = nc - 1 - i if rev else i
    hd = SSD_HEAD_DIM
    ssd_dim = heads * hd
    rep = heads // SSD_GROUPS

    @pl.when(i == 0)
    def _():
        s_ref[...] = s0_ref[0]

    if final:
        act = xbc_ref[0]
    else:
        ext_ref[0:HALO, :] = jnp.where(j > 0, prev_ref[0], 0.0)
        ext_ref[HALO:HALO + L, :] = xbc_ref[0]
        ext_ref[HALO + L:, :] = jnp.where(j < nc - 1, next_ref[0], 0.0)
        base = HALO - CONV_W // 2
        acc = cb_ref[...] + cw_ref[0:1, :] * ext_ref[base:base + L, :]
        for k in range(1, CONV_W):
            acc = acc + cw_ref[k:k + 1, :] * ext_ref[base + k:base + k + L, :]
        act = acc * jax.nn.sigmoid(acc)
        act_ref[0] = act
    xs = act[:, :ssd_dim]
    bmat = act[:, ssd_dim:ssd_dim + SSD_GROUPS * SSD_STATE]
    cmat = act[:, ssd_dim + SSD_GROUPS * SSD_STATE:]

    dtv = jax.nn.softplus(dt_ref[0] + dtb_ref[...])
    la = dtv * a_ref[...]
    row = lax.broadcasted_iota(jnp.int32, (L, L), 0)
    col = lax.broadcasted_iota(jnp.int32, (L, L), 1)
    mask = (row <= col) if rev else (row >= col)
    cs_col, cs_row = _cumsum_both(la, mask.astype(BF16))
    last = 0 if rev else L - 1

    gmats = []
    for g in range(SSD_GROUPS):
        cg = cmat[:, g * SSD_STATE:(g + 1) * SSD_STATE].astype(BF16)
        bg = bmat[:, g * SSD_STATE:(g + 1) * SSD_STATE].astype(BF16)
        gmats.append(lax.dot_general(cg, bg, (((1,), (1,)), ((), ())), preferred_element_type=F32))

    ys = []
    for h in range(heads):
        g = h // rep
        ci = (heads if rev else 0) + h
        csc = cs_col[:, ci:ci + 1]
        csr = cs_row[ci:ci + 1, :]
        tot = csc[last:last + 1, :]
        dec = jnp.exp(jnp.where(mask, csc - csr, NEG))
        xs_h = xs[:, h * hd:(h + 1) * hd]
        xdt = (xs_h * dtv[:, ci:ci + 1]).astype(BF16)
        b_g = bmat[:, g * SSD_STATE:(g + 1) * SSD_STATE]
        c_g = cmat[:, g * SSD_STATE:(g + 1) * SSD_STATE]
        s_h = s_ref[h]
        y_h = jnp.dot((gmats[g] * dec).astype(BF16), xdt, preferred_element_type=F32)
        y_h = y_h + jnp.dot((c_g * jnp.exp(csc)).astype(BF16), s_h.astype(BF16), preferred_element_type=F32)
        local = lax.dot_general((b_g * jnp.exp(tot - csc)).astype(BF16), xdt, (((0,), (0,)), ((), ())),
                                preferred_element_type=F32)
        s_ref[h] = s_h * jnp.exp(tot) + local
        if final:
            y_h = y_h + dsk_ref[:, h * hd:(h + 1) * hd] * xs_h
        ys.append(y_h)
    y = jnp.concatenate(ys, axis=-1)
    if final:
        y = y + yb_ref[0]
        z = z_ref[0]
        y = y * (z * jax.nn.sigmoid(z))
        y = y * lax.rsqrt(jnp.mean(y * y, axis=-1, keepdims=True) + EPS) * ng_ref[...]
    y_ref[0] = y

    @pl.when(i == nc - 1)
    def _():
        sfin_ref[0] = s_ref[...]


def _ssd_pass(xbc, dt, cw, cb, dtb, a, dsk, s0, *, rev, final_inputs=None):
    b, t, width = xbc.shape
    heads = s0.shape[1]
    ssd_dim = heads * SSD_HEAD_DIM
    L = min(t, 256)
    nc = t // L
    hb = L // HALO
    final = final_inputs is not None

    def cidx(i):
        return nc - 1 - i if rev else i

    in_specs = [
        pl.BlockSpec((1, L, width), lambda bi, i: (bi, cidx(i), 0)),
        pl.BlockSpec((1, HALO, width), lambda bi, i: (bi, jnp.maximum(cidx(i) * hb - 1, 0), 0)),
        pl.BlockSpec((1, HALO, width), lambda bi, i: (bi, jnp.minimum((cidx(i) + 1) * hb, t // HALO - 1), 0)),
        pl.BlockSpec((1, L, LANE), lambda bi, i: (bi, cidx(i), 0)),
        pl.BlockSpec((8, width), lambda bi, i: (0, 0)),
        pl.BlockSpec((1, width), lambda bi, i: (0, 0)),
        pl.BlockSpec((1, LANE), lambda bi, i: (0, 0)),
        pl.BlockSpec((1, LANE), lambda bi, i: (0, 0)),
        pl.BlockSpec((1, ssd_dim), lambda bi, i: (0, 0)),
        pl.BlockSpec((1, heads, SSD_STATE, SSD_HEAD_DIM), lambda bi, i: (bi, 0, 0, 0)),
    ]
    args = [xbc, xbc, xbc, dt, cw, cb, dtb, a, dsk, s0]
    if final:
        yb, z, ng = final_inputs
        in_specs += [
            pl.BlockSpec((1, L, ssd_dim), lambda bi, i: (bi, cidx(i), 0)),
            pl.BlockSpec((1, L, ssd_dim), lambda bi, i: (bi, cidx(i), 0)),
            pl.BlockSpec((1, ssd_dim), lambda bi, i: (0, 0)),
        ]
        args += [yb, z, ng]
    out_specs = [
        pl.BlockSpec((1, L, ssd_dim), lambda bi, i: (bi, cidx(i), 0)),
        pl.BlockSpec((1, heads, SSD_STATE, SSD_HEAD_DIM), lambda bi, i: (bi, 0, 0, 0)),
    ]
    out_shape = [
        jax.ShapeDtypeStruct((b, t, ssd_dim), F32),
        jax.ShapeDtypeStruct((b, heads, SSD_STATE, SSD_HEAD_DIM), F32),
    ]
    if not final:
        out_specs.append(pl.BlockSpec((1, L, width), lambda bi, i: (bi, cidx(i), 0)))
        out_shape.append(jax.ShapeDtypeStruct((b, t, width), F32))
    return pl.pallas_call(
        functools.partial(_ssd_kernel, chunk=L, rev=rev, final=final, heads=heads),
        grid=(b, nc),
        in_specs=in_specs,
        out_specs=out_specs,
        out_shape=out_shape,
        scratch_shapes=[
            pltpu.VMEM((heads, SSD_STATE, SSD_HEAD_DIM), F32),
            pltpu.VMEM((L + 2 * HALO, width), F32),
        ],
        compiler_params=_params(),
        name="ssd_bwd" if rev else "ssd_fwd",
    )(*args)


def _ssd_stream(z, xbc, dt, conv_w, conv_b, a_log, dt_bias, d_skip, norm_g, s0_f, s0_b):
    heads = s0_f.shape[1]
    cw = jnp.pad(conv_w, ((0, 8 - CONV_W), (0, 0)))
    cb = conv_b[None]
    dtb = jnp.pad(dt_bias.reshape(1, -1), ((0, 0), (0, LANE - 2 * heads)))
    a = jnp.pad(-jnp.exp(a_log).reshape(1, -1), ((0, 0), (0, LANE - 2 * heads)))
    dsk = jnp.repeat(d_skip, SSD_HEAD_DIM)[None]
    yb, s_b, act = _ssd_pass(xbc, dt, cw, cb, dtb, a, dsk, s0_b, rev=True)
    y, s_f = _ssd_pass(act, dt, cw, cb, dtb, a, dsk, s0_f, rev=False, final_inputs=(yb, z, norm_g[None]))
    return y, s_f, s_b


def _group_sum(a, ones_bd):
    hi = a.astype(BF16)
    lo = (a - hi.astype(F32)).astype(BF16)
    return (jnp.dot(hi, ones_bd, preferred_element_type=F32) + jnp.dot(lo, ones_bd, preferred_element_type=F32))


def _mm_bf16(a, b):
    return jnp.dot(a.astype(BF16), b.astype(BF16), preferred_element_type=F32)


def _unit_tri_inverse_many(xms, rev):
    n = xms[0].shape[0]
    row = lax.broadcasted_iota(jnp.int32, (n, n), 0)
    col = lax.broadcasted_iota(jnp.int32, (n, n), 1)
    eye = (row == col).astype(F32)
    ds = None
    m, sh = 1, 0
    while m < n:
        same = (row >> (sh + 1)) == (col >> (sh + 1))
        rbit = (row >> sh) & 1
        cbit = (col >> sh) & 1
        sel = same & ((rbit == 0) & (cbit == 1) if rev else (rbit == 1) & (cbit == 0))
        cs = [jnp.where(sel, x, 0.0) for x in xms]
        if ds is None:
            ds = [eye - c for c in cs]
        elif m < SUBLANES:
            es = [_mm_bf16(c, d) for c, d in zip(cs, ds)]
            ds = [d - _mm_bf16(d, e) for d, e in zip(ds, es)]
        else:
            act = 0 if rev else 1

            def rows(a, which):
                return a.reshape(n // (2 * m), 2, m, n)[:, which].reshape(n // 2, n)

            def merge(keep, new):
                pair = (new, keep) if rev else (keep, new)
                return jnp.stack([p.reshape(n // (2 * m), m, n) for p in pair], axis=1).reshape(n, n)

            es = [_mm_bf16(rows(c, act), d) for c, d in zip(cs, ds)]
            zero = jnp.zeros((n // 2, n), F32)
            ds = [merge(rows(d, 1 - act), rows(d, act) - _mm_bf16(rows(d, act), merge(zero, e)))
                  for d, e in zip(ds, es)]
        m, sh = 2 * m, sh + 1
    return ds


def _gdn_kernel(*refs, block, rev, final, heads):
    if final:
        (qkv_ref, prev_ref, next_ref, ab_ref, cw_ref, dtb_ref, a_ref, s0_ref, ob_ref, gate_ref, ng_ref,
         o_ref, sfin_ref, s_ref, ext_ref) = refs
    else:
        (qkv_ref, prev_ref, next_ref, ab_ref, cw_ref, dtb_ref, a_ref, s0_ref,
         o_ref, sfin_ref, act_ref, s_ref, ext_ref) = refs
    TB = block
    L = block
    hd = GDN_HEAD_DIM
    dim = heads * hd
    i = pl.program_id(1)
    nb = pl.num_programs(1)
    j = nb - 1 - i if rev else i

    @pl.when(i == 0)
    def _():
        s_ref[...] = s0_ref[0]

    ri = lax.broadcasted_iota(jnp.int32, (dim, dim), 0) // hd
    ci_ = lax.broadcasted_iota(jnp.int32, (dim, dim), 1) // hd
    ones_bd = (ri == ci_).astype(BF16)
    if final:
        act = qkv_ref[0]
        q = act[:, :dim]
        k = act[:, dim:2 * dim]
        v = act[:, 2 * dim:]
    else:
        ext_ref[0:HALO, :] = jnp.where(j > 0, prev_ref[0], 0.0)
        ext_ref[HALO:HALO + TB, :] = qkv_ref[0]
        ext_ref[HALO + TB:, :] = jnp.where(j < nb - 1, next_ref[0], 0.0)
        base = HALO - CONV_W // 2
        acc = cw_ref[0:1, :] * ext_ref[base:base + TB, :]
        for kk in range(1, CONV_W):
            acc = acc + cw_ref[kk:kk + 1, :] * ext_ref[base + kk:base + kk + TB, :]
        act = acc * jax.nn.sigmoid(acc)
        q = act[:, :dim]
        k = act[:, dim:2 * dim]
        v = act[:, 2 * dim:]
        q = q * lax.rsqrt(_group_sum(q * q, ones_bd) + EPS) * (hd ** -0.5)
        k = k * lax.rsqrt(_group_sum(k * k, ones_bd) + EPS)
        act_ref[0] = jnp.concatenate([q, k, v], axis=-1)
    ab = ab_ref[0]
    gl = a_ref[...] * jax.nn.softplus(ab + dtb_ref[...])
    beta = jax.nn.sigmoid(ab)

    row = lax.broadcasted_iota(jnp.int32, (L, L), 0)
    col = lax.broadcasted_iota(jnp.int32, (L, L), 1)
    incl = (row <= col) if rev else (row >= col)
    strict = (row < col) if rev else (row > col)
    last = 0 if rev else L - 1
    cs_col, cs_row = _cumsum_both(gl, incl.astype(BF16))

    hs = range(heads)
    cis = [(heads if rev else 0) + h for h in hs]
    qh = [q[:, h * hd:(h + 1) * hd] for h in hs]
    kh = [k[:, h * hd:(h + 1) * hd] for h in hs]
    vh = [v[:, h * hd:(h + 1) * hd] for h in hs]
    csc = [cs_col[:, ci:ci + 1] for ci in cis]
    bcol = [beta[:, 2 * heads + ci:2 * heads + ci + 1] for ci in cis]
    qk_kk = [lax.dot_general(jnp.concatenate([qh[h], kh[h]], axis=0).astype(BF16), kh[h].astype(BF16),
                             (((1,), (1,)), ((), ())), preferred_element_type=F32) for h in hs]
    dec = [jnp.exp(jnp.where(incl, csc[h] - cs_row[cis[h]:cis[h] + 1, :], NEG)) for h in hs]
    attn = [(qk_kk[h][:L] * dec[h]).astype(BF16) for h in hs]
    xm = [jnp.where(strict, bcol[h] * qk_kk[h][L:] * dec[h], 0.0) for h in hs]
    tm = _unit_tri_inverse_many(xm, rev)
    egc = [jnp.exp(cc) for cc in csc]
    uw = [jnp.dot(tm[h].astype(BF16),
                  jnp.concatenate([vh[h] * bcol[h], kh[h] * (bcol[h] * egc[h])], axis=1).astype(BF16),
                  preferred_element_type=F32) for h in hs]
    s_old = [s_ref[h] for h in hs]
    rs = [jnp.dot(jnp.concatenate([uw[h][:, hd:], qh[h] * egc[h]], axis=0).astype(BF16), s_old[h].astype(BF16),
                  preferred_element_type=F32) for h in hs]
    vnb = [(uw[h][:, :hd] - rs[h][:L]).astype(BF16) for h in hs]
    outs = [rs[h][L:] + jnp.dot(attn[h], vnb[h], preferred_element_type=F32) for h in hs]
    for h in hs:
        tot = csc[h][last:last + 1, :]
        kend = (kh[h] * jnp.exp(tot - csc[h])).astype(BF16)
        s_ref[h] = s_old[h] * jnp.exp(tot) + lax.dot_general(kend, vnb[h], (((0,), (0,)), ((), ())),
                                                            preferred_element_type=F32)
    o = jnp.concatenate(outs, axis=-1)
    if final:
        o = o + ob_ref[0]
        ms = _group_sum(o * o, ones_bd) * (1.0 / hd)
        gate = gate_ref[0]
        o = o * lax.rsqrt(ms + EPS) * ng_ref[...] * (gate * jax.nn.sigmoid(gate))
    o_ref[0] = o

    @pl.when(i == nb - 1)
    def _():
        sfin_ref[0] = s_ref[...]


def _gdn_pass(qkv, ab, cw, dtb, a, s0, *, rev, block, final_inputs=None, raster_out=False):
    b, t, width = qkv.shape
    heads = s0.shape[1]
    dim = heads * GDN_HEAD_DIM
    TB = block
    nb = t // TB
    hb = TB // HALO
    final = final_inputs is not None

    def bidx(i):
        return nb - 1 - i if rev else i

    in_specs = [
        pl.BlockSpec((1, TB, width), lambda bi, i: (bi, bidx(i), 0)),
        pl.BlockSpec((1, HALO, width), lambda bi, i: (bi, jnp.maximum(bidx(i) * hb - 1, 0), 0)),
        pl.BlockSpec((1, HALO, width), lambda bi, i: (bi, jnp.minimum((bidx(i) + 1) * hb, t // HALO - 1), 0)),
        pl.BlockSpec((1, TB, LANE), lambda bi, i: (bi, bidx(i), 0)),
        pl.BlockSpec((8, width), lambda bi, i: (0, 0)),
        pl.BlockSpec((1, LANE), lambda bi, i: (0, 0)),
        pl.BlockSpec((1, LANE), lambda bi, i: (0, 0)),
        pl.BlockSpec((1, heads, GDN_HEAD_DIM, GDN_HEAD_DIM), lambda bi, i: (bi, 0, 0, 0)),
    ]
    args = [qkv, qkv, qkv, ab, cw, dtb, a, s0]
    if final:
        ob, gate, ng = final_inputs
        in_specs += [
            pl.BlockSpec((1, TB, dim), lambda bi, i: (bi, bidx(i), 0)),
            pl.BlockSpec((1, TB, dim), lambda bi, i: (bi, bidx(i), 0)),
            pl.BlockSpec((1, dim), lambda bi, i: (0, 0)),
        ]
        args += [ob, gate, ng]
    if raster_out:
        assert nb == GRID_W
        o_spec = pl.BlockSpec((1, TB, dim), lambda bi, i: (bi, 0, bidx(i)))
        o_shape = jax.ShapeDtypeStruct((b, TB, GRID_W * dim), F32)
    else:
        o_spec = pl.BlockSpec((1, TB, dim), lambda bi, i: (bi, bidx(i), 0))
        o_shape = jax.ShapeDtypeStruct((b, t, dim), F32)
    out_specs = [o_spec, pl.BlockSpec((1, heads, GDN_HEAD_DIM, GDN_HEAD_DIM), lambda bi, i: (bi, 0, 0, 0))]
    out_shape = [o_shape, jax.ShapeDtypeStruct((b, heads, GDN_HEAD_DIM, GDN_HEAD_DIM), F32)]
    if not final:
        out_specs.append(pl.BlockSpec((1, TB, width), lambda bi, i: (bi, bidx(i), 0)))
        out_shape.append(jax.ShapeDtypeStruct((b, t, width), F32))
    outs = pl.pallas_call(
        functools.partial(_gdn_kernel, block=TB, rev=rev, final=final, heads=heads),
        grid=(b, nb),
        in_specs=in_specs,
        out_specs=out_specs,
        out_shape=out_shape,
        scratch_shapes=[
            pltpu.VMEM((heads, GDN_HEAD_DIM, GDN_HEAD_DIM), F32),
            pltpu.VMEM((TB + 2 * HALO, width), F32),
        ],
        compiler_params=_params(),
        name="gdn_bwd" if rev else "gdn_fwd",
    )(*args)
    return (outs[0].reshape(b, t, dim),) + tuple(outs[1:])


def _gdn_stream(qkv, gate, ab, conv_w, a_log, dt_bias, norm_g, s0_f, s0_b, *, column_major):
    t = qkv.shape[1]
    heads = s0_f.shape[1]
    cw = jnp.pad(conv_w, ((0, 8 - CONV_W), (0, 0)))
    dtb = jnp.pad(dt_bias.reshape(1, -1), ((0, 0), (0, LANE - 2 * heads)))
    a = jnp.pad(-jnp.exp(a_log).reshape(1, -1), ((0, 0), (0, LANE - 2 * heads)))
    ng = jnp.tile(norm_g, heads)[None]
    rows = t // GRID_W
    fused_raster = column_major and rows <= 256 and rows >= 2 * HALO and rows & (rows - 1) == 0
    block = rows if fused_raster else min(t, 256)
    ob, s_b, act = _gdn_pass(qkv, ab, cw, dtb, a, s0_b, rev=True, block=block)
    o, s_f = _gdn_pass(act, ab, cw, dtb, a, s0_f, rev=False, block=block, final_inputs=(ob, gate, ng),
                       raster_out=fused_raster)
    if column_major and not fused_raster:
        o = _to_raster(o)
    return o, s_f, s_b


def _router_kernel(x_ref, sc_ref, sh_ref, g_ref, rw_ref, h_ref, a_ref):
    x = x_ref[0]
    ms = jnp.mean(x * x, axis=-1, keepdims=True)
    h = x * lax.rsqrt(ms + EPS) * g_ref[...]
    h = h * (1.0 + sc_ref[0]) + sh_ref[0]
    hh, hm, _ = _split3(h)
    h_ref[0] = hh
    rw = rw_ref[...]
    rh = rw.astype(BF16)
    rm = (rw - rh.astype(F32)).astype(BF16)
    nt = (((1,), (1,)), ((), ()))
    lg = (lax.dot_general(rh, hh, nt, preferred_element_type=F32)
          + lax.dot_general(rh, hm, nt, preferred_element_type=F32)
          + lax.dot_general(rm, hh, nt, preferred_element_type=F32))
    ex = jnp.exp(lg - jnp.max(lg, axis=0, keepdims=True))
    aff = ex / jnp.sum(ex, axis=0, keepdims=True)
    for k in range(a_ref.shape[1]):
        a_ref[0, k] = aff[:, k * LANE:(k + 1) * LANE]


def _router(x, scale, shift, g, rw_t):
    b, t, d = x.shape
    e = rw_t.shape[0]
    tm = min(t, 1024)
    return pl.pallas_call(
        _router_kernel,
        grid=(b, t // tm),
        in_specs=[
            pl.BlockSpec((1, tm, d), lambda i, j: (i, j, 0)),
            pl.BlockSpec((1, 1, d), lambda i, j: (i, 0, 0)),
            pl.BlockSpec((1, 1, d), lambda i, j: (i, 0, 0)),
            pl.BlockSpec((1, d), lambda i, j: (0, 0)),
            pl.BlockSpec((e, d), lambda i, j: (0, 0)),
        ],
        out_specs=[
            pl.BlockSpec((1, tm, d), lambda i, j: (i, j, 0)),
            pl.BlockSpec((1, tm // LANE, e, LANE), lambda i, j: (i, j, 0, 0)),
        ],
        out_shape=[
            jax.ShapeDtypeStruct((b, t, d), BF16),
            jax.ShapeDtypeStruct((b, t // LANE, e, LANE), F32),
        ],
        compiler_params=_params(),
        name="router",
    )(x, scale, shift, g, rw_t)


def _token_prefix(m3, ut, ones, lt):
    e, nb, _ = m3.shape
    m2 = m3.reshape(e * nb, LANE).astype(BF16)
    inb = jnp.dot(m2, ut, preferred_element_type=F32).reshape(e, nb, LANE)
    tot = jnp.dot(m2, ones, preferred_element_type=F32).reshape(e, nb, LANE)
    offs = jnp.stack([jnp.dot(lt, tot[i].astype(BF16), preferred_element_type=F32) for i in range(e)], axis=0)
    return inb, tot, offs


def _select_kernel(a_ref, idx_ref, gate_ref, srow_ref, st8_ref, npc_ref, cs_ref, cum_ref, cnt_ref, *, cap):
    a = a_ref[0]
    e_n, nb, _ = a.shape
    bits = lax.bitcast_convert_type(a, I32)

    def radix(i, prefix):
        cand = prefix | jnp.left_shift(jnp.int32(1), 30 - i)
        cnt = jnp.sum(jnp.sum((bits >= cand).astype(F32), axis=2, keepdims=True), axis=1, keepdims=True)
        return jnp.where(cnt >= cap, cand, prefix)

    thr = lax.fori_loop(0, 31, radix, jnp.zeros((e_n, 1, 1), I32))
    li = lax.broadcasted_iota(I32, (LANE, LANE), 0)
    lj = lax.broadcasted_iota(I32, (LANE, LANE), 1)
    ut = (li < lj).astype(BF16)
    ones = jnp.ones((LANE, LANE), BF16)
    bi = lax.broadcasted_iota(I32, (nb, nb), 0)
    bj = lax.broadcasted_iota(I32, (nb, nb), 1)
    lt = (bj < bi).astype(BF16)

    gt = bits > thr
    eq = bits == thr
    n_gt = jnp.sum(jnp.sum(gt.astype(F32), axis=2, keepdims=True), axis=1, keepdims=True)
    tie_in, _, tie_offs = _token_prefix(eq.astype(F32), ut, ones, lt)
    sel = gt | (eq & (tie_in + tie_offs < cap - n_gt))
    pos_in, cnt, offs = _token_prefix(sel.astype(F32), ut, ones, lt)
    cs_ref[...] = jnp.where(sel, pos_in + 1.0, 0.0)
    cum_ref[...] = offs + cnt
    cnt_ref[...] = cnt

    offs_i = offs.astype(I32)
    cnt_i = cnt.astype(I32)
    st8 = (offs_i >> 3) << 3
    npc = jnp.where(cnt_i > 0, (offs_i + cnt_i - st8 + (PIECE - 1)) >> 3, 0)
    rbase = []
    run = jnp.zeros((nb, LANE), I32)
    for i in range(e_n):
        rbase.append(run)
        run = run + PIECE * npc[i]
    rbase = jnp.stack(rbase, axis=0)
    srow_ref[0] = jnp.where(sel, rbase + pos_in.astype(I32) + offs_i - st8, -1)
    st8_ref[0] = st8
    npc_ref[0] = npc

    jrow = lax.broadcasted_iota(I32, (1, cap), 1).astype(F32)
    sub_nb = lax.broadcasted_iota(I32, (nb, cap), 0).astype(F32)
    sub_l = lax.broadcasted_iota(I32, (LANE, cap), 0).astype(F32)
    tn = (((0,), (0,)), ((), ()))

    def compact(ei, carry):
        cum_col = cum_ref[ei][:, 0:1]
        cnt_col = cnt_ref[ei][:, 0:1]
        ge = cum_col <= jrow
        blk_j = jnp.sum(ge.astype(F32), axis=0, keepdims=True)
        offs_j = jnp.sum(jnp.where(ge, cnt_col, 0.0), axis=0, keepdims=True)
        rank1 = jrow - offs_j + 1.0
        g_t = (sub_nb == blk_j).astype(BF16)
        row_t = lax.dot_general(cs_ref[ei].astype(BF16), g_t, tn, preferred_element_type=F32)
        match = row_t == rank1
        lane_j = jnp.sum(jnp.where(match, sub_l, 0.0), axis=0, keepdims=True)
        idx_ref[0, pl.ds(ei, 1), :] = (blk_j * LANE + lane_j).astype(I32)
        parts = _split3(a_ref[0, ei])
        aff_t = sum(lax.dot_general(p, g_t, tn, preferred_element_type=F32) for p in parts)
        gate_ref[0, pl.ds(ei, 1), :] = jnp.sum(jnp.where(match, aff_t, 0.0), axis=0, keepdims=True)
        return carry

    lax.fori_loop(0, e_n, compact, 0)


def _ec_select(aff_em, cap):
    b, e, nb, _ = aff_em.shape
    big = lambda dt: jax.ShapeDtypeStruct((b, e, nb, LANE), dt)
    spec4 = pl.BlockSpec((1, e, nb, LANE), lambda i: (i, 0, 0, 0))
    spec3 = pl.BlockSpec((1, e, cap), lambda i: (i, 0, 0))
    return pl.pallas_call(
        functools.partial(_select_kernel, cap=cap),
        grid=(b,),
        in_specs=[spec4],
        out_specs=[spec3, spec3, spec4, spec4, spec4],
        out_shape=[jax.ShapeDtypeStruct((b, e, cap), I32), jax.ShapeDtypeStruct((b, e, cap), F32),
                   big(I32), big(I32), big(I32)],
        scratch_shapes=[pltpu.VMEM((e, nb, LANE), F32)] * 3,
        compiler_params=pltpu.CompilerParams(dimension_semantics=("arbitrary",), vmem_limit_bytes=VMEM_LIMIT),
        name="ec_select",
    )(aff_em)


def _combine_kernel(st8_sm, npc_sm, ye_hbm, srow_ref, x_ref, g_ref, fn_ref, o_ref, stage, acc_ref, sem,
                    *, final, n_exp):
    b = pl.program_id(0)
    k = pl.program_id(1)
    nb = pl.num_programs(1)
    step = b * nb + k
    nsteps = pl.num_programs(0) * nb
    slot = step % 2

    def piece_copy(bb, e, src_row, sl, dst_row):
        return pltpu.make_async_copy(ye_hbm.at[bb, e, pl.ds(src_row, PIECE), :],
                                     stage.at[sl, pl.ds(dst_row, PIECE), :], sem.at[sl])

    def issue(st, sl):
        bb = st // nb

        def per_e(e, r):
            s8 = st8_sm[st * n_exp + e]
            n = npc_sm[st * n_exp + e]

            def per_p(p, r2):
                piece_copy(bb, e, pl.multiple_of(s8 + PIECE * p, PIECE), sl, pl.multiple_of(r2, PIECE)).start()
                return r2 + PIECE

            return lax.fori_loop(0, n, per_p, r)

        lax.fori_loop(0, n_exp, per_e, 0)

    @pl.when(step == 0)
    def _():
        issue(step, slot)

    @pl.when(step + 1 < nsteps)
    def _():
        issue(step + 1, 1 - slot)

    npieces = lax.fori_loop(0, n_exp, lambda e, s: s + npc_sm[step * n_exp + e], 0)
    rows = npieces * PIECE

    def wait_one(p, c):
        piece_copy(0, 0, 0, slot, 0).wait()
        return c

    lax.fori_loop(0, npieces, wait_one, 0)
    stage[slot, pl.ds(pl.multiple_of(rows, PIECE), LANE), :] = jnp.zeros((LANE, stage.shape[-1]), F32)

    srow = srow_ref[0, 0]
    acc_ref[...] = jnp.zeros_like(acc_ref)
    riota = lax.broadcasted_iota(I32, (LANE, LANE), 0)
    tn = (((0,), (0,)), ((), ()))

    def chunk(c, carry):
        r0 = pl.multiple_of(c * LANE, LANE)
        rid = riota + r0
        pt = (srow[0:1, :] == rid).astype(F32)
        for e in range(1, n_exp):
            pt = pt + (srow[e:e + 1, :] == rid).astype(F32)
        ptb = pt.astype(BF16)
        st = stage[slot, pl.ds(r0, LANE), :]
        hi = st.astype(BF16)
        lo = (st - hi.astype(F32)).astype(BF16)
        acc_ref[...] += lax.dot_general(jnp.concatenate([ptb, ptb], axis=0), jnp.concatenate([hi, lo], axis=0), tn,
                                        preferred_element_type=F32)
        return carry

    lax.fori_loop(0, (rows + LANE - 1) // LANE, chunk, 0)
    y = x_ref[0] + g_ref[0] * acc_ref[...]
    if final:
        y = y * lax.rsqrt(jnp.mean(y * y, axis=-1, keepdims=True) + EPS) * fn_ref[...]
    o_ref[0] = y


def _ec_combine(ye, srow_bm, st8, npc, x, gate, final_g=None):
    b, t, d = x.shape
    e = ye.shape[1]
    nb = t // LANE
    final = final_g is not None
    fn = final_g if final else jnp.ones((1, d), F32)
    max_rows = e * (LANE + 2 * PIECE) + LANE
    grid_spec = pltpu.PrefetchScalarGridSpec(
        num_scalar_prefetch=2,
        grid=(b, nb),
        in_specs=[
            pl.BlockSpec(memory_space=pl.ANY),
            pl.BlockSpec((1, 1, e, LANE), lambda i, j, *_: (i, j, 0, 0)),
            pl.BlockSpec((1, LANE, d), lambda i, j, *_: (i, j, 0)),
            pl.BlockSpec((1, 1, d), lambda i, j, *_: (i, 0, 0)),
            pl.BlockSpec((1, d), lambda i, j, *_: (0, 0)),
        ],
        out_specs=pl.BlockSpec((1, LANE, d), lambda i, j, *_: (i, j, 0)),
        scratch_shapes=[
            pltpu.VMEM((2, max_rows, d), F32),
            pltpu.VMEM((LANE, d), F32),
            pltpu.SemaphoreType.DMA((2,)),
        ],
    )
    return pl.pallas_call(
        functools.partial(_combine_kernel, final=final, n_exp=e),
        grid_spec=grid_spec,
        out_shape=jax.ShapeDtypeStruct((b, t, d), F32),
        compiler_params=_params(),
        name="ec_combine",
    )(st8, npc, ye, srow_bm, x, gate, fn)


def _expert_choice_block(x, scale, shift, g2, gate2, router_w, ew, final_g):
    b, t, d = x.shape
    e = router_w.shape[-1]
    cap = EC_CAPACITY * t // e
    h2, aff_bm = _router(x, scale, shift, g2, router_w.T)
    idx, gate, srow, st8, npc = _ec_select(jnp.transpose(aff_bm, (0, 2, 1, 3)), cap)
    xe = jax.vmap(lambda hb, ib: hb[ib])(h2, idx)
    ye = _expert_ffn(xe.reshape(b * e, cap, d), gate.reshape(b * e, cap, 1), *ew).reshape(b, e, cap, d)
    srow_bm = jnp.transpose(srow, (0, 2, 1, 3))
    st8_f = jnp.transpose(st8[..., 0], (0, 2, 1)).reshape(-1)
    npc_f = jnp.transpose(npc[..., 0], (0, 2, 1)).reshape(-1)
    return _ec_combine(ye, srow_bm, st8_f, npc_f, x, gate2, final_g)


def _rms_norm(x, g):
    return x * lax.rsqrt(jnp.mean(x * x, axis=-1, keepdims=True) + EPS) * g


def _to_raster(u):
    b, t, c = u.shape
    rows = t // GRID_W
    return u.reshape(b, GRID_W, rows, c).transpose(0, 2, 1, 3).reshape(b, t, c)


def _expert_choice_ffn(h, router_w, ew):
    b, t, d = h.shape
    cap = EC_CAPACITY * t // N_EXPERTS
    aff = jax.nn.softmax(jnp.einsum('btd,de->bte', h, router_w, precision=lax.Precision.HIGHEST), axis=-1)
    gate, idx = lax.top_k(jnp.swapaxes(aff, 1, 2), cap)
    xe = jax.vmap(lambda hb, ib: hb[ib])(h, idx)
    ye = _expert_ffn(xe.reshape(b * N_EXPERTS, cap, d), gate.reshape(b * N_EXPERTS, cap, 1), *ew)
    ye = ye.reshape(b, N_EXPERTS, cap, d)
    return jax.vmap(lambda ib, yb: jnp.zeros((t, d), yb.dtype).at[ib.reshape(-1)].add(yb.reshape(-1, d)))(idx, ye)


def kernel(x, c, ctx, c_ctx, norm1_g, norm2_g, ada_w, ada_b, w_in, w_out, pool_w, pool_scale, ssd_conv_w, ssd_conv_b, ssd_a_log, ssd_dt_bias, ssd_d, ssd_norm_g, gdn_conv_w, gdn_a_log, gdn_dt_bias, gdn_norm_g, router_w, exp_w_gate, exp_w_up, exp_w_down, final_norm_g):
    depth, d, _ = w_in.shape
    b, t, _ = x.shape
    pool_dim = pool_scale.shape[-1]
    ssd_dim = ssd_norm_g.shape[-1]
    ssd_heads = ssd_dim // SSD_HEAD_DIM
    ssd_bc = SSD_GROUPS * SSD_STATE
    gdn_dim = gdn_conv_w.shape[-1] // 3
    gdn_heads = gdn_dim // GDN_HEAD_DIM
    splits = (pool_dim, ssd_dim, ssd_dim + 2 * ssd_bc, 2 * ssd_heads, 3 * gdn_dim, gdn_dim, 2 * gdn_heads,
              2 * gdn_heads)
    cut = [0] + np.cumsum(splits).tolist()
    r_widths = (ssd_dim, ssd_dim + 2 * ssd_bc, pool_dim, LANE)
    g_widths = (3 * gdn_dim, gdn_dim, LANE)

    sc = jax.nn.silu(c)
    scc = jax.nn.silu(c_ctx)[None]
    for l in range(depth):
        last = l == depth - 1
        wl = w_in[l]
        seg = [wl[:, cut[i]:cut[i + 1]] for i in range(8)]
        w_r = jnp.concatenate(
            [seg[1], seg[2], seg[0], jnp.pad(seg[3], ((0, 0), (0, LANE - 2 * ssd_heads)))], axis=1).astype(BF16)
        w_g = jnp.concatenate(
            [seg[4], seg[5], jnp.pad(jnp.concatenate([seg[6], seg[7]], axis=1), ((0, 0), (0, LANE - 4 * gdn_heads)))],
            axis=1).astype(BF16)
        w_o = w_out[l].astype(BF16)
        ew = (exp_w_gate, exp_w_up, exp_w_down, l)
        m_lat = jnp.split(sc @ ada_w[l] + ada_b[l], 6, axis=-1)
        m_ctx = [jnp.broadcast_to(m, (b, d)) for m in jnp.split(scc @ ada_w[l] + ada_b[l], 6, axis=-1)]
        g1 = norm1_g[l][None]

        def project(xx, mm, column_major):
            pr = _inproj(xx, mm[1][:, None], mm[0][:, None], g1, w_r, r_widths, column_major=False)
            pg = _inproj(xx, mm[1][:, None], mm[0][:, None], g1, w_g, g_widths, column_major=column_major)
            return pr, pg

        ssd_p = (ssd_conv_w[l], ssd_conv_b[l], ssd_a_log[l], ssd_dt_bias[l], ssd_d[l], ssd_norm_g[l])
        gdn_p = (gdn_conv_w[l], gdn_a_log[l], gdn_dt_bias[l], gdn_norm_g[l])
        zs = jnp.zeros((b, ssd_heads, SSD_STATE, SSD_HEAD_DIM), F32)
        zg = jnp.zeros((b, gdn_heads, GDN_HEAD_DIM, GDN_HEAD_DIM), F32)

        (c_z, c_xbc, c_pool, c_dt), (c_qkv, c_gate, c_ab) = project(ctx, m_ctx, False)
        (l_z, l_xbc, l_pool, l_dt), (l_qkv, l_gate, l_ab) = project(x, m_lat, True)

        s_ctx, ssd_sf, ssd_sb = _ssd_stream(c_z, c_xbc, c_dt, *ssd_p, zs, zs)
        g_ctx, gdn_sf, gdn_sb = _gdn_stream(c_qkv, c_gate, c_ab, *gdn_p, zg, zg, column_major=False)
        s_lat, _, _ = _ssd_stream(l_z, l_xbc, l_dt, *ssd_p, ssd_sf, ssd_sb)
        g_lat, _, _ = _gdn_stream(l_qkv, l_gate, l_ab, *gdn_p, gdn_sf, gdn_sb, column_major=True)
        w_o_parts = [w_o[:pool_dim], w_o[pool_dim:pool_dim + ssd_dim], w_o[pool_dim + ssd_dim:]]
        x = _outproj([_pool_branch(l_pool, pool_w[l], pool_scale[l]), s_lat, g_lat], x, m_lat[2][:, None],
                     w_o_parts)
        fin = final_norm_g[None] if last else None
        if t % 1024 == 0:
            x = _expert_choice_block(x, m_lat[4][:, None], m_lat[3][:, None], norm2_g[l][None], m_lat[5][:, None],
                                     router_w[l], ew, fin)
        else:
            h2 = _rms_norm(x, norm2_g[l]) * (1 + m_lat[4][:, None]) + m_lat[3][:, None]
            x = x + m_lat[5][:, None] * _expert_choice_ffn(h2, router_w[l], ew)
            if last:
                x = _rms_norm(x, final_norm_g)
        if not last:
            ctx = _outproj([_pool_branch(c_pool, pool_w[l], pool_scale[l]), s_ctx, g_ctx], ctx, m_ctx[2][:, None],
                           w_o_parts)
            h2c = _rms_norm(ctx, norm2_g[l]) * (1 + m_ctx[4][:, None]) + m_ctx[3][:, None]
            ctx = ctx + m_ctx[5][:, None] * _expert_choice_ffn(h2c, router_w[l], ew)
    return x
```

```python
import functools

import numpy as np
import jax
import jax.numpy as jnp
from jax import lax
from jax.experimental import pallas as pl
from jax.experimental.pallas import tpu as pltpu

F32 = jnp.float32
BF16 = jnp.bfloat16

GRID_W = 64
CHUNK = 64
CONV_W = 5
POOL_WINDOWS = (2, 4, 8, 16)
SSD_HEAD_DIM = 64
SSD_GROUPS = 2
SSD_STATE = 128
GDN_HEAD_DIM = 64
N_EXPERTS = 16
EC_CAPACITY = 2
EPS = 1e-6
LANE = 128
SUBLANES = 8
HALO = 8
NEG = -1e30
PIECE = 8
I32 = jnp.int32
VMEM_LIMIT = 48 * 1024 * 1024


def _round_up(n, m):
    return (n + m - 1) // m * m


def _params():
    return pltpu.CompilerParams(dimension_semantics=("arbitrary", "arbitrary"), vmem_limit_bytes=VMEM_LIMIT)


def _inproj_kernel(x_ref, sc_ref, sh_ref, g_ref, w_ref, *o_refs, transpose_grid):
    x = x_ref[0]
    tm = x.shape[0]
    ms = jnp.mean(x * x, axis=-1, keepdims=True)
    h = x * lax.rsqrt(ms + EPS) * g_ref[...]
    h = (h * (1.0 + sc_ref[0]) + sh_ref[0]).astype(BF16)
    if transpose_grid:
        p = lax.broadcasted_iota(jnp.int32, (tm, tm), 0)
        q = lax.broadcasted_iota(jnp.int32, (tm, tm), 1)
        perm = (q == (p % SUBLANES) * GRID_W + p // SUBLANES).astype(BF16)
        h = jnp.dot(perm, h, preferred_element_type=F32).astype(BF16)
    y = jnp.dot(h, w_ref[...], preferred_element_type=F32)
    off = 0
    for o_ref in o_refs:
        n = o_ref.shape[-1]
        if transpose_grid:
            o_ref[0] = y[:, off:off + n].reshape(GRID_W, SUBLANES, n)
        else:
            o_ref[0] = y[:, off:off + n]
        off += n


def _inproj(x, scale, shift, g, w, widths, *, column_major):
    b, t, d = x.shape
    rows = t // GRID_W
    if column_major:
        assert rows % SUBLANES == 0
        tm = SUBLANES * GRID_W
        out_specs = [pl.BlockSpec((1, GRID_W, SUBLANES, n), lambda i, j: (i, 0, j, 0)) for n in widths]
        out_shape = [jax.ShapeDtypeStruct((b, GRID_W, rows, n), F32) for n in widths]
    else:
        tm = min(t, 1024)
        out_specs = [pl.BlockSpec((1, tm, n), lambda i, j: (i, j, 0)) for n in widths]
        out_shape = [jax.ShapeDtypeStruct((b, t, n), F32) for n in widths]
    outs = pl.pallas_call(
        functools.partial(_inproj_kernel, transpose_grid=column_major),
        grid=(b, t // tm),
        in_specs=[
            pl.BlockSpec((1, tm, d), lambda i, j: (i, j, 0)),
            pl.BlockSpec((1, 1, d), lambda i, j: (i, 0, 0)),
            pl.BlockSpec((1, 1, d), lambda i, j: (i, 0, 0)),
            pl.BlockSpec((1, d), lambda i, j: (0, 0)),
            pl.BlockSpec((d, w.shape[1]), lambda i, j: (0, 0)),
        ],
        out_specs=out_specs,
        out_shape=out_shape,
        compiler_params=_params(),
        name="inproj",
    )(x, scale, shift, g, w)
    return [o.reshape(b, t, n) for o, n in zip(outs, widths)]


def _outproj_kernel(*refs, n_in):
    a_refs = refs[:n_in]
    x_ref, gate_ref, w_ref, o_ref = refs[n_in:]
    a = jnp.concatenate([a_ref[0].astype(BF16) for a_ref in a_refs], axis=-1)
    y = jnp.dot(a, w_ref[...], preferred_element_type=F32)
    o_ref[0] = x_ref[0] + gate_ref[0] * y


def _outproj(parts, x, gate, w):
    b, t, d = x.shape
    tm = min(t, 512)
    n_in = len(parts)
    return pl.pallas_call(
        functools.partial(_outproj_kernel, n_in=n_in),
        grid=(b, t // tm),
        in_specs=(
            [pl.BlockSpec((1, tm, p.shape[-1]), lambda i, j: (i, j, 0)) for p in parts]
            + [pl.BlockSpec((1, tm, d), lambda i, j: (i, j, 0)), pl.BlockSpec((1, 1, d), lambda i, j: (i, 0, 0)),
               pl.BlockSpec(w.shape, lambda i, j: (0, 0))]),
        out_specs=pl.BlockSpec((1, tm, d), lambda i, j: (i, j, 0)),
        out_shape=jax.ShapeDtypeStruct((b, t, d), F32),
        compiler_params=_params(),
        name="outproj",
    )(*parts, x, gate, w)


def _pool_kernel(u_ref, prev_ref, next_ref, w_ref, sc_ref, o_ref, e0_ref, e1_ref, *, block, seq, group):
    TM = block
    j = pl.program_id(1)
    nb = pl.num_programs(1)
    u = u_ref[0]
    n_ext = TM + 2 * HALO
    e0_ref[0:HALO, :] = jnp.where(j > 0, prev_ref[0], 0.0)
    e0_ref[HALO:HALO + TM, :] = u
    e0_ref[HALO + TM:, :] = jnp.where(j < nb - 1, next_ref[0], 0.0)
    tok = j * TM + lax.broadcasted_iota(jnp.int32, (TM, 1), 0)
    lane = lax.broadcasted_iota(jnp.int32, (1, u.shape[-1]), 1)
    src, dst = e0_ref, e1_ref
    pooled = jnp.zeros_like(u)
    half = 1
    for gi, win in enumerate(POOL_WINDOWS):
        assert win == 2 * half
        lo_r, hi_r = half, n_ext - half
        if half == 1:
            dst[lo_r:hi_r, :] = src[lo_r - 1:hi_r - 1, :] + src[lo_r:hi_r, :]
        else:
            q = half // 2
            dst[lo_r:hi_r, :] = src[lo_r - q:hi_r - q, :] + src[lo_r + q:hi_r + q, :]
        cnt = (jnp.minimum(tok + half, seq) - jnp.maximum(tok - half, 0)).astype(F32)
        mean = dst[HALO:HALO + TM, :] / cnt
        pooled = jnp.where((lane >= gi * group) & (lane < (gi + 1) * group), mean, pooled)
        src, dst = dst, src
        half *= 2
    dd = pooled - u
    y = jnp.dot(dd.astype(BF16), w_ref[...], preferred_element_type=F32)
    o_ref[0] = y * sc_ref[...]


def _pool_branch(u, pool_w, pool_scale):
    b, t, c = u.shape
    ng, pg, _ = pool_w.shape
    tm = min(t, 512)
    hb = tm // HALO
    w_bd = jnp.zeros((c, c), F32)
    for gi in range(ng):
        w_bd = w_bd.at[gi * pg:(gi + 1) * pg, gi * pg:(gi + 1) * pg].set(pool_w[gi])
    return pl.pallas_call(
        functools.partial(_pool_kernel, block=tm, seq=t, group=pg),
        grid=(b, t // tm),
        in_specs=[
            pl.BlockSpec((1, tm, c), lambda i, j: (i, j, 0)),
            pl.BlockSpec((1, HALO, c), lambda i, j: (i, jnp.maximum(j * hb - 1, 0), 0)),
            pl.BlockSpec((1, HALO, c), lambda i, j: (i, jnp.minimum((j + 1) * hb, t // HALO - 1), 0)),
            pl.BlockSpec((c, c), lambda i, j: (0, 0)),
            pl.BlockSpec((1, c), lambda i, j: (0, 0)),
        ],
        out_specs=pl.BlockSpec((1, tm, c), lambda i, j: (i, j, 0)),
        out_shape=jax.ShapeDtypeStruct((b, t, c), F32),
        scratch_shapes=[pltpu.VMEM((tm + 2 * HALO, c), F32), pltpu.VMEM((tm + 2 * HALO, c), F32)],
        compiler_params=_params(),
        name="pool",
    )(u, u, u, w_bd.astype(BF16), pool_scale[None])


def _ffn_kernel(x_ref, gate_ref, wg_ref, wu_ref, wd_ref, o_ref, wgb_ref, wub_ref, wdb_ref):
    @pl.when(pl.program_id(1) == 0)
    def _():
        wgb_ref[...] = wg_ref[0].astype(BF16)
        wub_ref[...] = wu_ref[0].astype(BF16)
        wdb_ref[...] = wd_ref[0].astype(BF16)

    x = x_ref[0].astype(BF16)
    hg = jnp.dot(x, wgb_ref[...], preferred_element_type=F32)
    hu = jnp.dot(x, wub_ref[...], preferred_element_type=F32)
    hid = (hg * jax.nn.sigmoid(hg)) * hu
    y = jnp.dot(hid.astype(BF16), wdb_ref[...], preferred_element_type=F32)
    o_ref[0] = y * gate_ref[0]


def _expert_ffn(xe, gate, wg, wu, wd, layer):
    be, c, d = xe.shape
    _, e, _, f = wg.shape
    tc = min(c, 1024)
    return pl.pallas_call(
        _ffn_kernel,
        grid=(be, c // tc),
        in_specs=[
            pl.BlockSpec((1, tc, d), lambda i, j: (i, j, 0)),
            pl.BlockSpec((1, tc, 1), lambda i, j: (i, j, 0)),
            pl.BlockSpec((None, 1, d, f), lambda i, j: (layer, i % e, 0, 0)),
            pl.BlockSpec((None, 1, d, f), lambda i, j: (layer, i % e, 0, 0)),
            pl.BlockSpec((None, 1, f, d), lambda i, j: (layer, i % e, 0, 0)),
        ],
        out_specs=pl.BlockSpec((1, tc, d), lambda i, j: (i, j, 0)),
        out_shape=jax.ShapeDtypeStruct((be, c, d), F32),
        scratch_shapes=[pltpu.VMEM((d, f), BF16), pltpu.VMEM((d, f), BF16), pltpu.VMEM((f, d), BF16)],
        compiler_params=_params(),
        name="expert_ffn",
    )(xe, gate, wg, wu, wd)


def _split3(a):
    hi = a.astype(BF16)
    r1 = a - hi.astype(F32)
    mid = r1.astype(BF16)
    lo = (r1 - mid.astype(F32)).astype(BF16)
    return hi, mid, lo


def _cumsum_both(la, incl):
    parts = _split3(la)
    cs_col = sum(jnp.dot(incl, p, preferred_element_type=F32) for p in parts)
    cs_row = sum(lax.dot_general(p, incl, (((0,), (1,)), ((), ())), preferred_element_type=F32) for p in parts)
    return cs_col, cs_row


def _ssd_kernel(*refs, chunk, rev, final, heads):
    if final:
        (xbc_ref, prev_ref, next_ref, dt_ref, cw_ref, cb_ref, dtb_ref, a_ref, dsk_ref, s0_ref,
         yb_ref, z_ref, ng_ref, y_ref, sfin_ref, s_ref, ext_ref) = refs
    else:
        (xbc_ref, prev_ref, next_ref, dt_ref, cw_ref, cb_ref, dtb_ref, a_ref, dsk_ref, s0_ref,
         y_ref, sfin_ref, act_ref, s_ref, ext_ref) = refs
    L = chunk
    i = pl.program_id(1)
    nc = pl.num_programs(1)
    j = nc - 1 - i if rev else i
    hd = SSD_HEAD_DIM
    ssd_dim = heads * hd
    rep = heads // SSD_GROUPS

    @pl.when(i == 0)
    def _():
        s_ref[...] = s0_ref[0]

    if final:
        act = xbc_ref[0]
    else:
        ext_ref[0:HALO, :] = jnp.where(j > 0, prev_ref[0], 0.0)
        ext_ref[HALO:HALO + L, :] = xbc_ref[0]
        ext_ref[HALO + L:, :] = jnp.where(j < nc - 1, next_ref[0], 0.0)
        base = HALO - CONV_W // 2
        acc = cb_ref[...] + cw_ref[0:1, :] * ext_ref[base:base + L, :]
        for k in range(1, CONV_W):
            acc = acc + cw_ref[k:k + 1, :] * ext_ref[base + k:base + k + L, :]
        act = acc * jax.nn.sigmoid(acc)
        act_ref[0] = act
    xs = act[:, :ssd_dim]
    bmat = act[:, ssd_dim:ssd_dim + SSD_GROUPS * SSD_STATE]
    cmat = act[:, ssd_dim + SSD_GROUPS * SSD_STATE:]

    dtv = jax.nn.softplus(dt_ref[0] + dtb_ref[...])
    la = dtv * a_ref[...]
    row = lax.broadcasted_iota(jnp.int32, (L, L), 0)
    col = lax.broadcasted_iota(jnp.int32, (L, L), 1)
    mask = (row <= col) if rev else (row >= col)
    cs_col, cs_row = _cumsum_both(la, mask.astype(BF16))
    last = 0 if rev else L - 1

    gmats = []
    for g in range(SSD_GROUPS):
        cg = cmat[:, g * SSD_STATE:(g + 1) * SSD_STATE].astype(BF16)
        bg = bmat[:, g * SSD_STATE:(g + 1) * SSD_STATE].astype(BF16)
        gmats.append(lax.dot_general(cg, bg, (((1,), (1,)), ((), ())), preferred_element_type=F32))

    hs = range(heads)
    cis = [(heads if rev else 0) + h for h in hs]
    b_g = [bmat[:, g * SSD_STATE:(g + 1) * SSD_STATE] for g in range(SSD_GROUPS)]
    c_g = [cmat[:, g * SSD_STATE:(g + 1) * SSD_STATE] for g in range(SSD_GROUPS)]
    csc = [cs_col[:, ci:ci + 1] for ci in cis]
    tot = [cc[last:last + 1, :] for cc in csc]
    dec = [jnp.exp(jnp.where(mask, csc[h] - cs_row[cis[h]:cis[h] + 1, :], NEG)) for h in hs]
    xs_h = [xs[:, h * hd:(h + 1) * hd] for h in hs]
    xdt = [(xs_h[h] * dtv[:, cis[h]:cis[h] + 1]).astype(BF16) for h in hs]
    s_old = [s_ref[h] for h in hs]
    y_intra = [jnp.dot((gmats[h // rep] * dec[h]).astype(BF16), xdt[h], preferred_element_type=F32) for h in hs]
    y_inter = [jnp.dot((c_g[h // rep] * jnp.exp(csc[h])).astype(BF16), s_old[h].astype(BF16),
                       preferred_element_type=F32) for h in hs]
    local = [lax.dot_general((b_g[h // rep] * jnp.exp(tot[h] - csc[h])).astype(BF16), xdt[h],
                             (((0,), (0,)), ((), ())), preferred_element_type=F32) for h in hs]
    for h in hs:
        s_ref[h] = s_old[h] * jnp.exp(tot[h]) + local[h]
    ys = [y_intra[h] + y_inter[h] for h in hs]
    if final:
        ys = [ys[h] + dsk_ref[:, h * hd:(h + 1) * hd] * xs_h[h] for h in hs]
    y = jnp.concatenate(ys, axis=-1)
    if final:
        y = y + yb_ref[0]
        z = z_ref[0]
        y = y * (z * jax.nn.sigmoid(z))
        y = y * lax.rsqrt(jnp.mean(y * y, axis=-1, keepdims=True) + EPS) * ng_ref[...]
    y_ref[0] = y

    @pl.when(i == nc - 1)
    def _():
        sfin_ref[0] = s_ref[...]


def _ssd_pass(xbc, dt, cw, cb, dtb, a, dsk, s0, *, rev, final_inputs=None):
    b, t, width = xbc.shape
    heads = s0.shape[1]
    ssd_dim = heads * SSD_HEAD_DIM
    L = min(t, 256)
    nc = t // L
    hb = L // HALO
    final = final_inputs is not None

    def cidx(i):
        return nc - 1 - i if rev else i

    in_specs = [
        pl.BlockSpec((1, L, width), lambda bi, i: (bi, cidx(i), 0)),
        pl.BlockSpec((1, HALO, width), lambda bi, i: (bi, jnp.maximum(cidx(i) * hb - 1, 0), 0)),
        pl.BlockSpec((1, HALO, width), lambda bi, i: (bi, jnp.minimum((cidx(i) + 1) * hb, t // HALO - 1), 0)),
        pl.BlockSpec((1, L, LANE), lambda bi, i: (bi, cidx(i), 0)),
        pl.BlockSpec((8, width), lambda bi, i: (0, 0)),
        pl.BlockSpec((1, width), lambda bi, i: (0, 0)),
        pl.BlockSpec((1, LANE), lambda bi, i: (0, 0)),
        pl.BlockSpec((1, LANE), lambda bi, i: (0, 0)),
        pl.BlockSpec((1, ssd_dim), lambda bi, i: (0, 0)),
        pl.BlockSpec((1, heads, SSD_STATE, SSD_HEAD_DIM), lambda bi, i: (bi, 0, 0, 0)),
    ]
    args = [xbc, xbc, xbc, dt, cw, cb, dtb, a, dsk, s0]
    if final:
        yb, z, ng = final_inputs
        in_specs += [
            pl.BlockSpec((1, L, ssd_dim), lambda bi, i: (bi, cidx(i), 0)),
            pl.BlockSpec((1, L, ssd_dim), lambda bi, i: (bi, cidx(i), 0)),
            pl.BlockSpec((1, ssd_dim), lambda bi, i: (0, 0)),
        ]
        args += [yb, z, ng]
    out_specs = [
        pl.BlockSpec((1, L, ssd_dim), lambda bi, i: (bi, cidx(i), 0)),
        pl.BlockSpec((1, heads, SSD_STATE, SSD_HEAD_DIM), lambda bi, i: (bi, 0, 0, 0)),
    ]
    out_shape = [
        jax.ShapeDtypeStruct((b, t, ssd_dim), F32),
        jax.ShapeDtypeStruct((b, heads, SSD_STATE, SSD_HEAD_DIM), F32),
    ]
    if not final:
        out_specs.append(pl.BlockSpec((1, L, width), lambda bi, i: (bi, cidx(i), 0)))
        out_shape.append(jax.ShapeDtypeStruct((b, t, width), F32))
    return pl.pallas_call(
        functools.partial(_ssd_kernel, chunk=L, rev=rev, final=final, heads=heads),
        grid=(b, nc),
        in_specs=in_specs,
        out_specs=out_specs,
        out_shape=out_shape,
        scratch_shapes=[
            pltpu.VMEM((heads, SSD_STATE, SSD_HEAD_DIM), F32),
            pltpu.VMEM((L + 2 * HALO, width), F32),
        ],
        compiler_params=_params(),
        name="ssd_bwd" if rev else "ssd_fwd",
    )(*args)


def _ssd_stream(z, xbc, dt, conv_w, conv_b, a_log, dt_bias, d_skip, norm_g, s0_f, s0_b):
    heads = s0_f.shape[1]
    cw = jnp.pad(conv_w, ((0, 8 - CONV_W), (0, 0)))
    cb = conv_b[None]
    dtb = jnp.pad(dt_bias.reshape(1, -1), ((0, 0), (0, LANE - 2 * heads)))
    a = jnp.pad(-jnp.exp(a_log).reshape(1, -1), ((0, 0), (0, LANE - 2 * heads)))
    dsk = jnp.repeat(d_skip, SSD_HEAD_DIM)[None]
    yb, s_b, act = _ssd_pass(xbc, dt, cw, cb, dtb, a, dsk, s0_b, rev=True)
    y, s_f = _ssd_pass(act, dt, cw, cb, dtb, a, dsk, s0_f, rev=False, final_inputs=(yb, z, norm_g[None]))
    return y, s_f, s_b


def _group_sum(a, ones_bd):
    hi = a.astype(BF16)
    lo = (a - hi.astype(F32)).astype(BF16)
    return (jnp.dot(hi, ones_bd, preferred_element_type=F32) + jnp.dot(lo, ones_bd, preferred_element_type=F32))


def _mm_bf16(a, b):
    return jnp.dot(a.astype(BF16), b.astype(BF16), preferred_element_type=F32)


def _unit_tri_inverse_many(xms, rev):
    n = xms[0].shape[0]
    row = lax.broadcasted_iota(jnp.int32, (n, n), 0)
    col = lax.broadcasted_iota(jnp.int32, (n, n), 1)
    eye = (row == col).astype(F32)
    ds = None
    m, sh = 1, 0
    while m < n:
        same = (row >> (sh + 1)) == (col >> (sh + 1))
        rbit = (row >> sh) & 1
        cbit = (col >> sh) & 1
        sel = same & ((rbit == 0) & (cbit == 1) if rev else (rbit == 1) & (cbit == 0))
        cs = [jnp.where(sel, x, 0.0) for x in xms]
        if ds is None:
            ds = [eye - c for c in cs]
        elif m < SUBLANES:
            es = [_mm_bf16(c, d) for c, d in zip(cs, ds)]
            ds = [d - _mm_bf16(d, e) for d, e in zip(ds, es)]
        else:
            act = 0 if rev else 1

            def rows(a, which):
                return a.reshape(n // (2 * m), 2, m, n)[:, which].reshape(n // 2, n)

            def merge(keep, new):
                pair = (new, keep) if rev else (keep, new)
                return jnp.stack([p.reshape(n // (2 * m), m, n) for p in pair], axis=1).reshape(n, n)

            es = [_mm_bf16(rows(c, act), d) for c, d in zip(cs, ds)]
            zero = jnp.zeros((n // 2, n), F32)
            ds = [merge(rows(d, 1 - act), rows(d, act) - _mm_bf16(rows(d, act), merge(zero, e)))
                  for d, e in zip(ds, es)]
        m, sh = 2 * m, sh + 1
    return ds


def _gdn_kernel(*refs, block, rev, final, heads):
    if final:
        (qkv_ref, prev_ref, next_ref, ab_ref, cw_ref, dtb_ref, a_ref, s0_ref, ob_ref, gate_ref, ng_ref,
         o_ref, sfin_ref, s_ref, ext_ref) = refs
    else:
        (qkv_ref, prev_ref, next_ref, ab_ref, cw_ref, dtb_ref, a_ref, s0_ref,
         o_ref, sfin_ref, act_ref, s_ref, ext_ref) = refs
    TB = block
    L = block
    hd = GDN_HEAD_DIM
    dim = heads * hd
    i = pl.program_id(1)
    nb = pl.num_programs(1)
    j = nb - 1 - i if rev else i

    @pl.when(i == 0)
    def _():
        s_ref[...] = s0_ref[0]

    ri = lax.broadcasted_iota(jnp.int32, (dim, dim), 0) // hd
    ci_ = lax.broadcasted_iota(jnp.int32, (dim, dim), 1) // hd
    ones_bd = (ri == ci_).astype(BF16)
    if final:
        act = qkv_ref[0]
        q = act[:, :dim]
        k = act[:, dim:2 * dim]
        v = act[:, 2 * dim:]
    else:
        ext_ref[0:HALO, :] = jnp.where(j > 0, prev_ref[0], 0.0)
        ext_ref[HALO:HALO + TB, :] = qkv_ref[0]
        ext_ref[HALO + TB:, :] = jnp.where(j < nb - 1, next_ref[0], 0.0)
        base = HALO - CONV_W // 2
        acc = cw_ref[0:1, :] * ext_ref[base:base + TB, :]
        for kk in range(1, CONV_W):
            acc = acc + cw_ref[kk:kk + 1, :] * ext_ref[base + kk:base + kk + TB, :]
        act = acc * jax.nn.sigmoid(acc)
        q = act[:, :dim]
        k = act[:, dim:2 * dim]
        v = act[:, 2 * dim:]
        q = q * lax.rsqrt(_group_sum(q * q, ones_bd) + EPS) * (hd ** -0.5)
        k = k * lax.rsqrt(_group_sum(k * k, ones_bd) + EPS)
        act_ref[0] = jnp.concatenate([q, k, v], axis=-1)
    ab = ab_ref[0]
    gl = a_ref[...] * jax.nn.softplus(ab + dtb_ref[...])
    beta = jax.nn.sigmoid(ab)

    row = lax.broadcasted_iota(jnp.int32, (L, L), 0)
    col = lax.broadcasted_iota(jnp.int32, (L, L), 1)
    incl = (row <= col) if rev else (row >= col)
    strict = (row < col) if rev else (row > col)
    last = 0 if rev else L - 1
    cs_col, cs_row = _cumsum_both(gl, incl.astype(BF16))

    hs = range(heads)
    cis = [(heads if rev else 0) + h for h in hs]
    qh = [q[:, h * hd:(h + 1) * hd] for h in hs]
    kh = [k[:, h * hd:(h + 1) * hd] for h in hs]
    vh = [v[:, h * hd:(h + 1) * hd] for h in hs]
    csc = [cs_col[:, ci:ci + 1] for ci in cis]
    bcol = [beta[:, 2 * heads + ci:2 * heads + ci + 1] for ci in cis]
    qk_kk = [lax.dot_general(jnp.concatenate([qh[h], kh[h]], axis=0).astype(BF16), kh[h].astype(BF16),
                             (((1,), (1,)), ((), ())), preferred_element_type=F32) for h in hs]
    dec = [jnp.exp(jnp.where(incl, csc[h] - cs_row[cis[h]:cis[h] + 1, :], NEG)) for h in hs]
    attn = [(qk_kk[h][:L] * dec[h]).astype(BF16) for h in hs]
    xm = [jnp.where(strict, bcol[h] * qk_kk[h][L:] * dec[h], 0.0) for h in hs]
    tm = _unit_tri_inverse_many(xm, rev)
    egc = [jnp.exp(cc) for cc in csc]
    uw = [jnp.dot(tm[h].astype(BF16),
                  jnp.concatenate([vh[h] * bcol[h], kh[h] * (bcol[h] * egc[h])], axis=1).astype(BF16),
                  preferred_element_type=F32) for h in hs]
    s_old = [s_ref[h] for h in hs]
    rs = [jnp.dot(jnp.concatenate([uw[h][:, hd:], qh[h] * egc[h]], axis=0).astype(BF16), s_old[h].astype(BF16),
                  preferred_element_type=F32) for h in hs]
    vnb = [(uw[h][:, :hd] - rs[h][:L]).astype(BF16) for h in hs]
    outs = [rs[h][L:] + jnp.dot(attn[h], vnb[h], preferred_element_type=F32) for h in hs]
    for h in hs:
        tot = csc[h][last:last + 1, :]
        kend = (kh[h] * jnp.exp(tot - csc[h])).astype(BF16)
        s_ref[h] = s_old[h] * jnp.exp(tot) + lax.dot_general(kend, vnb[h], (((0,), (0,)), ((), ())),
                                                            preferred_element_type=F32)
    o = jnp.concatenate(outs, axis=-1)
    if final:
        o = o + ob_ref[0]
        ms = _group_sum(o * o, ones_bd) * (1.0 / hd)
        gate = gate_ref[0]
        o = o * lax.rsqrt(ms + EPS) * ng_ref[...] * (gate * jax.nn.sigmoid(gate))
    o_ref[0] = o

    @pl.when(i == nb - 1)
    def _():
        sfin_ref[0] = s_ref[...]


def _gdn_pass(qkv, ab, cw, dtb, a, s0, *, rev, block, final_inputs=None, raster_out=False):
    b, t, width = qkv.shape
    heads = s0.shape[1]
    dim = heads * GDN_HEAD_DIM
    TB = block
    nb = t // TB
    hb = TB // HALO
    final = final_inputs is not None

    def bidx(i):
        return nb - 1 - i if rev else i

    in_specs = [
        pl.BlockSpec((1, TB, width), lambda bi, i: (bi, bidx(i), 0)),
        pl.BlockSpec((1, HALO, width), lambda bi, i: (bi, jnp.maximum(bidx(i) * hb - 1, 0), 0)),
        pl.BlockSpec((1, HALO, width), lambda bi, i: (bi, jnp.minimum((bidx(i) + 1) * hb, t // HALO - 1), 0)),
        pl.BlockSpec((1, TB, LANE), lambda bi, i: (bi, bidx(i), 0)),
        pl.BlockSpec((8, width), lambda bi, i: (0, 0)),
        pl.BlockSpec((1, LANE), lambda bi, i: (0, 0)),
        pl.BlockSpec((1, LANE), lambda bi, i: (0, 0)),
        pl.BlockSpec((1, heads, GDN_HEAD_DIM, GDN_HEAD_DIM), lambda bi, i: (bi, 0, 0, 0)),
    ]
    args = [qkv, qkv, qkv, ab, cw, dtb, a, s0]
    if final:
        ob, gate, ng = final_inputs
        in_specs += [
            pl.BlockSpec((1, TB, dim), lambda bi, i: (bi, bidx(i), 0)),
            pl.BlockSpec((1, TB, dim), lambda bi, i: (bi, bidx(i), 0)),
            pl.BlockSpec((1, dim), lambda bi, i: (0, 0)),
        ]
        args += [ob, gate, ng]
    if raster_out:
        assert nb == GRID_W
        o_spec = pl.BlockSpec((1, TB, dim), lambda bi, i: (bi, 0, bidx(i)))
        o_shape = jax.ShapeDtypeStruct((b, TB, GRID_W * dim), F32)
    else:
        o_spec = pl.BlockSpec((1, TB, dim), lambda bi, i: (bi, bidx(i), 0))
        o_shape = jax.ShapeDtypeStruct((b, t, dim), F32)
    out_specs = [o_spec, pl.BlockSpec((1, heads, GDN_HEAD_DIM, GDN_HEAD_DIM), lambda bi, i: (bi, 0, 0, 0))]
    out_shape = [o_shape, jax.ShapeDtypeStruct((b, heads, GDN_HEAD_DIM, GDN_HEAD_DIM), F32)]
    if not final:
        out_specs.append(pl.BlockSpec((1, TB, width), lambda bi, i: (bi, bidx(i), 0)))
        out_shape.append(jax.ShapeDtypeStruct((b, t, width), F32))
    outs = pl.pallas_call(
        functools.partial(_gdn_kernel, block=TB, rev=rev, final=final, heads=heads),
        grid=(b, nb),
        in_specs=in_specs,
        out_specs=out_specs,
        out_shape=out_shape,
        scratch_shapes=[
            pltpu.VMEM((heads, GDN_HEAD_DIM, GDN_HEAD_DIM), F32),
            pltpu.VMEM((TB + 2 * HALO, width), F32),
        ],
        compiler_params=_params(),
        name="gdn_bwd" if rev else "gdn_fwd",
    )(*args)
    return (outs[0].reshape(b, t, dim),) + tuple(outs[1:])


def _gdn_stream(qkv, gate, ab, conv_w, a_log, dt_bias, norm_g, s0_f, s0_b, *, column_major):
    t = qkv.shape[1]
    heads = s0_f.shape[1]
    cw = jnp.pad(conv_w, ((0, 8 - CONV_W), (0, 0)))
    dtb = jnp.pad(dt_bias.reshape(1, -1), ((0, 0), (0, LANE - 2 * heads)))
    a = jnp.pad(-jnp.exp(a_log).reshape(1, -1), ((0, 0), (0, LANE - 2 * heads)))
    ng = jnp.tile(norm_g, heads)[None]
    rows = t // GRID_W
    fused_raster = column_major and rows <= 256 and rows >= 2 * HALO and rows & (rows - 1) == 0
    block = rows if fused_raster else min(t, 256)
    ob, s_b, act = _gdn_pass(qkv, ab, cw, dtb, a, s0_b, rev=True, block=block)
    o, s_f = _gdn_pass(act, ab, cw, dtb, a, s0_f, rev=False, block=block, final_inputs=(ob, gate, ng),
                       raster_out=fused_raster)
    if column_major and not fused_raster:
        o = _to_raster(o)
    return o, s_f, s_b


def _router_kernel(x_ref, sc_ref, sh_ref, g_ref, rw_ref, h_ref, a_ref):
    x = x_ref[0]
    ms = jnp.mean(x * x, axis=-1, keepdims=True)
    h = x * lax.rsqrt(ms + EPS) * g_ref[...]
    h = h * (1.0 + sc_ref[0]) + sh_ref[0]
    hh, hm, _ = _split3(h)
    h_ref[0] = hh
    rw = rw_ref[...]
    rh = rw.astype(BF16)
    rm = (rw - rh.astype(F32)).astype(BF16)
    nt = (((1,), (1,)), ((), ()))
    lg = (lax.dot_general(rh, hh, nt, preferred_element_type=F32)
          + lax.dot_general(rh, hm, nt, preferred_element_type=F32)
          + lax.dot_general(rm, hh, nt, preferred_element_type=F32))
    ex = jnp.exp(lg - jnp.max(lg, axis=0, keepdims=True))
    aff = ex / jnp.sum(ex, axis=0, keepdims=True)
    for k in range(a_ref.shape[1]):
        a_ref[0, k] = aff[:, k * LANE:(k + 1) * LANE]


def _router(x, scale, shift, g, rw_t):
    b, t, d = x.shape
    e = rw_t.shape[0]
    tm = min(t, 1024)
    return pl.pallas_call(
        _router_kernel,
        grid=(b, t // tm),
        in_specs=[
            pl.BlockSpec((1, tm, d), lambda i, j: (i, j, 0)),
            pl.BlockSpec((1, 1, d), lambda i, j: (i, 0, 0)),
            pl.BlockSpec((1, 1, d), lambda i, j: (i, 0, 0)),
            pl.BlockSpec((1, d), lambda i, j: (0, 0)),
            pl.BlockSpec((e, d), lambda i, j: (0, 0)),
        ],
        out_specs=[
            pl.BlockSpec((1, tm, d), lambda i, j: (i, j, 0)),
            pl.BlockSpec((1, tm // LANE, e, LANE), lambda i, j: (i, j, 0, 0)),
        ],
        out_shape=[
            jax.ShapeDtypeStruct((b, t, d), BF16),
            jax.ShapeDtypeStruct((b, t // LANE, e, LANE), F32),
        ],
        compiler_params=_params(),
        name="router",
    )(x, scale, shift, g, rw_t)


def _token_prefix(m3, ut, ones, lt):
    e, nb, _ = m3.shape
    m2 = m3.reshape(e * nb, LANE).astype(BF16)
    inb = jnp.dot(m2, ut, preferred_element_type=F32).reshape(e, nb, LANE)
    tot = jnp.dot(m2, ones, preferred_element_type=F32).reshape(e, nb, LANE)
    offs = jnp.stack([jnp.dot(lt, tot[i].astype(BF16), preferred_element_type=F32) for i in range(e)], axis=0)
    return inb, tot, offs


def _select_kernel(a_ref, idx_ref, gate_ref, srow_ref, st8_ref, npc_ref, cs_ref, cum_ref, cnt_ref, *, cap):
    a = a_ref[0]
    e_n, nb, _ = a.shape
    bits = lax.bitcast_convert_type(a, I32)

    def radix(i, prefix):
        cand = prefix | jnp.left_shift(jnp.int32(1), 30 - i)
        cnt = jnp.sum(jnp.sum((bits >= cand).astype(F32), axis=2, keepdims=True), axis=1, keepdims=True)
        return jnp.where(cnt >= cap, cand, prefix)

    thr = lax.fori_loop(0, 31, radix, jnp.zeros((e_n, 1, 1), I32))
    li = lax.broadcasted_iota(I32, (LANE, LANE), 0)
    lj = lax.broadcasted_iota(I32, (LANE, LANE), 1)
    ut = (li < lj).astype(BF16)
    ones = jnp.ones((LANE, LANE), BF16)
    bi = lax.broadcasted_iota(I32, (nb, nb), 0)
    bj = lax.broadcasted_iota(I32, (nb, nb), 1)
    lt = (bj < bi).astype(BF16)

    gt = bits > thr
    eq = bits == thr
    n_gt = jnp.sum(jnp.sum(gt.astype(F32), axis=2, keepdims=True), axis=1, keepdims=True)
    tie_in, _, tie_offs = _token_prefix(eq.astype(F32), ut, ones, lt)
    sel = gt | (eq & (tie_in + tie_offs < cap - n_gt))
    pos_in, cnt, offs = _token_prefix(sel.astype(F32), ut, ones, lt)
    cs_ref[...] = jnp.where(sel, pos_in + 1.0, 0.0)
    cum_ref[...] = offs + cnt
    cnt_ref[...] = cnt

    offs_i = offs.astype(I32)
    cnt_i = cnt.astype(I32)
    st8 = (offs_i >> 3) << 3
    npc = jnp.where(cnt_i > 0, (offs_i + cnt_i - st8 + (PIECE - 1)) >> 3, 0)
    rbase = []
    run = jnp.zeros((nb, LANE), I32)
    for i in range(e_n):
        rbase.append(run)
        run = run + PIECE * npc[i]
    rbase = jnp.stack(rbase, axis=0)
    srow_ref[0] = jnp.where(sel, rbase + pos_in.astype(I32) + offs_i - st8, -1)
    st8_ref[0] = st8
    npc_ref[0] = npc

    jrow = lax.broadcasted_iota(I32, (1, cap), 1).astype(F32)
    sub_nb = lax.broadcasted_iota(I32, (nb, cap), 0).astype(F32)
    sub_l = lax.broadcasted_iota(I32, (LANE, cap), 0).astype(F32)
    tn = (((0,), (0,)), ((), ()))

    def compact(ei, carry):
        cum_col = cum_ref[ei][:, 0:1]
        cnt_col = cnt_ref[ei][:, 0:1]
        ge = cum_col <= jrow
        blk_j = jnp.sum(ge.astype(F32), axis=0, keepdims=True)
        offs_j = jnp.sum(jnp.where(ge, cnt_col, 0.0), axis=0, keepdims=True)
        rank1 = jrow - offs_j + 1.0
        g_t = (sub_nb == blk_j).astype(BF16)
        row_t = lax.dot_general(cs_ref[ei].astype(BF16), g_t, tn, preferred_element_type=F32)
        match = row_t == rank1
        lane_j = jnp.sum(jnp.where(match, sub_l, 0.0), axis=0, keepdims=True)
        idx_ref[0, pl.ds(ei, 1), :] = (blk_j * LANE + lane_j).astype(I32)
        parts = _split3(a_ref[0, ei])
        aff_t = sum(lax.dot_general(p, g_t, tn, preferred_element_type=F32) for p in parts)
        gate_ref[0, pl.ds(ei, 1), :] = jnp.sum(jnp.where(match, aff_t, 0.0), axis=0, keepdims=True)
        return carry

    lax.fori_loop(0, e_n, compact, 0)


def _ec_select(aff_em, cap):
    b, e, nb, _ = aff_em.shape
    big = lambda dt: jax.ShapeDtypeStruct((b, e, nb, LANE), dt)
    spec4 = pl.BlockSpec((1, e, nb, LANE), lambda i: (i, 0, 0, 0))
    spec3 = pl.BlockSpec((1, e, cap), lambda i: (i, 0, 0))
    return pl.pallas_call(
        functools.partial(_select_kernel, cap=cap),
        grid=(b,),
        in_specs=[spec4],
        out_specs=[spec3, spec3, spec4, spec4, spec4],
        out_shape=[jax.ShapeDtypeStruct((b, e, cap), I32), jax.ShapeDtypeStruct((b, e, cap), F32),
                   big(I32), big(I32), big(I32)],
        scratch_shapes=[pltpu.VMEM((e, nb, LANE), F32)] * 3,
        compiler_params=pltpu.CompilerParams(dimension_semantics=("arbitrary",), vmem_limit_bytes=VMEM_LIMIT),
        name="ec_select",
    )(aff_em)


def _combine_kernel(st8_sm, npc_sm, ye_hbm, srow_ref, x_ref, g_ref, fn_ref, o_ref, stage, acc_ref, sem,
                    *, final, n_exp):
    b = pl.program_id(0)
    k = pl.program_id(1)
    nb = pl.num_programs(1)
    step = b * nb + k
    nsteps = pl.num_programs(0) * nb
    slot = step % 2

    def piece_copy(bb, e, src_row, sl, dst_row):
        return pltpu.make_async_copy(ye_hbm.at[bb, e, pl.ds(src_row, PIECE), :],
                                     stage.at[sl, pl.ds(dst_row, PIECE), :], sem.at[sl])

    def issue(st, sl):
        bb = st // nb

        def per_e(e, r):
            s8 = st8_sm[st * n_exp + e]
            n = npc_sm[st * n_exp + e]

            def per_p(p, r2):
                piece_copy(bb, e, pl.multiple_of(s8 + PIECE * p, PIECE), sl, pl.multiple_of(r2, PIECE)).start()
                return r2 + PIECE

            return lax.fori_loop(0, n, per_p, r)

        lax.fori_loop(0, n_exp, per_e, 0)

    @pl.when(step == 0)
    def _():
        issue(step, slot)

    @pl.when(step + 1 < nsteps)
    def _():
        issue(step + 1, 1 - slot)

    npieces = lax.fori_loop(0, n_exp, lambda e, s: s + npc_sm[step * n_exp + e], 0)
    rows = npieces * PIECE

    @pl.when(npieces > 0)
    def _():
        pltpu.make_async_copy(ye_hbm.at[0, 0, pl.ds(0, rows), :], stage.at[slot, pl.ds(0, rows), :],
                              sem.at[slot]).wait()
    stage[slot, pl.ds(pl.multiple_of(rows, PIECE), LANE), :] = jnp.zeros((LANE, stage.shape[-1]), F32)

    srow = srow_ref[0, 0]
    acc_ref[...] = jnp.zeros_like(acc_ref)
    riota = lax.broadcasted_iota(I32, (LANE, LANE), 0)
    tn = (((0,), (0,)), ((), ()))

    def chunk(c, carry):
        r0 = pl.multiple_of(c * LANE, LANE)
        rid = riota + r0
        pt = (srow[0:1, :] == rid).astype(F32)
        for e in range(1, n_exp):
            pt = pt + (srow[e:e + 1, :] == rid).astype(F32)
        ptb = pt.astype(BF16)
        st = stage[slot, pl.ds(r0, LANE), :]
        hi = st.astype(BF16)
        lo = (st - hi.astype(F32)).astype(BF16)
        acc_ref[...] += lax.dot_general(jnp.concatenate([ptb, ptb], axis=0), jnp.concatenate([hi, lo], axis=0), tn,
                                        preferred_element_type=F32)
        return carry

    lax.fori_loop(0, (rows + LANE - 1) // LANE, chunk, 0)
    y = x_ref[0] + g_ref[0] * acc_ref[...]
    if final:
        y = y * lax.rsqrt(jnp.mean(y * y, axis=-1, keepdims=True) + EPS) * fn_ref[...]
    o_ref[0] = y


def _ec_combine(ye, srow_bm, st8, npc, x, gate, final_g=None):
    b, t, d = x.shape
    e = ye.shape[1]
    nb = t // LANE
    final = final_g is not None
    fn = final_g if final else jnp.ones((1, d), F32)
    max_rows = e * (LANE + 2 * PIECE) + LANE
    grid_spec = pltpu.PrefetchScalarGridSpec(
        num_scalar_prefetch=2,
        grid=(b, nb),
        in_specs=[
            pl.BlockSpec(memory_space=pl.ANY),
            pl.BlockSpec((1, 1, e, LANE), lambda i, j, *_: (i, j, 0, 0)),
            pl.BlockSpec((1, LANE, d), lambda i, j, *_: (i, j, 0)),
            pl.BlockSpec((1, 1, d), lambda i, j, *_: (i, 0, 0)),
            pl.BlockSpec((1, d), lambda i, j, *_: (0, 0)),
        ],
        out_specs=pl.BlockSpec((1, LANE, d), lambda i, j, *_: (i, j, 0)),
        scratch_shapes=[
            pltpu.VMEM((2, max_rows, d), F32),
            pltpu.VMEM((LANE, d), F32),
            pltpu.SemaphoreType.DMA((2,)),
        ],
    )
    return pl.pallas_call(
        functools.partial(_combine_kernel, final=final, n_exp=e),
        grid_spec=grid_spec,
        out_shape=jax.ShapeDtypeStruct((b, t, d), F32),
        compiler_params=_params(),
        name="ec_combine",
    )(st8, npc, ye, srow_bm, x, gate, fn)


def _expert_choice_block(x, scale, shift, g2, gate2, router_w, ew, final_g):
    b, t, d = x.shape
    e = router_w.shape[-1]
    cap = EC_CAPACITY * t // e
    h2, aff_bm = _router(x, scale, shift, g2, router_w.T)
    idx, gate, srow, st8, npc = _ec_select(jnp.transpose(aff_bm, (0, 2, 1, 3)), cap)
    xe = jax.vmap(lambda hb, ib: hb[ib])(h2, idx)
    ye = _expert_ffn(xe.reshape(b * e, cap, d), gate.reshape(b * e, cap, 1), *ew).reshape(b, e, cap, d)
    srow_bm = jnp.transpose(srow, (0, 2, 1, 3))
    st8_f = jnp.transpose(st8[..., 0], (0, 2, 1)).reshape(-1)
    npc_f = jnp.transpose(npc[..., 0], (0, 2, 1)).reshape(-1)
    return _ec_combine(ye, srow_bm, st8_f, npc_f, x, gate2, final_g)


def _rms_norm(x, g):
    return x * lax.rsqrt(jnp.mean(x * x, axis=-1, keepdims=True) + EPS) * g


def _to_raster(u):
    b, t, c = u.shape
    rows = t // GRID_W
    return u.reshape(b, GRID_W, rows, c).transpose(0, 2, 1, 3).reshape(b, t, c)


def _expert_choice_ffn(h, router_w, ew):
    b, t, d = h.shape
    cap = EC_CAPACITY * t // N_EXPERTS
    aff = jax.nn.softmax(jnp.einsum('btd,de->bte', h, router_w, precision=lax.Precision.HIGHEST), axis=-1)
    gate, idx = lax.top_k(jnp.swapaxes(aff, 1, 2), cap)
    xe = jax.vmap(lambda hb, ib: hb[ib])(h, idx)
    ye = _expert_ffn(xe.reshape(b * N_EXPERTS, cap, d), gate.reshape(b * N_EXPERTS, cap, 1), *ew)
    ye = ye.reshape(b, N_EXPERTS, cap, d)
    return jax.vmap(lambda ib, yb: jnp.zeros((t, d), yb.dtype).at[ib.reshape(-1)].add(yb.reshape(-1, d)))(idx, ye)


def kernel(x, c, ctx, c_ctx, norm1_g, norm2_g, ada_w, ada_b, w_in, w_out, pool_w, pool_scale, ssd_conv_w, ssd_conv_b, ssd_a_log, ssd_dt_bias, ssd_d, ssd_norm_g, gdn_conv_w, gdn_a_log, gdn_dt_bias, gdn_norm_g, router_w, exp_w_gate, exp_w_up, exp_w_down, final_norm_g):
    depth, d, _ = w_in.shape
    b, t, _ = x.shape
    pool_dim = pool_scale.shape[-1]
    ssd_dim = ssd_norm_g.shape[-1]
    ssd_heads = ssd_dim // SSD_HEAD_DIM
    ssd_bc = SSD_GROUPS * SSD_STATE
    gdn_dim = gdn_conv_w.shape[-1] // 3
    gdn_heads = gdn_dim // GDN_HEAD_DIM
    splits = (pool_dim, ssd_dim, ssd_dim + 2 * ssd_bc, 2 * ssd_heads, 3 * gdn_dim, gdn_dim, 2 * gdn_heads,
              2 * gdn_heads)
    cut = [0] + np.cumsum(splits).tolist()
    r_widths = (ssd_dim, ssd_dim + 2 * ssd_bc, pool_dim, LANE)
    g_widths = (3 * gdn_dim, gdn_dim, LANE)

    sc = jax.nn.silu(c)
    scc = jax.nn.silu(c_ctx)[None]
    for l in range(depth):
        last = l == depth - 1
        wl = w_in[l]
        seg = [wl[:, cut[i]:cut[i + 1]] for i in range(8)]
        w_r = jnp.concatenate(
            [seg[1], seg[2], seg[0], jnp.pad(seg[3], ((0, 0), (0, LANE - 2 * ssd_heads)))], axis=1).astype(BF16)
        w_g = jnp.concatenate(
            [seg[4], seg[5], jnp.pad(jnp.concatenate([seg[6], seg[7]], axis=1), ((0, 0), (0, LANE - 4 * gdn_heads)))],
            axis=1).astype(BF16)
        w_o = w_out[l].astype(BF16)
        ew = (exp_w_gate, exp_w_up, exp_w_down, l)
        m_lat = jnp.split(sc @ ada_w[l] + ada_b[l], 6, axis=-1)
        m_ctx = [jnp.broadcast_to(m, (b, d)) for m in jnp.split(scc @ ada_w[l] + ada_b[l], 6, axis=-1)]
        g1 = norm1_g[l][None]

        def project(xx, mm, column_major):
            pr = _inproj(xx, mm[1][:, None], mm[0][:, None], g1, w_r, r_widths, column_major=False)
            pg = _inproj(xx, mm[1][:, None], mm[0][:, None], g1, w_g, g_widths, column_major=column_major)
            return pr, pg

        ssd_p = (ssd_conv_w[l], ssd_conv_b[l], ssd_a_log[l], ssd_dt_bias[l], ssd_d[l], ssd_norm_g[l])
        gdn_p = (gdn_conv_w[l], gdn_a_log[l], gdn_dt_bias[l], gdn_norm_g[l])
        zs = jnp.zeros((b, ssd_heads, SSD_STATE, SSD_HEAD_DIM), F32)
        zg = jnp.zeros((b, gdn_heads, GDN_HEAD_DIM, GDN_HEAD_DIM), F32)

        (c_z, c_xbc, c_pool, c_dt), (c_qkv, c_gate, c_ab) = project(ctx, m_ctx, False)
        (l_z, l_xbc, l_pool, l_dt), (l_qkv, l_gate, l_ab) = project(x, m_lat, True)

        s_ctx, ssd_sf, ssd_sb = _ssd_stream(c_z, c_xbc, c_dt, *ssd_p, zs, zs)
        g_ctx, gdn_sf, gdn_sb = _gdn_stream(c_qkv, c_gate, c_ab, *gdn_p, zg, zg, column_major=False)
        s_lat, _, _ = _ssd_stream(l_z, l_xbc, l_dt, *ssd_p, ssd_sf, ssd_sb)
        g_lat, _, _ = _gdn_stream(l_qkv, l_gate, l_ab, *gdn_p, gdn_sf, gdn_sb, column_major=True)
        x = _outproj([_pool_branch(l_pool, pool_w[l], pool_scale[l]), s_lat, g_lat], x, m_lat[2][:, None], w_o)
        fin = final_norm_g[None] if last else None
        if t % 1024 == 0:
            x = _expert_choice_block(x, m_lat[4][:, None], m_lat[3][:, None], norm2_g[l][None], m_lat[5][:, None],
                                     router_w[l], ew, fin)
        else:
            h2 = _rms_norm(x, norm2_g[l]) * (1 + m_lat[4][:, None]) + m_lat[3][:, None]
            x = x + m_lat[5][:, None] * _expert_choice_ffn(h2, router_w[l], ew)
            if last:
                x = _rms_norm(x, final_norm_g)
        if not last:
            ctx = _outproj([_pool_branch(c_pool, pool_w[l], pool_scale[l]), s_ctx, g_ctx], ctx, m_ctx[2][:, None],
                           w_o)
            h2c = _rms_norm(ctx, norm2_g[l]) * (1 + m_ctx[4][:, None]) + m_ctx[3][:, None]
            ctx = ctx + m_ctx[5][:, None] * _expert_choice_ffn(h2c, router_w[l], ew)
    return x
```

```python
import functools

import numpy as np
import jax
import jax.numpy as jnp
from jax import lax
from jax.experimental import pallas as pl
from jax.experimental.pallas import tpu as pltpu

F32 = jnp.float32
BF16 = jnp.bfloat16

I32 = jnp.int32

GRID_W = 64
CONV_W = 5
POOL_WINDOWS = (2, 4, 8, 16)
SSD_HEAD_DIM = 64
SSD_GROUPS = 2
SSD_STATE = 128
GDN_HEAD_DIM = 64
N_EXPERTS = 16
EC_CAPACITY = 2
EPS = 1e-6

LANE = 128
SUBLANES = 8
VMEM_LIMIT = 48 * 1024 * 1024

ROW_TILE = 512
WIDE_TILE = 1024
SCAN_CHUNK = 256
HALO = SUBLANES
PIECE = SUBLANES
NEG = -1e30


def _params():
    return pltpu.CompilerParams(dimension_semantics=("arbitrary", "arbitrary"), vmem_limit_bytes=VMEM_LIMIT)


def _inproj_kernel(x_ref, sc_ref, sh_ref, g_ref, w_ref, *o_refs, transpose_grid):
    x = x_ref[0]
    tm = x.shape[0]
    ms = jnp.mean(x * x, axis=-1, keepdims=True)
    h = x * lax.rsqrt(ms + EPS) * g_ref[...]
    h = (h * (1.0 + sc_ref[0]) + sh_ref[0]).astype(BF16)
    if transpose_grid:
        p = lax.broadcasted_iota(jnp.int32, (tm, tm), 0)
        q = lax.broadcasted_iota(jnp.int32, (tm, tm), 1)
        perm = (q == (p % SUBLANES) * GRID_W + p // SUBLANES).astype(BF16)
        h = jnp.dot(perm, h, preferred_element_type=F32).astype(BF16)
    y = jnp.dot(h, w_ref[...], preferred_element_type=F32)
    off = 0
    for o_ref in o_refs:
        n = o_ref.shape[-1]
        if transpose_grid:
            o_ref[0] = y[:, off:off + n].reshape(GRID_W, SUBLANES, n)
        else:
            o_ref[0] = y[:, off:off + n]
        off += n


def _inproj(x, scale, shift, g, w, widths, *, column_major):
    b, t, d = x.shape
    rows = t // GRID_W
    if column_major:
        assert rows % SUBLANES == 0
        tm = SUBLANES * GRID_W
        out_specs = [pl.BlockSpec((1, GRID_W, SUBLANES, n), lambda i, j: (i, 0, j, 0)) for n in widths]
        out_shape = [jax.ShapeDtypeStruct((b, GRID_W, rows, n), F32) for n in widths]
    else:
        tm = min(t, WIDE_TILE)
        out_specs = [pl.BlockSpec((1, tm, n), lambda i, j: (i, j, 0)) for n in widths]
        out_shape = [jax.ShapeDtypeStruct((b, t, n), F32) for n in widths]
    outs = pl.pallas_call(
        functools.partial(_inproj_kernel, transpose_grid=column_major),
        grid=(b, t // tm),
        in_specs=[
            pl.BlockSpec((1, tm, d), lambda i, j: (i, j, 0)),
            pl.BlockSpec((1, 1, d), lambda i, j: (i, 0, 0)),
            pl.BlockSpec((1, 1, d), lambda i, j: (i, 0, 0)),
            pl.BlockSpec((1, d), lambda i, j: (0, 0)),
            pl.BlockSpec((d, w.shape[1]), lambda i, j: (0, 0)),
        ],
        out_specs=out_specs,
        out_shape=out_shape,
        compiler_params=_params(),
        name="inproj",
    )(x, scale, shift, g, w)
    return [o.reshape(b, t, n) for o, n in zip(outs, widths)]


def _outproj_kernel(*refs, n_in):
    a_refs = refs[:n_in]
    x_ref, gate_ref, w_ref, o_ref = refs[n_in:]
    a = jnp.concatenate([a_ref[0].astype(BF16) for a_ref in a_refs], axis=-1)
    y = jnp.dot(a, w_ref[...], preferred_element_type=F32)
    o_ref[0] = x_ref[0] + gate_ref[0] * y


def _outproj(parts, x, gate, w):
    b, t, d = x.shape
    tm = min(t, ROW_TILE)
    n_in = len(parts)
    return pl.pallas_call(
        functools.partial(_outproj_kernel, n_in=n_in),
        grid=(b, t // tm),
        in_specs=(
            [pl.BlockSpec((1, tm, p.shape[-1]), lambda i, j: (i, j, 0)) for p in parts]
            + [pl.BlockSpec((1, tm, d), lambda i, j: (i, j, 0)), pl.BlockSpec((1, 1, d), lambda i, j: (i, 0, 0)),
               pl.BlockSpec(w.shape, lambda i, j: (0, 0))]),
        out_specs=pl.BlockSpec((1, tm, d), lambda i, j: (i, j, 0)),
        out_shape=jax.ShapeDtypeStruct((b, t, d), F32),
        compiler_params=_params(),
        name="outproj",
    )(*parts, x, gate, w)


def _pool_kernel(u_ref, prev_ref, next_ref, w_ref, sc_ref, o_ref, e0_ref, e1_ref, *, block, seq, group):
    TM = block
    j = pl.program_id(1)
    nb = pl.num_programs(1)
    u = u_ref[0]
    n_ext = TM + 2 * HALO
    e0_ref[0:HALO, :] = jnp.where(j > 0, prev_ref[0], 0.0)
    e0_ref[HALO:HALO + TM, :] = u
    e0_ref[HALO + TM:, :] = jnp.where(j < nb - 1, next_ref[0], 0.0)
    tok = j * TM + lax.broadcasted_iota(jnp.int32, (TM, 1), 0)
    lane = lax.broadcasted_iota(jnp.int32, (1, u.shape[-1]), 1)
    src, dst = e0_ref, e1_ref
    pooled = jnp.zeros_like(u)
    half = 1
    for gi, win in enumerate(POOL_WINDOWS):
        assert win == 2 * half
        lo_r, hi_r = half, n_ext - half
        if half == 1:
            dst[lo_r:hi_r, :] = src[lo_r - 1:hi_r - 1, :] + src[lo_r:hi_r, :]
        else:
            q = half // 2
            dst[lo_r:hi_r, :] = src[lo_r - q:hi_r - q, :] + src[lo_r + q:hi_r + q, :]
        cnt = (jnp.minimum(tok + half, seq) - jnp.maximum(tok - half, 0)).astype(F32)
        mean = dst[HALO:HALO + TM, :] / cnt
        pooled = jnp.where((lane >= gi * group) & (lane < (gi + 1) * group), mean, pooled)
        src, dst = dst, src
        half *= 2
    dd = pooled - u
    y = jnp.dot(dd.astype(BF16), w_ref[...], preferred_element_type=F32)
    o_ref[0] = y * sc_ref[...]


def _pool_branch(u, pool_w, pool_scale):
    b, t, c = u.shape
    ng, pg, _ = pool_w.shape
    tm = min(t, ROW_TILE)
    hb = tm // HALO
    w_bd = jnp.zeros((c, c), F32)
    for gi in range(ng):
        w_bd = w_bd.at[gi * pg:(gi + 1) * pg, gi * pg:(gi + 1) * pg].set(pool_w[gi])
    return pl.pallas_call(
        functools.partial(_pool_kernel, block=tm, seq=t, group=pg),
        grid=(b, t // tm),
        in_specs=[
            pl.BlockSpec((1, tm, c), lambda i, j: (i, j, 0)),
            pl.BlockSpec((1, HALO, c), lambda i, j: (i, jnp.maximum(j * hb - 1, 0), 0)),
            pl.BlockSpec((1, HALO, c), lambda i, j: (i, jnp.minimum((j + 1) * hb, t // HALO - 1), 0)),
            pl.BlockSpec((c, c), lambda i, j: (0, 0)),
            pl.BlockSpec((1, c), lambda i, j: (0, 0)),
        ],
        out_specs=pl.BlockSpec((1, tm, c), lambda i, j: (i, j, 0)),
        out_shape=jax.ShapeDtypeStruct((b, t, c), F32),
        scratch_shapes=[pltpu.VMEM((tm + 2 * HALO, c), F32), pltpu.VMEM((tm + 2 * HALO, c), F32)],
        compiler_params=_params(),
        name="pool",
    )(u, u, u, w_bd.astype(BF16), pool_scale[None])


def _ffn_kernel(x_ref, gate_ref, wg_ref, wu_ref, wd_ref, o_ref, wgb_ref, wub_ref, wdb_ref):
    @pl.when(pl.program_id(1) == 0)
    def _():
        wgb_ref[...] = wg_ref[0].astype(BF16)
        wub_ref[...] = wu_ref[0].astype(BF16)
        wdb_ref[...] = wd_ref[0].astype(BF16)

    x = x_ref[0].astype(BF16)
    hg = jnp.dot(x, wgb_ref[...], preferred_element_type=F32)
    hu = jnp.dot(x, wub_ref[...], preferred_element_type=F32)
    hid = (hg * jax.nn.sigmoid(hg)) * hu
    y = jnp.dot(hid.astype(BF16), wdb_ref[...], preferred_element_type=F32)
    o_ref[0] = y * gate_ref[0]


def _expert_ffn(xe, gate, wg, wu, wd, layer):
    be, c, d = xe.shape
    _, e, _, f = wg.shape
    tc = min(c, WIDE_TILE)
    return pl.pallas_call(
        _ffn_kernel,
        grid=(be, c // tc),
        in_specs=[
            pl.BlockSpec((1, tc, d), lambda i, j: (i, j, 0)),
            pl.BlockSpec((1, tc, 1), lambda i, j: (i, j, 0)),
            pl.BlockSpec((None, 1, d, f), lambda i, j: (layer, i % e, 0, 0)),
            pl.BlockSpec((None, 1, d, f), lambda i, j: (layer, i % e, 0, 0)),
            pl.BlockSpec((None, 1, f, d), lambda i, j: (layer, i % e, 0, 0)),
        ],
        out_specs=pl.BlockSpec((1, tc, d), lambda i, j: (i, j, 0)),
        out_shape=jax.ShapeDtypeStruct((be, c, d), F32),
        scratch_shapes=[pltpu.VMEM((d, f), BF16), pltpu.VMEM((d, f), BF16), pltpu.VMEM((f, d), BF16)],
        compiler_params=_params(),
        name="expert_ffn",
    )(xe, gate, wg, wu, wd)


def _split3(a):
    hi = a.astype(BF16)
    r1 = a - hi.astype(F32)
    mid = r1.astype(BF16)
    lo = (r1 - mid.astype(F32)).astype(BF16)
    return hi, mid, lo


def _cumsum_both(la, incl):
    parts = _split3(la)
    cs_col = sum(jnp.dot(incl, p, preferred_element_type=F32) for p in parts)
    cs_row = sum(lax.dot_general(p, incl, (((0,), (1,)), ((), ())), preferred_element_type=F32) for p in parts)
    return cs_col, cs_row


def _ssd_kernel(*refs, chunk, rev, final, heads):
    if final:
        (xbc_ref, prev_ref, next_ref, dt_ref, cw_ref, cb_ref, dtb_ref, a_ref, dsk_ref, s0_ref,
         yb_ref, z_ref, ng_ref, y_ref, sfin_ref, s_ref, ext_ref) = refs
    else:
        (xbc_ref, prev_ref, next_ref, dt_ref, cw_ref, cb_ref, dtb_ref, a_ref, dsk_ref, s0_ref,
         y_ref, sfin_ref, act_ref, s_ref, ext_ref) = refs
    L = chunk
    i = pl.program_id(1)
    nc = pl.num_programs(1)
    j = nc - 1 - i if rev else i
    hd = SSD_HEAD_DIM
    ssd_dim = heads * hd
    rep = heads // SSD_GROUPS

    @pl.when(i == 0)
    def _():
        s_ref[...] = s0_ref[0]

    if final:
        act = xbc_ref[0]
    else:
        ext_ref[0:HALO, :] = jnp.where(j > 0, prev_ref[0], 0.0)
        ext_ref[HALO:HALO + L, :] = xbc_ref[0]
        ext_ref[HALO + L:, :] = jnp.where(j < nc - 1, next_ref[0], 0.0)
        base = HALO - CONV_W // 2
        acc = cb_ref[...] + cw_ref[0:1, :] * ext_ref[base:base + L, :]
        for k in range(1, CONV_W):
            acc = acc + cw_ref[k:k + 1, :] * ext_ref[base + k:base + k + L, :]
        act = acc * jax.nn.sigmoid(acc)
        act_ref[0] = act
    xs = act[:, :ssd_dim]
    bmat = act[:, ssd_dim:ssd_dim + SSD_GROUPS * SSD_STATE]
    cmat = act[:, ssd_dim + SSD_GROUPS * SSD_STATE:]

    dtv = jax.nn.softplus(dt_ref[0] + dtb_ref[...])
    la = dtv * a_ref[...]
    row = lax.broadcasted_iota(jnp.int32, (L, L), 0)
    col = lax.broadcasted_iota(jnp.int32, (L, L), 1)
    mask = (row <= col) if rev else (row >= col)
    cs_col, cs_row = _cumsum_both(la, mask.astype(BF16))
    last = 0 if rev else L - 1

    gmats = []
    for g in range(SSD_GROUPS):
        cg = cmat[:, g * SSD_STATE:(g + 1) * SSD_STATE].astype(BF16)
        bg = bmat[:, g * SSD_STATE:(g + 1) * SSD_STATE].astype(BF16)
        gmats.append(lax.dot_general(cg, bg, (((1,), (1,)), ((), ())), preferred_element_type=F32))

    hs = range(heads)
    cis = [(heads if rev else 0) + h for h in hs]
    b_g = [bmat[:, g * SSD_STATE:(g + 1) * SSD_STATE] for g in range(SSD_GROUPS)]
    c_g = [cmat[:, g * SSD_STATE:(g + 1) * SSD_STATE] for g in range(SSD_GROUPS)]
    csc = [cs_col[:, ci:ci + 1] for ci in cis]
    tot = [cc[last:last + 1, :] for cc in csc]
    dec = [jnp.exp(jnp.where(mask, csc[h] - cs_row[cis[h]:cis[h] + 1, :], NEG)) for h in hs]
    xs_h = [xs[:, h * hd:(h + 1) * hd] for h in hs]
    xdt = [(xs_h[h] * dtv[:, cis[h]:cis[h] + 1]).astype(BF16) for h in hs]
    s_old = [s_ref[h] for h in hs]
    y_intra = [jnp.dot((gmats[h // rep] * dec[h]).astype(BF16), xdt[h], preferred_element_type=F32) for h in hs]
    y_inter = [jnp.dot((c_g[h // rep] * jnp.exp(csc[h])).astype(BF16), s_old[h].astype(BF16),
                       preferred_element_type=F32) for h in hs]
    local = [lax.dot_general((b_g[h // rep] * jnp.exp(tot[h] - csc[h])).astype(BF16), xdt[h],
                             (((0,), (0,)), ((), ())), preferred_element_type=F32) for h in hs]
    for h in hs:
        s_ref[h] = s_old[h] * jnp.exp(tot[h]) + local[h]
    ys = [y_intra[h] + y_inter[h] for h in hs]
    if final:
        ys = [ys[h] + dsk_ref[:, h * hd:(h + 1) * hd] * xs_h[h] for h in hs]
    y = jnp.concatenate(ys, axis=-1)
    if final:
        y = y + yb_ref[0]
        z = z_ref[0]
        y = y * (z * jax.nn.sigmoid(z))
        y = y * lax.rsqrt(jnp.mean(y * y, axis=-1, keepdims=True) + EPS) * ng_ref[...]
    y_ref[0] = y

    @pl.when(i == nc - 1)
    def _():
        sfin_ref[0] = s_ref[...]


def _ssd_pass(xbc, dt, cw, cb, dtb, a, dsk, s0, *, rev, final_inputs=None):
    b, t, width = xbc.shape
    heads = s0.shape[1]
    ssd_dim = heads * SSD_HEAD_DIM
    L = min(t, SCAN_CHUNK)
    nc = t // L
    hb = L // HALO
    final = final_inputs is not None

    def cidx(i):
        return nc - 1 - i if rev else i

    in_specs = [
        pl.BlockSpec((1, L, width), lambda bi, i: (bi, cidx(i), 0)),
        pl.BlockSpec((1, HALO, width), lambda bi, i: (bi, jnp.maximum(cidx(i) * hb - 1, 0), 0)),
        pl.BlockSpec((1, HALO, width), lambda bi, i: (bi, jnp.minimum((cidx(i) + 1) * hb, t // HALO - 1), 0)),
        pl.BlockSpec((1, L, LANE), lambda bi, i: (bi, cidx(i), 0)),
        pl.BlockSpec((8, width), lambda bi, i: (0, 0)),
        pl.BlockSpec((1, width), lambda bi, i: (0, 0)),
        pl.BlockSpec((1, LANE), lambda bi, i: (0, 0)),
        pl.BlockSpec((1, LANE), lambda bi, i: (0, 0)),
        pl.BlockSpec((1, ssd_dim), lambda bi, i: (0, 0)),
        pl.BlockSpec((1, heads, SSD_STATE, SSD_HEAD_DIM), lambda bi, i: (bi, 0, 0, 0)),
    ]
    args = [xbc, xbc, xbc, dt, cw, cb, dtb, a, dsk, s0]
    if final:
        yb, z, ng = final_inputs
        in_specs += [
            pl.BlockSpec((1, L, ssd_dim), lambda bi, i: (bi, cidx(i), 0)),
            pl.BlockSpec((1, L, ssd_dim), lambda bi, i: (bi, cidx(i), 0)),
            pl.BlockSpec((1, ssd_dim), lambda bi, i: (0, 0)),
        ]
        args += [yb, z, ng]
    out_specs = [
        pl.BlockSpec((1, L, ssd_dim), lambda bi, i: (bi, cidx(i), 0)),
        pl.BlockSpec((1, heads, SSD_STATE, SSD_HEAD_DIM), lambda bi, i: (bi, 0, 0, 0)),
    ]
    out_shape = [
        jax.ShapeDtypeStruct((b, t, ssd_dim), F32),
        jax.ShapeDtypeStruct((b, heads, SSD_STATE, SSD_HEAD_DIM), F32),
    ]
    if not final:
        out_specs.append(pl.BlockSpec((1, L, width), lambda bi, i: (bi, cidx(i), 0)))
        out_shape.append(jax.ShapeDtypeStruct((b, t, width), F32))
    return pl.pallas_call(
        functools.partial(_ssd_kernel, chunk=L, rev=rev, final=final, heads=heads),
        grid=(b, nc),
        in_specs=in_specs,
        out_specs=out_specs,
        out_shape=out_shape,
        scratch_shapes=[
            pltpu.VMEM((heads, SSD_STATE, SSD_HEAD_DIM), F32),
            pltpu.VMEM((L + 2 * HALO, width), F32),
        ],
        compiler_params=_params(),
        name="ssd_bwd" if rev else "ssd_fwd",
    )(*args)


def _ssd_stream(z, xbc, dt, conv_w, conv_b, a_log, dt_bias, d_skip, norm_g, s0_f, s0_b):
    heads = s0_f.shape[1]
    cw = jnp.pad(conv_w, ((0, 8 - CONV_W), (0, 0)))
    cb = conv_b[None]
    dtb = jnp.pad(dt_bias.reshape(1, -1), ((0, 0), (0, LANE - 2 * heads)))
    a = jnp.pad(-jnp.exp(a_log).reshape(1, -1), ((0, 0), (0, LANE - 2 * heads)))
    dsk = jnp.repeat(d_skip, SSD_HEAD_DIM)[None]
    yb, s_b, act = _ssd_pass(xbc, dt, cw, cb, dtb, a, dsk, s0_b, rev=True)
    y, s_f = _ssd_pass(act, dt, cw, cb, dtb, a, dsk, s0_f, rev=False, final_inputs=(yb, z, norm_g[None]))
    return y, s_f, s_b


def _group_sum(a, ones_bd):
    hi = a.astype(BF16)
    lo = (a - hi.astype(F32)).astype(BF16)
    return (jnp.dot(hi, ones_bd, preferred_element_type=F32) + jnp.dot(lo, ones_bd, preferred_element_type=F32))


def _mm_bf16(a, b):
    return jnp.dot(a.astype(BF16), b.astype(BF16), preferred_element_type=F32)


def _unit_tri_inverse_many(xms, rev):
    n = xms[0].shape[0]
    row = lax.broadcasted_iota(jnp.int32, (n, n), 0)
    col = lax.broadcasted_iota(jnp.int32, (n, n), 1)
    eye = (row == col).astype(F32)
    ds = None
    m, sh = 1, 0
    while m < n:
        same = (row >> (sh + 1)) == (col >> (sh + 1))
        rbit = (row >> sh) & 1
        cbit = (col >> sh) & 1
        sel = same & ((rbit == 0) & (cbit == 1) if rev else (rbit == 1) & (cbit == 0))
        cs = [jnp.where(sel, x, 0.0) for x in xms]
        if ds is None:
            ds = [eye - c for c in cs]
        elif m < SUBLANES:
            es = [_mm_bf16(c, d) for c, d in zip(cs, ds)]
            ds = [d - _mm_bf16(d, e) for d, e in zip(ds, es)]
        else:
            act = 0 if rev else 1

            def rows(a, which):
                return a.reshape(n // (2 * m), 2, m, n)[:, which].reshape(n // 2, n)

            def merge(keep, new):
                pair = (new, keep) if rev else (keep, new)
                return jnp.stack([p.reshape(n // (2 * m), m, n) for p in pair], axis=1).reshape(n, n)

            es = [_mm_bf16(rows(c, act), d) for c, d in zip(cs, ds)]
            zero = jnp.zeros((n // 2, n), F32)
            ds = [merge(rows(d, 1 - act), rows(d, act) - _mm_bf16(rows(d, act), merge(zero, e)))
                  for d, e in zip(ds, es)]
        m, sh = 2 * m, sh + 1
    return ds


def _gdn_kernel(*refs, block, rev, final, heads):
    if final:
        (qkv_ref, prev_ref, next_ref, ab_ref, cw_ref, dtb_ref, a_ref, s0_ref, ob_ref, gate_ref, ng_ref,
         o_ref, sfin_ref, s_ref, ext_ref) = refs
    else:
        (qkv_ref, prev_ref, next_ref, ab_ref, cw_ref, dtb_ref, a_ref, s0_ref,
         o_ref, sfin_ref, act_ref, s_ref, ext_ref) = refs
    TB = block
    L = block
    hd = GDN_HEAD_DIM
    dim = heads * hd
    i = pl.program_id(1)
    nb = pl.num_programs(1)
    j = nb - 1 - i if rev else i

    @pl.when(i == 0)
    def _():
        s_ref[...] = s0_ref[0]

    ri = lax.broadcasted_iota(jnp.int32, (dim, dim), 0) // hd
    ci_ = lax.broadcasted_iota(jnp.int32, (dim, dim), 1) // hd
    ones_bd = (ri == ci_).astype(BF16)
    if final:
        act = qkv_ref[0]
        q = act[:, :dim]
        k = act[:, dim:2 * dim]
        v = act[:, 2 * dim:]
    else:
        ext_ref[0:HALO, :] = jnp.where(j > 0, prev_ref[0], 0.0)
        ext_ref[HALO:HALO + TB, :] = qkv_ref[0]
        ext_ref[HALO + TB:, :] = jnp.where(j < nb - 1, next_ref[0], 0.0)
        base = HALO - CONV_W // 2
        acc = cw_ref[0:1, :] * ext_ref[base:base + TB, :]
        for kk in range(1, CONV_W):
            acc = acc + cw_ref[kk:kk + 1, :] * ext_ref[base + kk:base + kk + TB, :]
        act = acc * jax.nn.sigmoid(acc)
        q = act[:, :dim]
        k = act[:, dim:2 * dim]
        v = act[:, 2 * dim:]
        q = q * lax.rsqrt(_group_sum(q * q, ones_bd) + EPS) * (hd ** -0.5)
        k = k * lax.rsqrt(_group_sum(k * k, ones_bd) + EPS)
        act_ref[0] = jnp.concatenate([q, k, v], axis=-1)
    ab = ab_ref[0]
    gl = a_ref[...] * jax.nn.softplus(ab + dtb_ref[...])
    beta = jax.nn.sigmoid(ab)

    row = lax.broadcasted_iota(jnp.int32, (L, L), 0)
    col = lax.broadcasted_iota(jnp.int32, (L, L), 1)
    incl = (row <= col) if rev else (row >= col)
    strict = (row < col) if rev else (row > col)
    last = 0 if rev else L - 1
    cs_col, cs_row = _cumsum_both(gl, incl.astype(BF16))

    hs = range(heads)
    cis = [(heads if rev else 0) + h for h in hs]
    qh = [q[:, h * hd:(h + 1) * hd] for h in hs]
    kh = [k[:, h * hd:(h + 1) * hd] for h in hs]
    vh = [v[:, h * hd:(h + 1) * hd] for h in hs]
    csc = [cs_col[:, ci:ci + 1] for ci in cis]
    bcol = [beta[:, 2 * heads + ci:2 * heads + ci + 1] for ci in cis]
    qk_kk = [lax.dot_general(jnp.concatenate([qh[h], kh[h]], axis=0).astype(BF16), kh[h].astype(BF16),
                             (((1,), (1,)), ((), ())), preferred_element_type=F32) for h in hs]
    dec = [jnp.exp(jnp.where(incl, csc[h] - cs_row[cis[h]:cis[h] + 1, :], NEG)) for h in hs]
    attn = [(qk_kk[h][:L] * dec[h]).astype(BF16) for h in hs]
    xm = [jnp.where(strict, bcol[h] * qk_kk[h][L:] * dec[h], 0.0) for h in hs]
    tm = _unit_tri_inverse_many(xm, rev)
    egc = [jnp.exp(cc) for cc in csc]
    uw = [jnp.dot(tm[h].astype(BF16),
                  jnp.concatenate([vh[h] * bcol[h], kh[h] * (bcol[h] * egc[h])], axis=1).astype(BF16),
                  preferred_element_type=F32) for h in hs]
    s_old = [s_ref[h] for h in hs]
    rs = [jnp.dot(jnp.concatenate([uw[h][:, hd:], qh[h] * egc[h]], axis=0).astype(BF16), s_old[h].astype(BF16),
                  preferred_element_type=F32) for h in hs]
    vnb = [(uw[h][:, :hd] - rs[h][:L]).astype(BF16) for h in hs]
    outs = [rs[h][L:] + jnp.dot(attn[h], vnb[h], preferred_element_type=F32) for h in hs]
    for h in hs:
        tot = csc[h][last:last + 1, :]
        kend = (kh[h] * jnp.exp(tot - csc[h])).astype(BF16)
        s_ref[h] = s_old[h] * jnp.exp(tot) + lax.dot_general(kend, vnb[h], (((0,), (0,)), ((), ())),
                                                            preferred_element_type=F32)
    o = jnp.concatenate(outs, axis=-1)
    if final:
        o = o + ob_ref[0]
        ms = _group_sum(o * o, ones_bd) * (1.0 / hd)
        gate = gate_ref[0]
        o = o * lax.rsqrt(ms + EPS) * ng_ref[...] * (gate * jax.nn.sigmoid(gate))
    o_ref[0] = o

    @pl.when(i == nb - 1)
    def _():
        sfin_ref[0] = s_ref[...]


def _gdn_pass(qkv, ab, cw, dtb, a, s0, *, rev, block, final_inputs=None, raster_out=False):
    b, t, width = qkv.shape
    heads = s0.shape[1]
    dim = heads * GDN_HEAD_DIM
    TB = block
    nb = t // TB
    hb = TB // HALO
    final = final_inputs is not None

    def bidx(i):
        return nb - 1 - i if rev else i

    in_specs = [
        pl.BlockSpec((1, TB, width), lambda bi, i: (bi, bidx(i), 0)),
        pl.BlockSpec((1, HALO, width), lambda bi, i: (bi, jnp.maximum(bidx(i) * hb - 1, 0), 0)),
        pl.BlockSpec((1, HALO, width), lambda bi, i: (bi, jnp.minimum((bidx(i) + 1) * hb, t // HALO - 1), 0)),
        pl.BlockSpec((1, TB, LANE), lambda bi, i: (bi, bidx(i), 0)),
        pl.BlockSpec((8, width), lambda bi, i: (0, 0)),
        pl.BlockSpec((1, LANE), lambda bi, i: (0, 0)),
        pl.BlockSpec((1, LANE), lambda bi, i: (0, 0)),
        pl.BlockSpec((1, heads, GDN_HEAD_DIM, GDN_HEAD_DIM), lambda bi, i: (bi, 0, 0, 0)),
    ]
    args = [qkv, qkv, qkv, ab, cw, dtb, a, s0]
    if final:
        ob, gate, ng = final_inputs
        in_specs += [
            pl.BlockSpec((1, TB, dim), lambda bi, i: (bi, bidx(i), 0)),
            pl.BlockSpec((1, TB, dim), lambda bi, i: (bi, bidx(i), 0)),
            pl.BlockSpec((1, dim), lambda bi, i: (0, 0)),
        ]
        args += [ob, gate, ng]
    if raster_out:
        assert nb == GRID_W
        o_spec = pl.BlockSpec((1, TB, dim), lambda bi, i: (bi, 0, bidx(i)))
        o_shape = jax.ShapeDtypeStruct((b, TB, GRID_W * dim), F32)
    else:
        o_spec = pl.BlockSpec((1, TB, dim), lambda bi, i: (bi, bidx(i), 0))
        o_shape = jax.ShapeDtypeStruct((b, t, dim), F32)
    out_specs = [o_spec, pl.BlockSpec((1, heads, GDN_HEAD_DIM, GDN_HEAD_DIM), lambda bi, i: (bi, 0, 0, 0))]
    out_shape = [o_shape, jax.ShapeDtypeStruct((b, heads, GDN_HEAD_DIM, GDN_HEAD_DIM), F32)]
    if not final:
        out_specs.append(pl.BlockSpec((1, TB, width), lambda bi, i: (bi, bidx(i), 0)))
        out_shape.append(jax.ShapeDtypeStruct((b, t, width), F32))
    outs = pl.pallas_call(
        functools.partial(_gdn_kernel, block=TB, rev=rev, final=final, heads=heads),
        grid=(b, nb),
        in_specs=in_specs,
        out_specs=out_specs,
        out_shape=out_shape,
        scratch_shapes=[
            pltpu.VMEM((heads, GDN_HEAD_DIM, GDN_HEAD_DIM), F32),
            pltpu.VMEM((TB + 2 * HALO, width), F32),
        ],
        compiler_params=_params(),
        name="gdn_bwd" if rev else "gdn_fwd",
    )(*args)
    return (outs[0].reshape(b, t, dim),) + tuple(outs[1:])


def _gdn_stream(qkv, gate, ab, conv_w, a_log, dt_bias, norm_g, s0_f, s0_b, *, column_major):
    t = qkv.shape[1]
    heads = s0_f.shape[1]
    cw = jnp.pad(conv_w, ((0, 8 - CONV_W), (0, 0)))
    dtb = jnp.pad(dt_bias.reshape(1, -1), ((0, 0), (0, LANE - 2 * heads)))
    a = jnp.pad(-jnp.exp(a_log).reshape(1, -1), ((0, 0), (0, LANE - 2 * heads)))
    ng = jnp.tile(norm_g, heads)[None]
    rows = t // GRID_W
    fused_raster = column_major and rows <= SCAN_CHUNK and rows >= 2 * HALO and rows & (rows - 1) == 0
    block = rows if fused_raster else min(t, SCAN_CHUNK)
    ob, s_b, act = _gdn_pass(qkv, ab, cw, dtb, a, s0_b, rev=True, block=block)
    o, s_f = _gdn_pass(act, ab, cw, dtb, a, s0_f, rev=False, block=block, final_inputs=(ob, gate, ng),
                       raster_out=fused_raster)
    if column_major and not fused_raster:
        o = _to_raster(o)
    return o, s_f, s_b


def _router_kernel(x_ref, sc_ref, sh_ref, g_ref, rw_ref, h_ref, a_ref):
    x = x_ref[0]
    ms = jnp.mean(x * x, axis=-1, keepdims=True)
    h = x * lax.rsqrt(ms + EPS) * g_ref[...]
    h = h * (1.0 + sc_ref[0]) + sh_ref[0]
    hh, hm, _ = _split3(h)
    h_ref[0] = hh
    rw = rw_ref[...]
    rh = rw.astype(BF16)
    rm = (rw - rh.astype(F32)).astype(BF16)
    nt = (((1,), (1,)), ((), ()))
    lg = (lax.dot_general(rh, hh, nt, preferred_element_type=F32)
          + lax.dot_general(rh, hm, nt, preferred_element_type=F32)
          + lax.dot_general(rm, hh, nt, preferred_element_type=F32))
    ex = jnp.exp(lg - jnp.max(lg, axis=0, keepdims=True))
    aff = ex / jnp.sum(ex, axis=0, keepdims=True)
    for k in range(a_ref.shape[1]):
        a_ref[0, k] = aff[:, k * LANE:(k + 1) * LANE]


def _router(x, scale, shift, g, rw_t):
    b, t, d = x.shape
    e = rw_t.shape[0]
    tm = min(t, WIDE_TILE)
    return pl.pallas_call(
        _router_kernel,
        grid=(b, t // tm),
        in_specs=[
            pl.BlockSpec((1, tm, d), lambda i, j: (i, j, 0)),
            pl.BlockSpec((1, 1, d), lambda i, j: (i, 0, 0)),
            pl.BlockSpec((1, 1, d), lambda i, j: (i, 0, 0)),
            pl.BlockSpec((1, d), lambda i, j: (0, 0)),
            pl.BlockSpec((e, d), lambda i, j: (0, 0)),
        ],
        out_specs=[
            pl.BlockSpec((1, tm, d), lambda i, j: (i, j, 0)),
            pl.BlockSpec((1, tm // LANE, e, LANE), lambda i, j: (i, j, 0, 0)),
        ],
        out_shape=[
            jax.ShapeDtypeStruct((b, t, d), BF16),
            jax.ShapeDtypeStruct((b, t // LANE, e, LANE), F32),
        ],
        compiler_params=_params(),
        name="router",
    )(x, scale, shift, g, rw_t)


def _token_prefix(m3, ut, ones, lt):
    e, nb, _ = m3.shape
    m2 = m3.reshape(e * nb, LANE).astype(BF16)
    inb = jnp.dot(m2, ut, preferred_element_type=F32).reshape(e, nb, LANE)
    tot = jnp.dot(m2, ones, preferred_element_type=F32).reshape(e, nb, LANE)
    offs = jnp.stack([jnp.dot(lt, tot[i].astype(BF16), preferred_element_type=F32) for i in range(e)], axis=0)
    return inb, tot, offs


def _select_kernel(a_ref, idx_ref, gate_ref, srow_ref, st8_ref, npc_ref, cs_ref, cum_ref, cnt_ref, *, cap):
    a = a_ref[0]
    e_n, nb, _ = a.shape
    bits = lax.bitcast_convert_type(a, I32)

    def radix(i, prefix):
        cand = prefix | jnp.left_shift(jnp.int32(1), 30 - i)
        cnt = jnp.sum(jnp.sum((bits >= cand).astype(F32), axis=2, keepdims=True), axis=1, keepdims=True)
        return jnp.where(cnt >= cap, cand, prefix)

    thr = lax.fori_loop(0, 31, radix, jnp.zeros((e_n, 1, 1), I32))
    li = lax.broadcasted_iota(I32, (LANE, LANE), 0)
    lj = lax.broadcasted_iota(I32, (LANE, LANE), 1)
    ut = (li < lj).astype(BF16)
    ones = jnp.ones((LANE, LANE), BF16)
    bi = lax.broadcasted_iota(I32, (nb, nb), 0)
    bj = lax.broadcasted_iota(I32, (nb, nb), 1)
    lt = (bj < bi).astype(BF16)

    gt = bits > thr
    eq = bits == thr
    n_gt = jnp.sum(jnp.sum(gt.astype(F32), axis=2, keepdims=True), axis=1, keepdims=True)
    tie_in, _, tie_offs = _token_prefix(eq.astype(F32), ut, ones, lt)
    sel = gt | (eq & (tie_in + tie_offs < cap - n_gt))
    pos_in, cnt, offs = _token_prefix(sel.astype(F32), ut, ones, lt)
    cs_ref[...] = jnp.where(sel, pos_in + 1.0, 0.0)
    cum_ref[...] = offs + cnt
    cnt_ref[...] = cnt

    offs_i = offs.astype(I32)
    cnt_i = cnt.astype(I32)
    st8 = (offs_i >> 3) << 3
    npc = jnp.where(cnt_i > 0, (offs_i + cnt_i - st8 + (PIECE - 1)) >> 3, 0)
    rbase = []
    run = jnp.zeros((nb, LANE), I32)
    for i in range(e_n):
        rbase.append(run)
        run = run + PIECE * npc[i]
    rbase = jnp.stack(rbase, axis=0)
    srow_ref[0] = jnp.where(sel, rbase + pos_in.astype(I32) + offs_i - st8, -1)
    st8_ref[0] = st8
    npc_ref[0] = npc

    jrow = lax.broadcasted_iota(I32, (1, cap), 1).astype(F32)
    sub_nb = lax.broadcasted_iota(I32, (nb, cap), 0).astype(F32)
    sub_l = lax.broadcasted_iota(I32, (LANE, cap), 0).astype(F32)
    tn = (((0,), (0,)), ((), ()))

    def compact(ei, carry):
        cum_col = cum_ref[ei][:, 0:1]
        cnt_col = cnt_ref[ei][:, 0:1]
        ge = cum_col <= jrow
        blk_j = jnp.sum(ge.astype(F32), axis=0, keepdims=True)
        offs_j = jnp.sum(jnp.where(ge, cnt_col, 0.0), axis=0, keepdims=True)
        rank1 = jrow - offs_j + 1.0
        g_t = (sub_nb == blk_j).astype(BF16)
        row_t = lax.dot_general(cs_ref[ei].astype(BF16), g_t, tn, preferred_element_type=F32)
        match = row_t == rank1
        lane_j = jnp.sum(jnp.where(match, sub_l, 0.0), axis=0, keepdims=True)
        idx_ref[0, pl.ds(ei, 1), :] = (blk_j * LANE + lane_j).astype(I32)
        parts = _split3(a_ref[0, ei])
        aff_t = sum(lax.dot_general(p, g_t, tn, preferred_element_type=F32) for p in parts)
        gate_ref[0, pl.ds(ei, 1), :] = jnp.sum(jnp.where(match, aff_t, 0.0), axis=0, keepdims=True)
        return carry

    lax.fori_loop(0, e_n, compact, 0)


def _ec_select(aff_em, cap):
    b, e, nb, _ = aff_em.shape
    big = lambda dt: jax.ShapeDtypeStruct((b, e, nb, LANE), dt)
    spec4 = pl.BlockSpec((1, e, nb, LANE), lambda i: (i, 0, 0, 0))
    spec3 = pl.BlockSpec((1, e, cap), lambda i: (i, 0, 0))
    return pl.pallas_call(
        functools.partial(_select_kernel, cap=cap),
        grid=(b,),
        in_specs=[spec4],
        out_specs=[spec3, spec3, spec4, spec4, spec4],
        out_shape=[jax.ShapeDtypeStruct((b, e, cap), I32), jax.ShapeDtypeStruct((b, e, cap), F32),
                   big(I32), big(I32), big(I32)],
        scratch_shapes=[pltpu.VMEM((e, nb, LANE), F32)] * 3,
        compiler_params=pltpu.CompilerParams(dimension_semantics=("arbitrary",), vmem_limit_bytes=VMEM_LIMIT),
        name="ec_select",
    )(aff_em)


def _combine_kernel(st8_sm, npc_sm, ye_hbm, srow_ref, x_ref, g_ref, fn_ref, o_ref, stage, acc_ref, sem,
                    *, final, n_exp):
    b = pl.program_id(0)
    k = pl.program_id(1)
    nb = pl.num_programs(1)
    step = b * nb + k
    nsteps = pl.num_programs(0) * nb
    slot = step % 2

    def piece_copy(bb, e, src_row, sl, dst_row):
        return pltpu.make_async_copy(ye_hbm.at[bb, e, pl.ds(src_row, PIECE), :],
                                     stage.at[sl, pl.ds(dst_row, PIECE), :], sem.at[sl])

    def issue(st, sl):
        bb = st // nb

        def per_e(e, r):
            s8 = st8_sm[st * n_exp + e]
            n = npc_sm[st * n_exp + e]

            def per_p(p, r2):
                piece_copy(bb, e, pl.multiple_of(s8 + PIECE * p, PIECE), sl, pl.multiple_of(r2, PIECE)).start()
                return r2 + PIECE

            return lax.fori_loop(0, n, per_p, r)

        lax.fori_loop(0, n_exp, per_e, 0)

    @pl.when(step == 0)
    def _():
        issue(step, slot)

    @pl.when(step + 1 < nsteps)
    def _():
        issue(step + 1, 1 - slot)

    npieces = lax.fori_loop(0, n_exp, lambda e, s: s + npc_sm[step * n_exp + e], 0)
    rows = npieces * PIECE

    @pl.when(npieces > 0)
    def _():
        pltpu.make_async_copy(ye_hbm.at[0, 0, pl.ds(0, rows), :], stage.at[slot, pl.ds(0, rows), :],
                              sem.at[slot]).wait()
    stage[slot, pl.ds(pl.multiple_of(rows, PIECE), LANE), :] = jnp.zeros((LANE, stage.shape[-1]), F32)

    srow = srow_ref[0, 0]
    acc_ref[...] = jnp.zeros_like(acc_ref)
    riota = lax.broadcasted_iota(I32, (LANE, LANE), 0)
    tn = (((0,), (0,)), ((), ()))

    def chunk(c, carry):
        r0 = pl.multiple_of(c * LANE, LANE)
        rid = riota + r0
        pt = (srow[0:1, :] == rid).astype(F32)
        for e in range(1, n_exp):
            pt = pt + (srow[e:e + 1, :] == rid).astype(F32)
        ptb = pt.astype(BF16)
        st = stage[slot, pl.ds(r0, LANE), :]
        hi = st.astype(BF16)
        lo = (st - hi.astype(F32)).astype(BF16)
        acc_ref[...] += lax.dot_general(jnp.concatenate([ptb, ptb], axis=0), jnp.concatenate([hi, lo], axis=0), tn,
                                        preferred_element_type=F32)
        return carry

    lax.fori_loop(0, (rows + LANE - 1) // LANE, chunk, 0)
    y = x_ref[0] + g_ref[0] * acc_ref[...]
    if final:
        y = y * lax.rsqrt(jnp.mean(y * y, axis=-1, keepdims=True) + EPS) * fn_ref[...]
    o_ref[0] = y


def _ec_combine(ye, srow_bm, st8, npc, x, gate, final_g=None):
    b, t, d = x.shape
    e = ye.shape[1]
    nb = t // LANE
    final = final_g is not None
    fn = final_g if final else jnp.ones((1, d), F32)
    max_rows = e * (LANE + 2 * PIECE) + LANE
    grid_spec = pltpu.PrefetchScalarGridSpec(
        num_scalar_prefetch=2,
        grid=(b, nb),
        in_specs=[
            pl.BlockSpec(memory_space=pl.ANY),
            pl.BlockSpec((1, 1, e, LANE), lambda i, j, *_: (i, j, 0, 0)),
            pl.BlockSpec((1, LANE, d), lambda i, j, *_: (i, j, 0)),
            pl.BlockSpec((1, 1, d), lambda i, j, *_: (i, 0, 0)),
            pl.BlockSpec((1, d), lambda i, j, *_: (0, 0)),
        ],
        out_specs=pl.BlockSpec((1, LANE, d), lambda i, j, *_: (i, j, 0)),
        scratch_shapes=[
            pltpu.VMEM((2, max_rows, d), F32),
            pltpu.VMEM((LANE, d), F32),
            pltpu.SemaphoreType.DMA((2,)),
        ],
    )
    return pl.pallas_call(
        functools.partial(_combine_kernel, final=final, n_exp=e),
        grid_spec=grid_spec,
        out_shape=jax.ShapeDtypeStruct((b, t, d), F32),
        compiler_params=_params(),
        name="ec_combine",
    )(st8, npc, ye, srow_bm, x, gate, fn)


def _expert_choice_block(x, scale, shift, g2, gate2, router_w, ew, final_g):
    b, t, d = x.shape
    e = router_w.shape[-1]
    cap = EC_CAPACITY * t // e
    h2, aff_bm = _router(x, scale, shift, g2, router_w.T)
    idx, gate, srow, st8, npc = _ec_select(jnp.transpose(aff_bm, (0, 2, 1, 3)), cap)
    xe = jax.vmap(lambda hb, ib: hb[ib])(h2, idx)
    ye = _expert_ffn(xe.reshape(b * e, cap, d), gate.reshape(b * e, cap, 1), *ew).reshape(b, e, cap, d)
    srow_bm = jnp.transpose(srow, (0, 2, 1, 3))
    st8_f = jnp.transpose(st8[..., 0], (0, 2, 1)).reshape(-1)
    npc_f = jnp.transpose(npc[..., 0], (0, 2, 1)).reshape(-1)
    return _ec_combine(ye, srow_bm, st8_f, npc_f, x, gate2, final_g)


def _rms_norm(x, g):
    return x * lax.rsqrt(jnp.mean(x * x, axis=-1, keepdims=True) + EPS) * g


def _to_raster(u):
    b, t, c = u.shape
    rows = t // GRID_W
    return u.reshape(b, GRID_W, rows, c).transpose(0, 2, 1, 3).reshape(b, t, c)


def _expert_choice_ffn(h, router_w, ew):
    b, t, d = h.shape
    cap = EC_CAPACITY * t // N_EXPERTS
    aff = jax.nn.softmax(jnp.einsum('btd,de->bte', h, router_w, precision=lax.Precision.HIGHEST), axis=-1)
    gate, idx = lax.top_k(jnp.swapaxes(aff, 1, 2), cap)
    xe = jax.vmap(lambda hb, ib: hb[ib])(h, idx)
    ye = _expert_ffn(xe.reshape(b * N_EXPERTS, cap, d), gate.reshape(b * N_EXPERTS, cap, 1), *ew)
    ye = ye.reshape(b, N_EXPERTS, cap, d)
    return jax.vmap(lambda ib, yb: jnp.zeros((t, d), yb.dtype).at[ib.reshape(-1)].add(yb.reshape(-1, d)))(idx, ye)


def kernel(x, c, ctx, c_ctx, norm1_g, norm2_g, ada_w, ada_b, w_in, w_out, pool_w, pool_scale, ssd_conv_w, ssd_conv_b, ssd_a_log, ssd_dt_bias, ssd_d, ssd_norm_g, gdn_conv_w, gdn_a_log, gdn_dt_bias, gdn_norm_g, router_w, exp_w_gate, exp_w_up, exp_w_down, final_norm_g):
    depth, d, _ = w_in.shape
    b, t, _ = x.shape
    pool_dim = pool_scale.shape[-1]
    ssd_dim = ssd_norm_g.shape[-1]
    ssd_heads = ssd_dim // SSD_HEAD_DIM
    ssd_bc = SSD_GROUPS * SSD_STATE
    gdn_dim = gdn_conv_w.shape[-1] // 3
    gdn_heads = gdn_dim // GDN_HEAD_DIM
    splits = (pool_dim, ssd_dim, ssd_dim + 2 * ssd_bc, 2 * ssd_heads, 3 * gdn_dim, gdn_dim, 2 * gdn_heads,
              2 * gdn_heads)
    cut = [0] + np.cumsum(splits).tolist()
    r_widths = (ssd_dim, ssd_dim + 2 * ssd_bc, pool_dim, LANE)
    g_widths = (3 * gdn_dim, gdn_dim, LANE)

    assert depth >= 1
    sc = jax.nn.silu(c)
    scc = jax.nn.silu(c_ctx)[None]
    for l in range(depth):
        last = l == depth - 1
        wl = w_in[l]
        seg = [wl[:, cut[i]:cut[i + 1]] for i in range(8)]
        w_r = jnp.concatenate(
            [seg[1], seg[2], seg[0], jnp.pad(seg[3], ((0, 0), (0, LANE - 2 * ssd_heads)))], axis=1).astype(BF16)
        w_g = jnp.concatenate(
            [seg[4], seg[5], jnp.pad(jnp.concatenate([seg[6], seg[7]], axis=1), ((0, 0), (0, LANE - 4 * gdn_heads)))],
            axis=1).astype(BF16)
        w_o = w_out[l].astype(BF16)
        ew = (exp_w_gate, exp_w_up, exp_w_down, l)
        m_lat = jnp.split(sc @ ada_w[l] + ada_b[l], 6, axis=-1)
        m_ctx = [jnp.broadcast_to(m, (b, d)) for m in jnp.split(scc @ ada_w[l] + ada_b[l], 6, axis=-1)]
        g1 = norm1_g[l][None]

        def project(xx, mm, column_major):
            pr = _inproj(xx, mm[1][:, None], mm[0][:, None], g1, w_r, r_widths, column_major=False)
            pg = _inproj(xx, mm[1][:, None], mm[0][:, None], g1, w_g, g_widths, column_major=column_major)
            return pr, pg

        ssd_p = (ssd_conv_w[l], ssd_conv_b[l], ssd_a_log[l], ssd_dt_bias[l], ssd_d[l], ssd_norm_g[l])
        gdn_p = (gdn_conv_w[l], gdn_a_log[l], gdn_dt_bias[l], gdn_norm_g[l])
        zs = jnp.zeros((b, ssd_heads, SSD_STATE, SSD_HEAD_DIM), F32)
        zg = jnp.zeros((b, gdn_heads, GDN_HEAD_DIM, GDN_HEAD_DIM), F32)

        (c_z, c_xbc, c_pool, c_dt), (c_qkv, c_gate, c_ab) = project(ctx, m_ctx, False)
        (l_z, l_xbc, l_pool, l_dt), (l_qkv, l_gate, l_ab) = project(x, m_lat, True)

        s_ctx, ssd_sf, ssd_sb = _ssd_stream(c_z, c_xbc, c_dt, *ssd_p, zs, zs)
        g_ctx, gdn_sf, gdn_sb = _gdn_stream(c_qkv, c_gate, c_ab, *gdn_p, zg, zg, column_major=False)
        s_lat, _, _ = _ssd_stream(l_z, l_xbc, l_dt, *ssd_p, ssd_sf, ssd_sb)
        g_lat, _, _ = _gdn_stream(l_qkv, l_gate, l_ab, *gdn_p, gdn_sf, gdn_sb, column_major=True)
        x = _outproj([_pool_branch(l_pool, pool_w[l], pool_scale[l]), s_lat, g_lat], x, m_lat[2][:, None], w_o)
        fin = final_norm_g[None] if last else None
        if t % WIDE_TILE == 0:
            x = _expert_choice_block(x, m_lat[4][:, None], m_lat[3][:, None], norm2_g[l][None], m_lat[5][:, None],
                                     router_w[l], ew, fin)
        else:
            h2 = _rms_norm(x, norm2_g[l]) * (1 + m_lat[4][:, None]) + m_lat[3][:, None]
            x = x + m_lat[5][:, None] * _expert_choice_ffn(h2, router_w[l], ew)
            if last:
                x = _rms_norm(x, final_norm_g)
        if not last:
            ctx = _outproj([_pool_branch(c_pool, pool_w[l], pool_scale[l]), s_ctx, g_ctx], ctx, m_ctx[2][:, None],
                           w_o)
            h2c = _rms_norm(ctx, norm2_g[l]) * (1 + m_ctx[4][:, None]) + m_ctx[3][:, None]
            ctx = ctx + m_ctx[5][:, None] * _expert_choice_ffn(h2c, router_w[l], ew)
    return x
```

```python
import functools

import numpy as np
import jax
import jax.numpy as jnp
from jax import lax
from jax.experimental import pallas as pl
from jax.experimental.pallas import tpu as pltpu

F32 = jnp.float32
BF16 = jnp.bfloat16

I32 = jnp.int32

GRID_W = 64
CONV_W = 5
POOL_WINDOWS = (2, 4, 8, 16)
SSD_HEAD_DIM = 64
SSD_GROUPS = 2
SSD_STATE = 128
GDN_HEAD_DIM = 64
N_EXPERTS = 16
EC_CAPACITY = 2
EPS = 1e-6

LANE = 128
SUBLANES = 8
VMEM_LIMIT = 48 * 1024 * 1024

ROW_TILE = 512
WIDE_TILE = 1024
SCAN_CHUNK = 256
HALO = SUBLANES
PIECE = SUBLANES
NEG = -1e30


def _params():
    return pltpu.CompilerParams(dimension_semantics=("arbitrary", "arbitrary"), vmem_limit_bytes=VMEM_LIMIT)


def _inproj_kernel(x_ref, sc_ref, sh_ref, g_ref, w_ref, *o_refs, transpose_grid):
    x = x_ref[0]
    tm = x.shape[0]
    ms = jnp.mean(x * x, axis=-1, keepdims=True)
    h = x * lax.rsqrt(ms + EPS) * g_ref[...]
    h = (h * (1.0 + sc_ref[0]) + sh_ref[0]).astype(BF16)
    if transpose_grid:
        p = lax.broadcasted_iota(jnp.int32, (tm, tm), 0)
        q = lax.broadcasted_iota(jnp.int32, (tm, tm), 1)
        perm = (q == (p % SUBLANES) * GRID_W + p // SUBLANES).astype(BF16)
        h = jnp.dot(perm, h, preferred_element_type=F32).astype(BF16)
    y = jnp.dot(h, w_ref[...], preferred_element_type=F32)
    off = 0
    for o_ref in o_refs:
        n = o_ref.shape[-1]
        if transpose_grid:
            o_ref[0] = y[:, off:off + n].reshape(GRID_W, SUBLANES, n)
        else:
            o_ref[0] = y[:, off:off + n]
        off += n


def _inproj(x, scale, shift, g, w, widths, *, column_major):
    b, t, d = x.shape
    rows = t // GRID_W
    if column_major:
        assert rows % SUBLANES == 0
        tm = SUBLANES * GRID_W
        out_specs = [pl.BlockSpec((1, GRID_W, SUBLANES, n), lambda i, j: (i, 0, j, 0)) for n in widths]
        out_shape = [jax.ShapeDtypeStruct((b, GRID_W, rows, n), F32) for n in widths]
    else:
        tm = min(t, WIDE_TILE)
        out_specs = [pl.BlockSpec((1, tm, n), lambda i, j: (i, j, 0)) for n in widths]
        out_shape = [jax.ShapeDtypeStruct((b, t, n), F32) for n in widths]
    outs = pl.pallas_call(
        functools.partial(_inproj_kernel, transpose_grid=column_major),
        grid=(b, t // tm),
        in_specs=[
            pl.BlockSpec((1, tm, d), lambda i, j: (i, j, 0)),
            pl.BlockSpec((1, 1, d), lambda i, j: (i, 0, 0)),
            pl.BlockSpec((1, 1, d), lambda i, j: (i, 0, 0)),
            pl.BlockSpec((1, d), lambda i, j: (0, 0)),
            pl.BlockSpec((d, w.shape[1]), lambda i, j: (0, 0)),
        ],
        out_specs=out_specs,
        out_shape=out_shape,
        compiler_params=_params(),
        name="inproj",
    )(x, scale, shift, g, w)
    return [o.reshape(b, t, n) for o, n in zip(outs, widths)]


def _outproj_kernel(*refs, n_in):
    a_refs = refs[:n_in]
    x_ref, gate_ref, w_ref, o_ref = refs[n_in:]
    a = jnp.concatenate([a_ref[0].astype(BF16) for a_ref in a_refs], axis=-1)
    y = jnp.dot(a, w_ref[...], preferred_element_type=F32)
    o_ref[0] = x_ref[0] + gate_ref[0] * y


def _outproj(parts, x, gate, w):
    b, t, d = x.shape
    tm = min(t, ROW_TILE)
    n_in = len(parts)
    return pl.pallas_call(
        functools.partial(_outproj_kernel, n_in=n_in),
        grid=(b, t // tm),
        in_specs=(
            [pl.BlockSpec((1, tm, p.shape[-1]), lambda i, j: (i, j, 0)) for p in parts]
            + [pl.BlockSpec((1, tm, d), lambda i, j: (i, j, 0)), pl.BlockSpec((1, 1, d), lambda i, j: (i, 0, 0)),
               pl.BlockSpec(w.shape, lambda i, j: (0, 0))]),
        out_specs=pl.BlockSpec((1, tm, d), lambda i, j: (i, j, 0)),
        out_shape=jax.ShapeDtypeStruct((b, t, d), F32),
        compiler_params=_params(),
        name="outproj",
    )(*parts, x, gate, w)


def _pool_kernel(u_ref, prev_ref, next_ref, w_ref, sc_ref, o_ref, e0_ref, e1_ref, *, block, seq, group):
    TM = block
    j = pl.program_id(1)
    nb = pl.num_programs(1)
    u = u_ref[0]
    n_ext = TM + 2 * HALO
    e0_ref[0:HALO, :] = jnp.where(j > 0, prev_ref[0], 0.0)
    e0_ref[HALO:HALO + TM, :] = u
    e0_ref[HALO + TM:, :] = jnp.where(j < nb - 1, next_ref[0], 0.0)
    tok = j * TM + lax.broadcasted_iota(jnp.int32, (TM, 1), 0)
    lane = lax.broadcasted_iota(jnp.int32, (1, u.shape[-1]), 1)
    src, dst = e0_ref, e1_ref
    pooled = jnp.zeros_like(u)
    half = 1
    for gi, win in enumerate(POOL_WINDOWS):
        assert win == 2 * half
        lo_r, hi_r = half, n_ext - half
        if half == 1:
            dst[lo_r:hi_r, :] = src[lo_r - 1:hi_r - 1, :] + src[lo_r:hi_r, :]
        else:
            q = half // 2
            dst[lo_r:hi_r, :] = src[lo_r - q:hi_r - q, :] + src[lo_r + q:hi_r + q, :]
        cnt = (jnp.minimum(tok + half, seq) - jnp.maximum(tok - half, 0)).astype(F32)
        mean = dst[HALO:HALO + TM, :] / cnt
        pooled = jnp.where((lane >= gi * group) & (lane < (gi + 1) * group), mean, pooled)
        src, dst = dst, src
        half *= 2
    dd = pooled - u
    y = jnp.dot(dd.astype(BF16), w_ref[...], preferred_element_type=F32)
    o_ref[0] = y * sc_ref[...]


def _pool_branch(u, pool_w, pool_scale):
    b, t, c = u.shape
    ng, pg, _ = pool_w.shape
    tm = min(t, ROW_TILE)
    hb = tm // HALO
    w_bd = jnp.zeros((c, c), F32)
    for gi in range(ng):
        w_bd = w_bd.at[gi * pg:(gi + 1) * pg, gi * pg:(gi + 1) * pg].set(pool_w[gi])
    return pl.pallas_call(
        functools.partial(_pool_kernel, block=tm, seq=t, group=pg),
        grid=(b, t // tm),
        in_specs=[
            pl.BlockSpec((1, tm, c), lambda i, j: (i, j, 0)),
            pl.BlockSpec((1, HALO, c), lambda i, j: (i, jnp.maximum(j * hb - 1, 0), 0)),
            pl.BlockSpec((1, HALO, c), lambda i, j: (i, jnp.minimum((j + 1) * hb, t // HALO - 1), 0)),
            pl.BlockSpec((c, c), lambda i, j: (0, 0)),
            pl.BlockSpec((1, c), lambda i, j: (0, 0)),
        ],
        out_specs=pl.BlockSpec((1, tm, c), lambda i, j: (i, j, 0)),
        out_shape=jax.ShapeDtypeStruct((b, t, c), F32),
        scratch_shapes=[pltpu.VMEM((tm + 2 * HALO, c), F32), pltpu.VMEM((tm + 2 * HALO, c), F32)],
        compiler_params=_params(),
        name="pool",
    )(u, u, u, w_bd.astype(BF16), pool_scale[None])


def _ffn_kernel(x_ref, gate_ref, wg_ref, wu_ref, wd_ref, o_ref, wgb_ref, wub_ref, wdb_ref):
    @pl.when(pl.program_id(1) == 0)
    def _():
        wgb_ref[...] = wg_ref[0].astype(BF16)
        wub_ref[...] = wu_ref[0].astype(BF16)
        wdb_ref[...] = wd_ref[0].astype(BF16)

    x = x_ref[0].astype(BF16)
    hg = jnp.dot(x, wgb_ref[...], preferred_element_type=F32)
    hu = jnp.dot(x, wub_ref[...], preferred_element_type=F32)
    hid = (hg * jax.nn.sigmoid(hg)) * hu
    y = jnp.dot(hid.astype(BF16), wdb_ref[...], preferred_element_type=F32)
    o_ref[0] = y * gate_ref[0]


def _expert_ffn(xe, gate, wg, wu, wd, layer):
    be, c, d = xe.shape
    _, e, _, f = wg.shape
    tc = min(c, WIDE_TILE)
    return pl.pallas_call(
        _ffn_kernel,
        grid=(be, c // tc),
        in_specs=[
            pl.BlockSpec((1, tc, d), lambda i, j: (i, j, 0)),
            pl.BlockSpec((1, tc, 1), lambda i, j: (i, j, 0)),
            pl.BlockSpec((None, 1, d, f), lambda i, j: (layer, i % e, 0, 0)),
            pl.BlockSpec((None, 1, d, f), lambda i, j: (layer, i % e, 0, 0)),
            pl.BlockSpec((None, 1, f, d), lambda i, j: (layer, i % e, 0, 0)),
        ],
        out_specs=pl.BlockSpec((1, tc, d), lambda i, j: (i, j, 0)),
        out_shape=jax.ShapeDtypeStruct((be, c, d), F32),
        scratch_shapes=[pltpu.VMEM((d, f), BF16), pltpu.VMEM((d, f), BF16), pltpu.VMEM((f, d), BF16)],
        compiler_params=_params(),
        name="expert_ffn",
    )(xe, gate, wg, wu, wd)


def _split3(a):
    hi = a.astype(BF16)
    r1 = a - hi.astype(F32)
    mid = r1.astype(BF16)
    lo = (r1 - mid.astype(F32)).astype(BF16)
    return hi, mid, lo


def _cumsum_both(la, incl):
    parts = _split3(la)
    cs_col = sum(jnp.dot(incl, p, preferred_element_type=F32) for p in parts)
    cs_row = sum(lax.dot_general(p, incl, (((0,), (1,)), ((), ())), preferred_element_type=F32) for p in parts)
    return cs_col, cs_row


def _ssd_kernel(*refs, chunk, rev, final, heads):
    if final:
        (xbc_ref, prev_ref, next_ref, dt_ref, cw_ref, cb_ref, dtb_ref, a_ref, dsk_ref, s0_ref,
         yb_ref, z_ref, ng_ref, y_ref, sfin_ref, s_ref, ext_ref) = refs
    else:
        (xbc_ref, prev_ref, next_ref, dt_ref, cw_ref, cb_ref, dtb_ref, a_ref, dsk_ref, s0_ref,
         y_ref, sfin_ref, act_ref, s_ref, ext_ref) = refs
    L = chunk
    i = pl.program_id(1)
    nc = pl.num_programs(1)
    j = nc - 1 - i if rev else i
    hd = SSD_HEAD_DIM
    ssd_dim = heads * hd
    rep = heads // SSD_GROUPS

    @pl.when(i == 0)
    def _():
        s_ref[...] = s0_ref[0]

    if final:
        act = xbc_ref[0]
    else:
        ext_ref[0:HALO, :] = jnp.where(j > 0, prev_ref[0], 0.0)
        ext_ref[HALO:HALO + L, :] = xbc_ref[0]
        ext_ref[HALO + L:, :] = jnp.where(j < nc - 1, next_ref[0], 0.0)
        base = HALO - CONV_W // 2
        acc = cb_ref[...] + cw_ref[0:1, :] * ext_ref[base:base + L, :]
        for k in range(1, CONV_W):
            acc = acc + cw_ref[k:k + 1, :] * ext_ref[base + k:base + k + L, :]
        act = acc * jax.nn.sigmoid(acc)
        act_ref[0] = act
    xs = act[:, :ssd_dim]
    bmat = act[:, ssd_dim:ssd_dim + SSD_GROUPS * SSD_STATE]
    cmat = act[:, ssd_dim + SSD_GROUPS * SSD_STATE:]

    dtv = jax.nn.softplus(dt_ref[0] + dtb_ref[...])
    la = dtv * a_ref[...]
    row = lax.broadcasted_iota(jnp.int32, (L, L), 0)
    col = lax.broadcasted_iota(jnp.int32, (L, L), 1)
    mask = (row <= col) if rev else (row >= col)
    cs_col, cs_row = _cumsum_both(la, mask.astype(BF16))
    last = 0 if rev else L - 1

    gmats = []
    for g in range(SSD_GROUPS):
        cg = cmat[:, g * SSD_STATE:(g + 1) * SSD_STATE].astype(BF16)
        bg = bmat[:, g * SSD_STATE:(g + 1) * SSD_STATE].astype(BF16)
        gmats.append(lax.dot_general(cg, bg, (((1,), (1,)), ((), ())), preferred_element_type=F32))

    hs = range(heads)
    cis = [(heads if rev else 0) + h for h in hs]
    b_g = [bmat[:, g * SSD_STATE:(g + 1) * SSD_STATE] for g in range(SSD_GROUPS)]
    c_g = [cmat[:, g * SSD_STATE:(g + 1) * SSD_STATE] for g in range(SSD_GROUPS)]
    csc = [cs_col[:, ci:ci + 1] for ci in cis]
    tot = [cc[last:last + 1, :] for cc in csc]
    dec = [jnp.exp(jnp.where(mask, csc[h] - cs_row[cis[h]:cis[h] + 1, :], NEG)) for h in hs]
    xs_h = [xs[:, h * hd:(h + 1) * hd] for h in hs]
    xdt = [(xs_h[h] * dtv[:, cis[h]:cis[h] + 1]).astype(BF16) for h in hs]
    s_old = [s_ref[h] for h in hs]
    y_intra = [jnp.dot((gmats[h // rep] * dec[h]).astype(BF16), xdt[h], preferred_element_type=F32) for h in hs]
    y_inter = [jnp.dot((c_g[h // rep] * jnp.exp(csc[h])).astype(BF16), s_old[h].astype(BF16),
                       preferred_element_type=F32) for h in hs]
    local = [lax.dot_general((b_g[h // rep] * jnp.exp(tot[h] - csc[h])).astype(BF16), xdt[h],
                             (((0,), (0,)), ((), ())), preferred_element_type=F32) for h in hs]
    for h in hs:
        s_ref[h] = s_old[h] * jnp.exp(tot[h]) + local[h]
    ys = [y_intra[h] + y_inter[h] for h in hs]
    if final:
        ys = [ys[h] + dsk_ref[:, h * hd:(h + 1) * hd] * xs_h[h] for h in hs]
    y = jnp.concatenate(ys, axis=-1)
    if final:
        y = y + yb_ref[0]
        z = z_ref[0]
        y = y * (z * jax.nn.sigmoid(z))
        y = y * lax.rsqrt(jnp.mean(y * y, axis=-1, keepdims=True) + EPS) * ng_ref[...]
    y_ref[0] = y

    @pl.when(i == nc - 1)
    def _():
        sfin_ref[0] = s_ref[...]


def _ssd_pass(xbc, dt, cw, cb, dtb, a, dsk, s0, *, rev, final_inputs=None):
    b, t, width = xbc.shape
    heads = s0.shape[1]
    ssd_dim = heads * SSD_HEAD_DIM
    L = min(t, SCAN_CHUNK)
    nc = t // L
    hb = L // HALO
    final = final_inputs is not None

    def cidx(i):
        return nc - 1 - i if rev else i

    in_specs = [
        pl.BlockSpec((1, L, width), lambda bi, i: (bi, cidx(i), 0)),
        pl.BlockSpec((1, HALO, width), lambda bi, i: (bi, jnp.maximum(cidx(i) * hb - 1, 0), 0)),
        pl.BlockSpec((1, HALO, width), lambda bi, i: (bi, jnp.minimum((cidx(i) + 1) * hb, t // HALO - 1), 0)),
        pl.BlockSpec((1, L, LANE), lambda bi, i: (bi, cidx(i), 0)),
        pl.BlockSpec((8, width), lambda bi, i: (0, 0)),
        pl.BlockSpec((1, width), lambda bi, i: (0, 0)),
        pl.BlockSpec((1, LANE), lambda bi, i: (0, 0)),
        pl.BlockSpec((1, LANE), lambda bi, i: (0, 0)),
        pl.BlockSpec((1, ssd_dim), lambda bi, i: (0, 0)),
        pl.BlockSpec((1, heads, SSD_STATE, SSD_HEAD_DIM), lambda bi, i: (bi, 0, 0, 0)),
    ]
    args = [xbc, xbc, xbc, dt, cw, cb, dtb, a, dsk, s0]
    if final:
        yb, z, ng = final_inputs
        in_specs += [
            pl.BlockSpec((1, L, ssd_dim), lambda bi, i: (bi, cidx(i), 0)),
            pl.BlockSpec((1, L, ssd_dim), lambda bi, i: (bi, cidx(i), 0)),
            pl.BlockSpec((1, ssd_dim), lambda bi, i: (0, 0)),
        ]
        args += [yb, z, ng]
    out_specs = [
        pl.BlockSpec((1, L, ssd_dim), lambda bi, i: (bi, cidx(i), 0)),
        pl.BlockSpec((1, heads, SSD_STATE, SSD_HEAD_DIM), lambda bi, i: (bi, 0, 0, 0)),
    ]
    out_shape = [
        jax.ShapeDtypeStruct((b, t, ssd_dim), F32),
        jax.ShapeDtypeStruct((b, heads, SSD_STATE, SSD_HEAD_DIM), F32),
    ]
    if not final:
        out_specs.append(pl.BlockSpec((1, L, width), lambda bi, i: (bi, cidx(i), 0)))
        out_shape.append(jax.ShapeDtypeStruct((b, t, width), F32))
    return pl.pallas_call(
        functools.partial(_ssd_kernel, chunk=L, rev=rev, final=final, heads=heads),
        grid=(b, nc),
        in_specs=in_specs,
        out_specs=out_specs,
        out_shape=out_shape,
        scratch_shapes=[
            pltpu.VMEM((heads, SSD_STATE, SSD_HEAD_DIM), F32),
            pltpu.VMEM((L + 2 * HALO, width), F32),
        ],
        compiler_params=_params(),
        name="ssd_bwd" if rev else "ssd_fwd",
    )(*args)


def _ssd_stream(z, xbc, dt, conv_w, conv_b, a_log, dt_bias, d_skip, norm_g, s0_f, s0_b):
    heads = s0_f.shape[1]
    cw = jnp.pad(conv_w, ((0, 8 - CONV_W), (0, 0)))
    cb = conv_b[None]
    dtb = jnp.pad(dt_bias.reshape(1, -1), ((0, 0), (0, LANE - 2 * heads)))
    a = jnp.pad(-jnp.exp(a_log).reshape(1, -1), ((0, 0), (0, LANE - 2 * heads)))
    dsk = jnp.repeat(d_skip, SSD_HEAD_DIM)[None]
    yb, s_b, act = _ssd_pass(xbc, dt, cw, cb, dtb, a, dsk, s0_b, rev=True)
    y, s_f = _ssd_pass(act, dt, cw, cb, dtb, a, dsk, s0_f, rev=False, final_inputs=(yb, z, norm_g[None]))
    return y, s_f, s_b


def _group_sum(a, ones_bd):
    hi = a.astype(BF16)
    lo = (a - hi.astype(F32)).astype(BF16)
    return (jnp.dot(hi, ones_bd, preferred_element_type=F32) + jnp.dot(lo, ones_bd, preferred_element_type=F32))


def _mm_bf16(a, b):
    return jnp.dot(a.astype(BF16), b.astype(BF16), preferred_element_type=F32)


def _unit_tri_inverse_many(xms, rev):
    n = xms[0].shape[0]
    row = lax.broadcasted_iota(jnp.int32, (n, n), 0)
    col = lax.broadcasted_iota(jnp.int32, (n, n), 1)
    eye = (row == col).astype(F32)
    ds = None
    m, sh = 1, 0
    while m < n:
        same = (row >> (sh + 1)) == (col >> (sh + 1))
        rbit = (row >> sh) & 1
        cbit = (col >> sh) & 1
        sel = same & ((rbit == 0) & (cbit == 1) if rev else (rbit == 1) & (cbit == 0))
        cs = [jnp.where(sel, x, 0.0) for x in xms]
        if ds is None:
            ds = [eye - c for c in cs]
        elif m < SUBLANES:
            es = [_mm_bf16(c, d) for c, d in zip(cs, ds)]
            ds = [d - _mm_bf16(d, e) for d, e in zip(ds, es)]
        else:
            act = 0 if rev else 1

            def rows(a, which):
                return a.reshape(n // (2 * m), 2, m, n)[:, which].reshape(n // 2, n)

            def merge(keep, new):
                pair = (new, keep) if rev else (keep, new)
                return jnp.stack([p.reshape(n // (2 * m), m, n) for p in pair], axis=1).reshape(n, n)

            es = [_mm_bf16(rows(c, act), d) for c, d in zip(cs, ds)]
            zero = jnp.zeros((n // 2, n), F32)
            ds = [merge(rows(d, 1 - act), rows(d, act) - _mm_bf16(rows(d, act), merge(zero, e)))
                  for d, e in zip(ds, es)]
        m, sh = 2 * m, sh + 1
    return ds


def _gdn_kernel(*refs, block, rev, final, heads):
    if final:
        (qkv_ref, prev_ref, next_ref, ab_ref, cw_ref, dtb_ref, a_ref, s0_ref, ob_ref, gate_ref, ng_ref,
         o_ref, sfin_ref, s_ref, ext_ref) = refs
    else:
        (qkv_ref, prev_ref, next_ref, ab_ref, cw_ref, dtb_ref, a_ref, s0_ref,
         o_ref, sfin_ref, act_ref, s_ref, ext_ref) = refs
    TB = block
    L = block
    hd = GDN_HEAD_DIM
    dim = heads * hd
    i = pl.program_id(1)
    nb = pl.num_programs(1)
    j = nb - 1 - i if rev else i

    @pl.when(i == 0)
    def _():
        s_ref[...] = s0_ref[0]

    ri = lax.broadcasted_iota(jnp.int32, (dim, dim), 0) // hd
    ci_ = lax.broadcasted_iota(jnp.int32, (dim, dim), 1) // hd
    ones_bd = (ri == ci_).astype(BF16)
    if final:
        act = qkv_ref[0]
        q = act[:, :dim]
        k = act[:, dim:2 * dim]
        v = act[:, 2 * dim:]
    else:
        ext_ref[0:HALO, :] = jnp.where(j > 0, prev_ref[0], 0.0)
        ext_ref[HALO:HALO + TB, :] = qkv_ref[0]
        ext_ref[HALO + TB:, :] = jnp.where(j < nb - 1, next_ref[0], 0.0)
        base = HALO - CONV_W // 2
        acc = cw_ref[0:1, :] * ext_ref[base:base + TB, :]
        for kk in range(1, CONV_W):
            acc = acc + cw_ref[kk:kk + 1, :] * ext_ref[base + kk:base + kk + TB, :]
        act = acc * jax.nn.sigmoid(acc)
        q = act[:, :dim]
        k = act[:, dim:2 * dim]
        v = act[:, 2 * dim:]
        q = q * lax.rsqrt(_group_sum(q * q, ones_bd) + EPS) * (hd ** -0.5)
        k = k * lax.rsqrt(_group_sum(k * k, ones_bd) + EPS)
        act_ref[0] = jnp.concatenate([q, k, v], axis=-1)
    ab = ab_ref[0]
    gl = a_ref[...] * jax.nn.softplus(ab + dtb_ref[...])
    beta = jax.nn.sigmoid(ab)

    row = lax.broadcasted_iota(jnp.int32, (L, L), 0)
    col = lax.broadcasted_iota(jnp.int32, (L, L), 1)
    incl = (row <= col) if rev else (row >= col)
    strict = (row < col) if rev else (row > col)
    last = 0 if rev else L - 1
    cs_col, cs_row = _cumsum_both(gl, incl.astype(BF16))

    hs = range(heads)
    cis = [(heads if rev else 0) + h for h in hs]
    qh = [q[:, h * hd:(h + 1) * hd] for h in hs]
    kh = [k[:, h * hd:(h + 1) * hd] for h in hs]
    vh = [v[:, h * hd:(h + 1) * hd] for h in hs]
    csc = [cs_col[:, ci:ci + 1] for ci in cis]
    bcol = [beta[:, 2 * heads + ci:2 * heads + ci + 1] for ci in cis]
    qk_kk = [lax.dot_general(jnp.concatenate([qh[h], kh[h]], axis=0).astype(BF16), kh[h].astype(BF16),
                             (((1,), (1,)), ((), ())), preferred_element_type=F32) for h in hs]
    dec = [jnp.exp(jnp.where(incl, csc[h] - cs_row[cis[h]:cis[h] + 1, :], NEG)) for h in hs]
    attn = [(qk_kk[h][:L] * dec[h]).astype(BF16) for h in hs]
    xm = [jnp.where(strict, bcol[h] * qk_kk[h][L:] * dec[h], 0.0) for h in hs]
    tm = _unit_tri_inverse_many(xm, rev)
    egc = [jnp.exp(cc) for cc in csc]
    uw = [jnp.dot(tm[h].astype(BF16),
                  jnp.concatenate([vh[h] * bcol[h], kh[h] * (bcol[h] * egc[h])], axis=1).astype(BF16),
                  preferred_element_type=F32) for h in hs]
    s_old = [s_ref[h] for h in hs]
    rs = [jnp.dot(jnp.concatenate([uw[h][:, hd:], qh[h] * egc[h]], axis=0).astype(BF16), s_old[h].astype(BF16),
                  preferred_element_type=F32) for h in hs]
    vnb = [(uw[h][:, :hd] - rs[h][:L]).astype(BF16) for h in hs]
    outs = [rs[h][L:] + jnp.dot(attn[h], vnb[h], preferred_element_type=F32) for h in hs]
    for h in hs:
        tot = csc[h][last:last + 1, :]
        kend = (kh[h] * jnp.exp(tot - csc[h])).astype(BF16)
        s_ref[h] = s_old[h] * jnp.exp(tot) + lax.dot_general(kend, vnb[h], (((0,), (0,)), ((), ())),
                                                            preferred_element_type=F32)
    o = jnp.concatenate(outs, axis=-1)
    if final:
        o = o + ob_ref[0]
        ms = _group_sum(o * o, ones_bd) * (1.0 / hd)
        gate = gate_ref[0]
        o = o * lax.rsqrt(ms + EPS) * ng_ref[...] * (gate * jax.nn.sigmoid(gate))
    o_ref[0] = o

    @pl.when(i == nb - 1)
    def _():
        sfin_ref[0] = s_ref[...]


def _gdn_pass(qkv, ab, cw, dtb, a, s0, *, rev, block, final_inputs=None, raster_out=False):
    b, t, width = qkv.shape
    heads = s0.shape[1]
    dim = heads * GDN_HEAD_DIM
    TB = block
    nb = t // TB
    hb = TB // HALO
    final = final_inputs is not None

    def bidx(i):
        return nb - 1 - i if rev else i

    in_specs = [
        pl.BlockSpec((1, TB, width), lambda bi, i: (bi, bidx(i), 0)),
        pl.BlockSpec((1, HALO, width), lambda bi, i: (bi, jnp.maximum(bidx(i) * hb - 1, 0), 0)),
        pl.BlockSpec((1, HALO, width), lambda bi, i: (bi, jnp.minimum((bidx(i) + 1) * hb, t // HALO - 1), 0)),
        pl.BlockSpec((1, TB, LANE), lambda bi, i: (bi, bidx(i), 0)),
        pl.BlockSpec((8, width), lambda bi, i: (0, 0)),
        pl.BlockSpec((1, LANE), lambda bi, i: (0, 0)),
        pl.BlockSpec((1, LANE), lambda bi, i: (0, 0)),
        pl.BlockSpec((1, heads, GDN_HEAD_DIM, GDN_HEAD_DIM), lambda bi, i: (bi, 0, 0, 0)),
    ]
    args = [qkv, qkv, qkv, ab, cw, dtb, a, s0]
    if final:
        ob, gate, ng = final_inputs
        in_specs += [
            pl.BlockSpec((1, TB, dim), lambda bi, i: (bi, bidx(i), 0)),
            pl.BlockSpec((1, TB, dim), lambda bi, i: (bi, bidx(i), 0)),
            pl.BlockSpec((1, dim), lambda bi, i: (0, 0)),
        ]
        args += [ob, gate, ng]
    if raster_out:
        assert nb == GRID_W
        o_spec = pl.BlockSpec((1, TB, dim), lambda bi, i: (bi, 0, bidx(i)))
        o_shape = jax.ShapeDtypeStruct((b, TB, GRID_W * dim), F32)
    else:
        o_spec = pl.BlockSpec((1, TB, dim), lambda bi, i: (bi, bidx(i), 0))
        o_shape = jax.ShapeDtypeStruct((b, t, dim), F32)
    out_specs = [o_spec, pl.BlockSpec((1, heads, GDN_HEAD_DIM, GDN_HEAD_DIM), lambda bi, i: (bi, 0, 0, 0))]
    out_shape = [o_shape, jax.ShapeDtypeStruct((b, heads, GDN_HEAD_DIM, GDN_HEAD_DIM), F32)]
    if not final:
        out_specs.append(pl.BlockSpec((1, TB, width), lambda bi, i: (bi, bidx(i), 0)))
        out_shape.append(jax.ShapeDtypeStruct((b, t, width), F32))
    outs = pl.pallas_call(
        functools.partial(_gdn_kernel, block=TB, rev=rev, final=final, heads=heads),
        grid=(b, nb),
        in_specs=in_specs,
        out_specs=out_specs,
        out_shape=out_shape,
        scratch_shapes=[
            pltpu.VMEM((heads, GDN_HEAD_DIM, GDN_HEAD_DIM), F32),
            pltpu.VMEM((TB + 2 * HALO, width), F32),
        ],
        compiler_params=_params(),
        name="gdn_bwd" if rev else "gdn_fwd",
    )(*args)
    return (outs[0].reshape(b, t, dim),) + tuple(outs[1:])


def _gdn_stream(qkv, gate, ab, conv_w, a_log, dt_bias, norm_g, s0_f, s0_b, *, column_major):
    t = qkv.shape[1]
    heads = s0_f.shape[1]
    cw = jnp.pad(conv_w, ((0, 8 - CONV_W), (0, 0)))
    dtb = jnp.pad(dt_bias.reshape(1, -1), ((0, 0), (0, LANE - 2 * heads)))
    a = jnp.pad(-jnp.exp(a_log).reshape(1, -1), ((0, 0), (0, LANE - 2 * heads)))
    ng = jnp.tile(norm_g, heads)[None]
    rows = t // GRID_W
    fused_raster = column_major and rows <= SCAN_CHUNK and rows >= 2 * HALO and rows & (rows - 1) == 0
    block = rows if fused_raster else min(t, SCAN_CHUNK)
    ob, s_b, act = _gdn_pass(qkv, ab, cw, dtb, a, s0_b, rev=True, block=block)
    o, s_f = _gdn_pass(act, ab, cw, dtb, a, s0_f, rev=False, block=block, final_inputs=(ob, gate, ng),
                       raster_out=fused_raster)
    if column_major and not fused_raster:
        o = _to_raster(o)
    return o, s_f, s_b


def _router_kernel(x_ref, sc_ref, sh_ref, g_ref, rw_ref, h_ref, a_ref):
    x = x_ref[0]
    ms = jnp.mean(x * x, axis=-1, keepdims=True)
    h = x * lax.rsqrt(ms + EPS) * g_ref[...]
    h = h * (1.0 + sc_ref[0]) + sh_ref[0]
    hh, hm, _ = _split3(h)
    h_ref[0] = hh
    rw = rw_ref[...]
    rh = rw.astype(BF16)
    rm = (rw - rh.astype(F32)).astype(BF16)
    nt = (((1,), (1,)), ((), ()))
    lg = (lax.dot_general(rh, hh, nt, preferred_element_type=F32)
          + lax.dot_general(rh, hm, nt, preferred_element_type=F32)
          + lax.dot_general(rm, hh, nt, preferred_element_type=F32))
    ex = jnp.exp(lg - jnp.max(lg, axis=0, keepdims=True))
    aff = ex / jnp.sum(ex, axis=0, keepdims=True)
    for k in range(a_ref.shape[1]):
        a_ref[0, k] = aff[:, k * LANE:(k + 1) * LANE]


def _router(x, scale, shift, g, rw_t):
    b, t, d = x.shape
    e = rw_t.shape[0]
    tm = min(t, WIDE_TILE)
    return pl.pallas_call(
        _router_kernel,
        grid=(b, t // tm),
        in_specs=[
            pl.BlockSpec((1, tm, d), lambda i, j: (i, j, 0)),
            pl.BlockSpec((1, 1, d), lambda i, j: (i, 0, 0)),
            pl.BlockSpec((1, 1, d), lambda i, j: (i, 0, 0)),
            pl.BlockSpec((1, d), lambda i, j: (0, 0)),
            pl.BlockSpec((e, d), lambda i, j: (0, 0)),
        ],
        out_specs=[
            pl.BlockSpec((1, tm, d), lambda i, j: (i, j, 0)),
            pl.BlockSpec((1, tm // LANE, e, LANE), lambda i, j: (i, j, 0, 0)),
        ],
        out_shape=[
            jax.ShapeDtypeStruct((b, t, d), BF16),
            jax.ShapeDtypeStruct((b, t // LANE, e, LANE), F32),
        ],
        compiler_params=_params(),
        name="router",
    )(x, scale, shift, g, rw_t)


def _token_prefix(m3, ut, ones, lt):
    e, nb, _ = m3.shape
    m2 = m3.reshape(e * nb, LANE).astype(BF16)
    inb = jnp.dot(m2, ut, preferred_element_type=F32).reshape(e, nb, LANE)
    tot = jnp.dot(m2, ones, preferred_element_type=F32).reshape(e, nb, LANE)
    offs = jnp.stack([jnp.dot(lt, tot[i].astype(BF16), preferred_element_type=F32) for i in range(e)], axis=0)
    return inb, tot, offs


def _select_kernel(a_ref, idx_ref, gate_ref, srow_ref, st8_ref, npc_ref, cs_ref, cum_ref, cnt_ref, *, cap):
    a = a_ref[0]
    e_n, nb, _ = a.shape
    bits = lax.bitcast_convert_type(a, I32)

    def radix(i, prefix):
        cand = prefix | jnp.left_shift(jnp.int32(1), 30 - i)
        cnt = jnp.sum(jnp.sum((bits >= cand).astype(F32), axis=2, keepdims=True), axis=1, keepdims=True)
        return jnp.where(cnt >= cap, cand, prefix)

    thr = lax.fori_loop(0, 31, radix, jnp.zeros((e_n, 1, 1), I32))
    li = lax.broadcasted_iota(I32, (LANE, LANE), 0)
    lj = lax.broadcasted_iota(I32, (LANE, LANE), 1)
    ut = (li < lj).astype(BF16)
    ones = jnp.ones((LANE, LANE), BF16)
    bi = lax.broadcasted_iota(I32, (nb, nb), 0)
    bj = lax.broadcasted_iota(I32, (nb, nb), 1)
    lt = (bj < bi).astype(BF16)

    gt = bits > thr
    eq = bits == thr
    n_gt = jnp.sum(jnp.sum(gt.astype(F32), axis=2, keepdims=True), axis=1, keepdims=True)
    tie_in, _, tie_offs = _token_prefix(eq.astype(F32), ut, ones, lt)
    sel = gt | (eq & (tie_in + tie_offs < cap - n_gt))
    pos_in, cnt, offs = _token_prefix(sel.astype(F32), ut, ones, lt)
    cs_ref[...] = jnp.where(sel, pos_in + 1.0, 0.0)
    cum_ref[...] = offs + cnt
    cnt_ref[...] = cnt

    offs_i = offs.astype(I32)
    cnt_i = cnt.astype(I32)
    st8 = (offs_i >> 3) << 3
    npc = jnp.where(cnt_i > 0, (offs_i + cnt_i - st8 + (PIECE - 1)) >> 3, 0)
    rbase = []
    run = jnp.zeros((nb, LANE), I32)
    for i in range(e_n):
        rbase.append(run)
        run = run + PIECE * npc[i]
    rbase = jnp.stack(rbase, axis=0)
    srow_ref[0] = jnp.where(sel, rbase + pos_in.astype(I32) + offs_i - st8, -1)
    st8_ref[0] = st8
    npc_ref[0] = npc

    jrow = lax.broadcasted_iota(I32, (1, cap), 1).astype(F32)
    sub_nb = lax.broadcasted_iota(I32, (nb, cap), 0).astype(F32)
    sub_l = lax.broadcasted_iota(I32, (LANE, cap), 0).astype(F32)
    tn = (((0,), (0,)), ((), ()))

    def compact(ei, carry):
        cum_col = cum_ref[ei][:, 0:1]
        cnt_col = cnt_ref[ei][:, 0:1]
        ge = cum_col <= jrow
        blk_j = jnp.sum(ge.astype(F32), axis=0, keepdims=True)
        offs_j = jnp.sum(jnp.where(ge, cnt_col, 0.0), axis=0, keepdims=True)
        rank1 = jrow - offs_j + 1.0
        g_t = (sub_nb == blk_j).astype(BF16)
        row_t = lax.dot_general(cs_ref[ei].astype(BF16), g_t, tn, preferred_element_type=F32)
        match = row_t == rank1
        lane_j = jnp.sum(jnp.where(match, sub_l, 0.0), axis=0, keepdims=True)
        idx_ref[0, pl.ds(ei, 1), :] = (blk_j * LANE + lane_j).astype(I32)
        parts = _split3(a_ref[0, ei])
        aff_t = sum(lax.dot_general(p, g_t, tn, preferred_element_type=F32) for p in parts)
        gate_ref[0, pl.ds(ei, 1), :] = jnp.sum(jnp.where(match, aff_t, 0.0), axis=0, keepdims=True)
        return carry

    lax.fori_loop(0, e_n, compact, 0)


def _ec_select(aff_em, cap):
    b, e, nb, _ = aff_em.shape
    big = lambda dt: jax.ShapeDtypeStruct((b, e, nb, LANE), dt)
    spec4 = pl.BlockSpec((1, e, nb, LANE), lambda i: (i, 0, 0, 0))
    spec3 = pl.BlockSpec((1, e, cap), lambda i: (i, 0, 0))
    return pl.pallas_call(
        functools.partial(_select_kernel, cap=cap),
        grid=(b,),
        in_specs=[spec4],
        out_specs=[spec3, spec3, spec4, spec4, spec4],
        out_shape=[jax.ShapeDtypeStruct((b, e, cap), I32), jax.ShapeDtypeStruct((b, e, cap), F32),
                   big(I32), big(I32), big(I32)],
        scratch_shapes=[pltpu.VMEM((e, nb, LANE), F32)] * 3,
        compiler_params=pltpu.CompilerParams(dimension_semantics=("arbitrary",), vmem_limit_bytes=VMEM_LIMIT),
        name="ec_select",
    )(aff_em)


def _combine_kernel(st8_sm, npc_sm, ye_hbm, srow_ref, x_ref, g_ref, fn_ref, o_ref, stage, acc_ref, sem,
                    *, final, n_exp):
    b = pl.program_id(0)
    k = pl.program_id(1)
    nb = pl.num_programs(1)
    step = b * nb + k
    nsteps = pl.num_programs(0) * nb
    slot = step % 2

    def run_copy(bb, e, src_row, sl, dst_row, pieces):
        size = pieces * PIECE
        return pltpu.make_async_copy(ye_hbm.at[bb, e, pl.ds(pl.multiple_of(src_row, PIECE), size), :],
                                     stage.at[sl, pl.ds(pl.multiple_of(dst_row, PIECE), size), :], sem.at[sl])

    def issue(st, sl):
        bb = st // nb

        def per_e(e, r):
            s8 = st8_sm[st * n_exp + e]
            n = npc_sm[st * n_exp + e]

            def quad(p, r2):
                run_copy(bb, e, s8 + 4 * PIECE * p, sl, r2, 4).start()
                return r2 + 4 * PIECE

            r = lax.fori_loop(0, n >> 2, quad, r)
            done = (n >> 2) << 2

            @pl.when((n & 2) != 0)
            def _():
                run_copy(bb, e, s8 + PIECE * done, sl, r, 2).start()

            r = r + PIECE * (n & 2)
            done = done + (n & 2)

            @pl.when((n & 1) != 0)
            def _():
                run_copy(bb, e, s8 + PIECE * done, sl, r, 1).start()

            return r + PIECE * (n & 1)

        lax.fori_loop(0, n_exp, per_e, 0)

    @pl.when(step == 0)
    def _():
        issue(step, slot)

    @pl.when(step + 1 < nsteps)
    def _():
        issue(step + 1, 1 - slot)

    npieces = lax.fori_loop(0, n_exp, lambda e, s: s + npc_sm[step * n_exp + e], 0)
    rows = npieces * PIECE

    @pl.when(npieces > 0)
    def _():
        pltpu.make_async_copy(ye_hbm.at[0, 0, pl.ds(0, rows), :], stage.at[slot, pl.ds(0, rows), :],
                              sem.at[slot]).wait()
    stage[slot, pl.ds(pl.multiple_of(rows, PIECE), LANE), :] = jnp.zeros((LANE, stage.shape[-1]), F32)

    srow = srow_ref[0, 0]
    acc_ref[...] = jnp.zeros_like(acc_ref)
    riota = lax.broadcasted_iota(I32, (LANE, LANE), 0)
    tn = (((0,), (0,)), ((), ()))

    def chunk(c, carry):
        r0 = pl.multiple_of(c * LANE, LANE)
        rid = riota + r0
        pt = (srow[0:1, :] == rid).astype(F32)
        for e in range(1, n_exp):
            pt = pt + (srow[e:e + 1, :] == rid).astype(F32)
        ptb = pt.astype(BF16)
        st = stage[slot, pl.ds(r0, LANE), :]
        hi = st.astype(BF16)
        lo = (st - hi.astype(F32)).astype(BF16)
        acc_ref[...] += lax.dot_general(jnp.concatenate([ptb, ptb], axis=0), jnp.concatenate([hi, lo], axis=0), tn,
                                        preferred_element_type=F32)
        return carry

    lax.fori_loop(0, (rows + LANE - 1) // LANE, chunk, 0)
    y = x_ref[0] + g_ref[0] * acc_ref[...]
    if final:
        y = y * lax.rsqrt(jnp.mean(y * y, axis=-1, keepdims=True) + EPS) * fn_ref[...]
    o_ref[0] = y


def _ec_combine(ye, srow_bm, st8, npc, x, gate, final_g=None):
    b, t, d = x.shape
    e = ye.shape[1]
    nb = t // LANE
    final = final_g is not None
    fn = final_g if final else jnp.ones((1, d), F32)
    max_rows = e * (LANE + 2 * PIECE) + LANE
    grid_spec = pltpu.PrefetchScalarGridSpec(
        num_scalar_prefetch=2,
        grid=(b, nb),
        in_specs=[
            pl.BlockSpec(memory_space=pl.ANY),
            pl.BlockSpec((1, 1, e, LANE), lambda i, j, *_: (i, j, 0, 0)),
            pl.BlockSpec((1, LANE, d), lambda i, j, *_: (i, j, 0)),
            pl.BlockSpec((1, 1, d), lambda i, j, *_: (i, 0, 0)),
            pl.BlockSpec((1, d), lambda i, j, *_: (0, 0)),
        ],
        out_specs=pl.BlockSpec((1, LANE, d), lambda i, j, *_: (i, j, 0)),
        scratch_shapes=[
            pltpu.VMEM((2, max_rows, d), F32),
            pltpu.VMEM((LANE, d), F32),
            pltpu.SemaphoreType.DMA((2,)),
        ],
    )
    return pl.pallas_call(
        functools.partial(_combine_kernel, final=final, n_exp=e),
        grid_spec=grid_spec,
        out_shape=jax.ShapeDtypeStruct((b, t, d), F32),
        compiler_params=_params(),
        name="ec_combine",
    )(st8, npc, ye, srow_bm, x, gate, fn)


def _expert_choice_block(x, scale, shift, g2, gate2, router_w, ew, final_g):
    b, t, d = x.shape
    e = router_w.shape[-1]
    cap = EC_CAPACITY * t // e
    h2, aff_bm = _router(x, scale, shift, g2, router_w.T)
    idx, gate, srow, st8, npc = _ec_select(jnp.transpose(aff_bm, (0, 2, 1, 3)), cap)
    xe = jax.vmap(lambda hb, ib: hb[ib])(h2, idx)
    ye = _expert_ffn(xe.reshape(b * e, cap, d), gate.reshape(b * e, cap, 1), *ew).reshape(b, e, cap, d)
    srow_bm = jnp.transpose(srow, (0, 2, 1, 3))
    st8_f = jnp.transpose(st8[..., 0], (0, 2, 1)).reshape(-1)
    npc_f = jnp.transpose(npc[..., 0], (0, 2, 1)).reshape(-1)
    return _ec_combine(ye, srow_bm, st8_f, npc_f, x, gate2, final_g)


def _rms_norm(x, g):
    return x * lax.rsqrt(jnp.mean(x * x, axis=-1, keepdims=True) + EPS) * g


def _to_raster(u):
    b, t, c = u.shape
    rows = t // GRID_W
    return u.reshape(b, GRID_W, rows, c).transpose(0, 2, 1, 3).reshape(b, t, c)


def _expert_choice_ffn(h, router_w, ew):
    b, t, d = h.shape
    cap = EC_CAPACITY * t // N_EXPERTS
    aff = jax.nn.softmax(jnp.einsum('btd,de->bte', h, router_w, precision=lax.Precision.HIGHEST), axis=-1)
    gate, idx = lax.top_k(jnp.swapaxes(aff, 1, 2), cap)
    xe = jax.vmap(lambda hb, ib: hb[ib])(h, idx)
    ye = _expert_ffn(xe.reshape(b * N_EXPERTS, cap, d), gate.reshape(b * N_EXPERTS, cap, 1), *ew)
    ye = ye.reshape(b, N_EXPERTS, cap, d)
    return jax.vmap(lambda ib, yb: jnp.zeros((t, d), yb.dtype).at[ib.reshape(-1)].add(yb.reshape(-1, d)))(idx, ye)


def kernel(x, c, ctx, c_ctx, norm1_g, norm2_g, ada_w, ada_b, w_in, w_out, pool_w, pool_scale, ssd_conv_w, ssd_conv_b, ssd_a_log, ssd_dt_bias, ssd_d, ssd_norm_g, gdn_conv_w, gdn_a_log, gdn_dt_bias, gdn_norm_g, router_w, exp_w_gate, exp_w_up, exp_w_down, final_norm_g):
    depth, d, _ = w_in.shape
    b, t, _ = x.shape
    pool_dim = pool_scale.shape[-1]
    ssd_dim = ssd_norm_g.shape[-1]
    ssd_heads = ssd_dim // SSD_HEAD_DIM
    ssd_bc = SSD_GROUPS * SSD_STATE
    gdn_dim = gdn_conv_w.shape[-1] // 3
    gdn_heads = gdn_dim // GDN_HEAD_DIM
    splits = (pool_dim, ssd_dim, ssd_dim + 2 * ssd_bc, 2 * ssd_heads, 3 * gdn_dim, gdn_dim, 2 * gdn_heads,
              2 * gdn_heads)
    cut = [0] + np.cumsum(splits).tolist()
    r_widths = (ssd_dim, ssd_dim + 2 * ssd_bc, pool_dim, LANE)
    g_widths = (3 * gdn_dim, gdn_dim, LANE)

    assert depth >= 1
    sc = jax.nn.silu(c)
    scc = jax.nn.silu(c_ctx)[None]
    for l in range(depth):
        last = l == depth - 1
        wl = w_in[l]
        seg = [wl[:, cut[i]:cut[i + 1]] for i in range(8)]
        w_r = jnp.concatenate(
            [seg[1], seg[2], seg[0], jnp.pad(seg[3], ((0, 0), (0, LANE - 2 * ssd_heads)))], axis=1).astype(BF16)
        w_g = jnp.concatenate(
            [seg[4], seg[5], jnp.pad(jnp.concatenate([seg[6], seg[7]], axis=1), ((0, 0), (0, LANE - 4 * gdn_heads)))],
            axis=1).astype(BF16)
        w_o = w_out[l].astype(BF16)
        ew = (exp_w_gate, exp_w_up, exp_w_down, l)
        m_lat = jnp.split(sc @ ada_w[l] + ada_b[l], 6, axis=-1)
        m_ctx = [jnp.broadcast_to(m, (b, d)) for m in jnp.split(scc @ ada_w[l] + ada_b[l], 6, axis=-1)]
        g1 = norm1_g[l][None]

        def project(xx, mm, column_major):
            pr = _inproj(xx, mm[1][:, None], mm[0][:, None], g1, w_r, r_widths, column_major=False)
            pg = _inproj(xx, mm[1][:, None], mm[0][:, None], g1, w_g, g_widths, column_major=column_major)
            return pr, pg

        ssd_p = (ssd_conv_w[l], ssd_conv_b[l], ssd_a_log[l], ssd_dt_bias[l], ssd_d[l], ssd_norm_g[l])
        gdn_p = (gdn_conv_w[l], gdn_a_log[l], gdn_dt_bias[l], gdn_norm_g[l])
        zs = jnp.zeros((b, ssd_heads, SSD_STATE, SSD_HEAD_DIM), F32)
        zg = jnp.zeros((b, gdn_heads, GDN_HEAD_DIM, GDN_HEAD_DIM), F32)

        (c_z, c_xbc, c_pool, c_dt), (c_qkv, c_gate, c_ab) = project(ctx, m_ctx, False)
        (l_z, l_xbc, l_pool, l_dt), (l_qkv, l_gate, l_ab) = project(x, m_lat, True)

        s_ctx, ssd_sf, ssd_sb = _ssd_stream(c_z, c_xbc, c_dt, *ssd_p, zs, zs)
        g_ctx, gdn_sf, gdn_sb = _gdn_stream(c_qkv, c_gate, c_ab, *gdn_p, zg, zg, column_major=False)
        s_lat, _, _ = _ssd_stream(l_z, l_xbc, l_dt, *ssd_p, ssd_sf, ssd_sb)
        g_lat, _, _ = _gdn_stream(l_qkv, l_gate, l_ab, *gdn_p, gdn_sf, gdn_sb, column_major=True)
        x = _outproj([_pool_branch(l_pool, pool_w[l], pool_scale[l]), s_lat, g_lat], x, m_lat[2][:, None], w_o)
        fin = final_norm_g[None] if last else None
        if t % WIDE_TILE == 0:
            x = _expert_choice_block(x, m_lat[4][:, None], m_lat[3][:, None], norm2_g[l][None], m_lat[5][:, None],
                                     router_w[l], ew, fin)
        else:
            h2 = _rms_norm(x, norm2_g[l]) * (1 + m_lat[4][:, None]) + m_lat[3][:, None]
            x = x + m_lat[5][:, None] * _expert_choice_ffn(h2, router_w[l], ew)
            if last:
                x = _rms_norm(x, final_norm_g)
        if not last:
            ctx = _outproj([_pool_branch(c_pool, pool_w[l], pool_scale[l]), s_ctx, g_ctx], ctx, m_ctx[2][:, None],
                           w_o)
            h2c = _rms_norm(ctx, norm2_g[l]) * (1 + m_ctx[4][:, None]) + m_ctx[3][:, None]
            ctx = ctx + m_ctx[5][:, None] * _expert_choice_ffn(h2c, router_w[l], ew)
    return x
```

```python
import functools

import numpy as np
import jax
import jax.numpy as jnp
from jax import lax
from jax.experimental import pallas as pl
from jax.experimental.pallas import tpu as pltpu

F32 = jnp.float32
BF16 = jnp.bfloat16

I32 = jnp.int32

GRID_W = 64
CONV_W = 5
POOL_WINDOWS = (2, 4, 8, 16)
SSD_HEAD_DIM = 64
SSD_GROUPS = 2
SSD_STATE = 128
GDN_HEAD_DIM = 64
N_EXPERTS = 16
EC_CAPACITY = 2
EPS = 1e-6

LANE = 128
SUBLANES = 8
VMEM_LIMIT = 48 * 1024 * 1024

ROW_TILE = 512
WIDE_TILE = 1024
SCAN_CHUNK = 256
HALO = SUBLANES
PIECE = SUBLANES
NEG = -1e30


def _params():
    return pltpu.CompilerParams(dimension_semantics=("arbitrary", "arbitrary"), vmem_limit_bytes=VMEM_LIMIT)


def _inproj_kernel(x_ref, sc_ref, sh_ref, g_ref, w_ref, *o_refs, transpose_grid):
    x = x_ref[0]
    tm = x.shape[0]
    ms = jnp.mean(x * x, axis=-1, keepdims=True)
    h = x * lax.rsqrt(ms + EPS) * g_ref[...]
    h = (h * (1.0 + sc_ref[0]) + sh_ref[0]).astype(BF16)
    if transpose_grid:
        p = lax.broadcasted_iota(jnp.int32, (tm, tm), 0)
        q = lax.broadcasted_iota(jnp.int32, (tm, tm), 1)
        perm = (q == (p % SUBLANES) * GRID_W + p // SUBLANES).astype(BF16)
        h = jnp.dot(perm, h, preferred_element_type=F32).astype(BF16)
    y = jnp.dot(h, w_ref[...], preferred_element_type=F32)
    off = 0
    for o_ref in o_refs:
        n = o_ref.shape[-1]
        if transpose_grid:
            o_ref[0] = y[:, off:off + n].reshape(GRID_W, SUBLANES, n)
        else:
            o_ref[0] = y[:, off:off + n]
        off += n


def _inproj(x, scale, shift, g, w, widths, *, column_major):
    b, t, d = x.shape
    rows = t // GRID_W
    if column_major:
        assert rows % SUBLANES == 0
        tm = SUBLANES * GRID_W
        out_specs = [pl.BlockSpec((1, GRID_W, SUBLANES, n), lambda i, j: (i, 0, j, 0)) for n in widths]
        out_shape = [jax.ShapeDtypeStruct((b, GRID_W, rows, n), F32) for n in widths]
    else:
        tm = min(t, WIDE_TILE)
        out_specs = [pl.BlockSpec((1, tm, n), lambda i, j: (i, j, 0)) for n in widths]
        out_shape = [jax.ShapeDtypeStruct((b, t, n), F32) for n in widths]
    outs = pl.pallas_call(
        functools.partial(_inproj_kernel, transpose_grid=column_major),
        grid=(b, t // tm),
        in_specs=[
            pl.BlockSpec((1, tm, d), lambda i, j: (i, j, 0)),
            pl.BlockSpec((1, 1, d), lambda i, j: (i, 0, 0)),
            pl.BlockSpec((1, 1, d), lambda i, j: (i, 0, 0)),
            pl.BlockSpec((1, d), lambda i, j: (0, 0)),
            pl.BlockSpec((d, w.shape[1]), lambda i, j: (0, 0)),
        ],
        out_specs=out_specs,
        out_shape=out_shape,
        compiler_params=_params(),
        name="inproj",
    )(x, scale, shift, g, w)
    return [o.reshape(b, t, n) for o, n in zip(outs, widths)]


def _outproj_kernel(*refs, n_in):
    a_refs = refs[:n_in]
    x_ref, gate_ref, w_ref, o_ref = refs[n_in:]
    a = jnp.concatenate([a_ref[0].astype(BF16) for a_ref in a_refs], axis=-1)
    y = jnp.dot(a, w_ref[...], preferred_element_type=F32)
    o_ref[0] = x_ref[0] + gate_ref[0] * y


def _outproj(parts, x, gate, w):
    b, t, d = x.shape
    tm = min(t, ROW_TILE)
    n_in = len(parts)
    return pl.pallas_call(
        functools.partial(_outproj_kernel, n_in=n_in),
        grid=(b, t // tm),
        in_specs=(
            [pl.BlockSpec((1, tm, p.shape[-1]), lambda i, j: (i, j, 0)) for p in parts]
            + [pl.BlockSpec((1, tm, d), lambda i, j: (i, j, 0)), pl.BlockSpec((1, 1, d), lambda i, j: (i, 0, 0)),
               pl.BlockSpec(w.shape, lambda i, j: (0, 0))]),
        out_specs=pl.BlockSpec((1, tm, d), lambda i, j: (i, j, 0)),
        out_shape=jax.ShapeDtypeStruct((b, t, d), F32),
        compiler_params=_params(),
        name="outproj",
    )(*parts, x, gate, w)


def _pool_kernel(u_ref, prev_ref, next_ref, w_ref, sc_ref, o_ref, e0_ref, e1_ref, *, block, seq, group):
    TM = block
    j = pl.program_id(1)
    nb = pl.num_programs(1)
    u = u_ref[0]
    n_ext = TM + 2 * HALO
    e0_ref[0:HALO, :] = jnp.where(j > 0, prev_ref[0], 0.0)
    e0_ref[HALO:HALO + TM, :] = u
    e0_ref[HALO + TM:, :] = jnp.where(j < nb - 1, next_ref[0], 0.0)
    tok = j * TM + lax.broadcasted_iota(jnp.int32, (TM, 1), 0)
    lane = lax.broadcasted_iota(jnp.int32, (1, u.shape[-1]), 1)
    src, dst = e0_ref, e1_ref
    pooled = jnp.zeros_like(u)
    half = 1
    for gi, win in enumerate(POOL_WINDOWS):
        assert win == 2 * half
        lo_r, hi_r = half, n_ext - half
        if half == 1:
            dst[lo_r:hi_r, :] = src[lo_r - 1:hi_r - 1, :] + src[lo_r:hi_r, :]
        else:
            q = half // 2
            dst[lo_r:hi_r, :] = src[lo_r - q:hi_r - q, :] + src[lo_r + q:hi_r + q, :]
        cnt = (jnp.minimum(tok + half, seq) - jnp.maximum(tok - half, 0)).astype(F32)
        mean = dst[HALO:HALO + TM, :] / cnt
        pooled = jnp.where((lane >= gi * group) & (lane < (gi + 1) * group), mean, pooled)
        src, dst = dst, src
        half *= 2
    dd = pooled - u
    y = jnp.dot(dd.astype(BF16), w_ref[...], preferred_element_type=F32)
    o_ref[0] = y * sc_ref[...]


def _pool_branch(u, pool_w, pool_scale):
    b, t, c = u.shape
    ng, pg, _ = pool_w.shape
    tm = min(t, ROW_TILE)
    hb = tm // HALO
    w_bd = jnp.zeros((c, c), F32)
    for gi in range(ng):
        w_bd = w_bd.at[gi * pg:(gi + 1) * pg, gi * pg:(gi + 1) * pg].set(pool_w[gi])
    return pl.pallas_call(
        functools.partial(_pool_kernel, block=tm, seq=t, group=pg),
        grid=(b, t // tm),
        in_specs=[
            pl.BlockSpec((1, tm, c), lambda i, j: (i, j, 0)),
            pl.BlockSpec((1, HALO, c), lambda i, j: (i, jnp.maximum(j * hb - 1, 0), 0)),
            pl.BlockSpec((1, HALO, c), lambda i, j: (i, jnp.minimum((j + 1) * hb, t // HALO - 1), 0)),
            pl.BlockSpec((c, c), lambda i, j: (0, 0)),
            pl.BlockSpec((1, c), lambda i, j: (0, 0)),
        ],
        out_specs=pl.BlockSpec((1, tm, c), lambda i, j: (i, j, 0)),
        out_shape=jax.ShapeDtypeStruct((b, t, c), F32),
        scratch_shapes=[pltpu.VMEM((tm + 2 * HALO, c), F32), pltpu.VMEM((tm + 2 * HALO, c), F32)],
        compiler_params=_params(),
        name="pool",
    )(u, u, u, w_bd.astype(BF16), pool_scale[None])


def _ffn_kernel(x_ref, gate_ref, wg_ref, wu_ref, wd_ref, o_ref, wgb_ref, wub_ref, wdb_ref):
    @pl.when(pl.program_id(1) == 0)
    def _():
        wgb_ref[...] = wg_ref[0].astype(BF16)
        wub_ref[...] = wu_ref[0].astype(BF16)
        wdb_ref[...] = wd_ref[0].astype(BF16)

    x = x_ref[0].astype(BF16)
    hg = jnp.dot(x, wgb_ref[...], preferred_element_type=F32)
    hu = jnp.dot(x, wub_ref[...], preferred_element_type=F32)
    hid = (hg * jax.nn.sigmoid(hg)) * hu
    y = jnp.dot(hid.astype(BF16), wdb_ref[...], preferred_element_type=F32)
    o_ref[0] = y * gate_ref[0]


def _expert_ffn(xe, gate, wg, wu, wd, layer):
    be, c, d = xe.shape
    _, e, _, f = wg.shape
    tc = min(c, WIDE_TILE)
    return pl.pallas_call(
        _ffn_kernel,
        grid=(be, c // tc),
        in_specs=[
            pl.BlockSpec((1, tc, d), lambda i, j: (i, j, 0)),
            pl.BlockSpec((1, tc, 1), lambda i, j: (i, j, 0)),
            pl.BlockSpec((None, 1, d, f), lambda i, j: (layer, i % e, 0, 0)),
            pl.BlockSpec((None, 1, d, f), lambda i, j: (layer, i % e, 0, 0)),
            pl.BlockSpec((None, 1, f, d), lambda i, j: (layer, i % e, 0, 0)),
        ],
        out_specs=pl.BlockSpec((1, tc, d), lambda i, j: (i, j, 0)),
        out_shape=jax.ShapeDtypeStruct((be, c, d), F32),
        scratch_shapes=[pltpu.VMEM((d, f), BF16), pltpu.VMEM((d, f), BF16), pltpu.VMEM((f, d), BF16)],
        compiler_params=_params(),
        name="expert_ffn",
    )(xe, gate, wg, wu, wd)


def _split3(a):
    hi = a.astype(BF16)
    r1 = a - hi.astype(F32)
    mid = r1.astype(BF16)
    lo = (r1 - mid.astype(F32)).astype(BF16)
    return hi, mid, lo


def _cumsum_both(la, incl):
    parts = _split3(la)
    cs_col = sum(jnp.dot(incl, p, preferred_element_type=F32) for p in parts)
    cs_row = sum(lax.dot_general(p, incl, (((0,), (1,)), ((), ())), preferred_element_type=F32) for p in parts)
    return cs_col, cs_row


def _ssd_kernel(*refs, chunk, rev, final, heads):
    if final:
        (xbc_ref, prev_ref, next_ref, dt_ref, cw_ref, cb_ref, dtb_ref, a_ref, dsk_ref, s0_ref,
         yb_ref, z_ref, ng_ref, y_ref, sfin_ref, s_ref, ext_ref) = refs
    else:
        (xbc_ref, prev_ref, next_ref, dt_ref, cw_ref, cb_ref, dtb_ref, a_ref, dsk_ref, s0_ref,
         y_ref, sfin_ref, act_ref, s_ref, ext_ref) = refs
    L = chunk
    i = pl.program_id(1)
    nc = pl.num_programs(1)
    j = nc - 1 - i if rev else i
    hd = SSD_HEAD_DIM
    ssd_dim = heads * hd
    rep = heads // SSD_GROUPS

    @pl.when(i == 0)
    def _():
        s_ref[...] = s0_ref[0]

    if final:
        act = xbc_ref[0]
    else:
        ext_ref[0:HALO, :] = jnp.where(j > 0, prev_ref[0], 0.0)
        ext_ref[HALO:HALO + L, :] = xbc_ref[0]
        ext_ref[HALO + L:, :] = jnp.where(j < nc - 1, next_ref[0], 0.0)
        base = HALO - CONV_W // 2
        acc = cb_ref[...] + cw_ref[0:1, :] * ext_ref[base:base + L, :]
        for k in range(1, CONV_W):
            acc = acc + cw_ref[k:k + 1, :] * ext_ref[base + k:base + k + L, :]
        act = acc * jax.nn.sigmoid(acc)
        act_ref[0] = act
    xs = act[:, :ssd_dim]
    bmat = act[:, ssd_dim:ssd_dim + SSD_GROUPS * SSD_STATE]
    cmat = act[:, ssd_dim + SSD_GROUPS * SSD_STATE:]

    dtv = jax.nn.softplus(dt_ref[0] + dtb_ref[...])
    la = dtv * a_ref[...]
    row = lax.broadcasted_iota(jnp.int32, (L, L), 0)
    col = lax.broadcasted_iota(jnp.int32, (L, L), 1)
    mask = (row <= col) if rev else (row >= col)
    cs_col, cs_row = _cumsum_both(la, mask.astype(BF16))
    last = 0 if rev else L - 1

    gmats = []
    for g in range(SSD_GROUPS):
        cg = cmat[:, g * SSD_STATE:(g + 1) * SSD_STATE].astype(BF16)
        bg = bmat[:, g * SSD_STATE:(g + 1) * SSD_STATE].astype(BF16)
        gmats.append(lax.dot_general(cg, bg, (((1,), (1,)), ((), ())), preferred_element_type=F32))

    hs = range(heads)
    cis = [(heads if rev else 0) + h for h in hs]
    b_g = [bmat[:, g * SSD_STATE:(g + 1) * SSD_STATE] for g in range(SSD_GROUPS)]
    c_g = [cmat[:, g * SSD_STATE:(g + 1) * SSD_STATE] for g in range(SSD_GROUPS)]
    csc = [cs_col[:, ci:ci + 1] for ci in cis]
    tot = [cc[last:last + 1, :] for cc in csc]
    dec = [jnp.exp(jnp.where(mask, csc[h] - cs_row[cis[h]:cis[h] + 1, :], NEG)) for h in hs]
    xs_h = [xs[:, h * hd:(h + 1) * hd] for h in hs]
    xdt = [(xs_h[h] * dtv[:, cis[h]:cis[h] + 1]).astype(BF16) for h in hs]
    s_old = [s_ref[h] for h in hs]
    y_intra = [jnp.dot((gmats[h // rep] * dec[h]).astype(BF16), xdt[h], preferred_element_type=F32) for h in hs]
    y_inter = [jnp.dot((c_g[h // rep] * jnp.exp(csc[h])).astype(BF16), s_old[h].astype(BF16),
                       preferred_element_type=F32) for h in hs]
    local = [lax.dot_general((b_g[h // rep] * jnp.exp(tot[h] - csc[h])).astype(BF16), xdt[h],
                             (((0,), (0,)), ((), ())), preferred_element_type=F32) for h in hs]
    for h in hs:
        s_ref[h] = s_old[h] * jnp.exp(tot[h]) + local[h]
    ys = [y_intra[h] + y_inter[h] for h in hs]
    if final:
        ys = [ys[h] + dsk_ref[:, h * hd:(h + 1) * hd] * xs_h[h] for h in hs]
    y = jnp.concatenate(ys, axis=-1)
    if final:
        y = y + yb_ref[0]
        z = z_ref[0]
        y = y * (z * jax.nn.sigmoid(z))
        y = y * lax.rsqrt(jnp.mean(y * y, axis=-1, keepdims=True) + EPS) * ng_ref[...]
    y_ref[0] = y

    @pl.when(i == nc - 1)
    def _():
        sfin_ref[0] = s_ref[...]


def _ssd_pass(xbc, dt, cw, cb, dtb, a, dsk, s0, *, rev, final_inputs=None):
    b, t, width = xbc.shape
    heads = s0.shape[1]
    ssd_dim = heads * SSD_HEAD_DIM
    L = min(t, SCAN_CHUNK)
    nc = t // L
    hb = L // HALO
    final = final_inputs is not None

    def cidx(i):
        return nc - 1 - i if rev else i

    in_specs = [
        pl.BlockSpec((1, L, width), lambda bi, i: (bi, cidx(i), 0)),
        pl.BlockSpec((1, HALO, width), lambda bi, i: (bi, jnp.maximum(cidx(i) * hb - 1, 0), 0)),
        pl.BlockSpec((1, HALO, width), lambda bi, i: (bi, jnp.minimum((cidx(i) + 1) * hb, t // HALO - 1), 0)),
        pl.BlockSpec((1, L, LANE), lambda bi, i: (bi, cidx(i), 0)),
        pl.BlockSpec((8, width), lambda bi, i: (0, 0)),
        pl.BlockSpec((1, width), lambda bi, i: (0, 0)),
        pl.BlockSpec((1, LANE), lambda bi, i: (0, 0)),
        pl.BlockSpec((1, LANE), lambda bi, i: (0, 0)),
        pl.BlockSpec((1, ssd_dim), lambda bi, i: (0, 0)),
        pl.BlockSpec((1, heads, SSD_STATE, SSD_HEAD_DIM), lambda bi, i: (bi, 0, 0, 0)),
    ]
    args = [xbc, xbc, xbc, dt, cw, cb, dtb, a, dsk, s0]
    if final:
        yb, z, ng = final_inputs
        in_specs += [
            pl.BlockSpec((1, L, ssd_dim), lambda bi, i: (bi, cidx(i), 0)),
            pl.BlockSpec((1, L, ssd_dim), lambda bi, i: (bi, cidx(i), 0)),
            pl.BlockSpec((1, ssd_dim), lambda bi, i: (0, 0)),
        ]
        args += [yb, z, ng]
    out_specs = [
        pl.BlockSpec((1, L, ssd_dim), lambda bi, i: (bi, cidx(i), 0)),
        pl.BlockSpec((1, heads, SSD_STATE, SSD_HEAD_DIM), lambda bi, i: (bi, 0, 0, 0)),
    ]
    out_shape = [
        jax.ShapeDtypeStruct((b, t, ssd_dim), F32),
        jax.ShapeDtypeStruct((b, heads, SSD_STATE, SSD_HEAD_DIM), F32),
    ]
    if not final:
        out_specs.append(pl.BlockSpec((1, L, width), lambda bi, i: (bi, cidx(i), 0)))
        out_shape.append(jax.ShapeDtypeStruct((b, t, width), F32))
    return pl.pallas_call(
        functools.partial(_ssd_kernel, chunk=L, rev=rev, final=final, heads=heads),
        grid=(b, nc),
        in_specs=in_specs,
        out_specs=out_specs,
        out_shape=out_shape,
        scratch_shapes=[
            pltpu.VMEM((heads, SSD_STATE, SSD_HEAD_DIM), F32),
            pltpu.VMEM((L + 2 * HALO, width), F32),
        ],
        compiler_params=_params(),
        name="ssd_bwd" if rev else "ssd_fwd",
    )(*args)


def _ssd_stream(z, xbc, dt, conv_w, conv_b, a_log, dt_bias, d_skip, norm_g, s0_f, s0_b):
    heads = s0_f.shape[1]
    cw = jnp.pad(conv_w, ((0, 8 - CONV_W), (0, 0)))
    cb = conv_b[None]
    dtb = jnp.pad(dt_bias.reshape(1, -1), ((0, 0), (0, LANE - 2 * heads)))
    a = jnp.pad(-jnp.exp(a_log).reshape(1, -1), ((0, 0), (0, LANE - 2 * heads)))
    dsk = jnp.repeat(d_skip, SSD_HEAD_DIM)[None]
    yb, s_b, act = _ssd_pass(xbc, dt, cw, cb, dtb, a, dsk, s0_b, rev=True)
    y, s_f = _ssd_pass(act, dt, cw, cb, dtb, a, dsk, s0_f, rev=False, final_inputs=(yb, z, norm_g[None]))
    return y, s_f, s_b


def _group_sum(a, ones_bd):
    hi = a.astype(BF16)
    lo = (a - hi.astype(F32)).astype(BF16)
    return (jnp.dot(hi, ones_bd, preferred_element_type=F32) + jnp.dot(lo, ones_bd, preferred_element_type=F32))


def _mm_bf16(a, b):
    return jnp.dot(a.astype(BF16), b.astype(BF16), preferred_element_type=F32)


def _unit_tri_inverse_many(xms, rev, size=None):
    n = xms[0].shape[0]
    row = lax.broadcasted_iota(jnp.int32, (n, n), 0)
    col = lax.broadcasted_iota(jnp.int32, (n, n), 1)
    eye = (row == col).astype(F32)
    ds = None
    m, sh = 1, 0
    while m < (size or n):
        same = (row >> (sh + 1)) == (col >> (sh + 1))
        rbit = (row >> sh) & 1
        cbit = (col >> sh) & 1
        sel = same & ((rbit == 0) & (cbit == 1) if rev else (rbit == 1) & (cbit == 0))
        cs = [jnp.where(sel, x, 0.0) for x in xms]
        if ds is None:
            ds = [eye - c for c in cs]
        elif m < SUBLANES:
            es = [_mm_bf16(c, d) for c, d in zip(cs, ds)]
            ds = [d - _mm_bf16(d, e) for d, e in zip(ds, es)]
        else:
            act = 0 if rev else 1

            def rows(a, which):
                return a.reshape(n // (2 * m), 2, m, n)[:, which].reshape(n // 2, n)

            def merge(keep, new):
                pair = (new, keep) if rev else (keep, new)
                return jnp.stack([p.reshape(n // (2 * m), m, n) for p in pair], axis=1).reshape(n, n)

            es = [_mm_bf16(rows(c, act), d) for c, d in zip(cs, ds)]
            zero = jnp.zeros((n // 2, n), F32)
            ds = [merge(rows(d, 1 - act), rows(d, act) - _mm_bf16(rows(d, act), merge(zero, e)))
                  for d, e in zip(ds, es)]
        m, sh = 2 * m, sh + 1
    return ds


def _gdn_kernel(*refs, block, rev, final, heads):
    if final:
        (qkv_ref, prev_ref, next_ref, ab_ref, cw_ref, dtb_ref, a_ref, s0_ref, ob_ref, gate_ref, ng_ref,
         o_ref, sfin_ref, s_ref, ext_ref) = refs
    else:
        (qkv_ref, prev_ref, next_ref, ab_ref, cw_ref, dtb_ref, a_ref, s0_ref,
         o_ref, sfin_ref, act_ref, s_ref, ext_ref) = refs
    TB = block
    L = block
    hd = GDN_HEAD_DIM
    dim = heads * hd
    i = pl.program_id(1)
    nb = pl.num_programs(1)
    j = nb - 1 - i if rev else i

    @pl.when(i == 0)
    def _():
        s_ref[...] = s0_ref[0]

    ri = lax.broadcasted_iota(jnp.int32, (dim, dim), 0) // hd
    ci_ = lax.broadcasted_iota(jnp.int32, (dim, dim), 1) // hd
    ones_bd = (ri == ci_).astype(BF16)
    if final:
        act = qkv_ref[0]
        q = act[:, :dim]
        k = act[:, dim:2 * dim]
        v = act[:, 2 * dim:]
    else:
        ext_ref[0:HALO, :] = jnp.where(j > 0, prev_ref[0], 0.0)
        ext_ref[HALO:HALO + TB, :] = qkv_ref[0]
        ext_ref[HALO + TB:, :] = jnp.where(j < nb - 1, next_ref[0], 0.0)
        base = HALO - CONV_W // 2
        acc = cw_ref[0:1, :] * ext_ref[base:base + TB, :]
        for kk in range(1, CONV_W):
            acc = acc + cw_ref[kk:kk + 1, :] * ext_ref[base + kk:base + kk + TB, :]
        act = acc * jax.nn.sigmoid(acc)
        q = act[:, :dim]
        k = act[:, dim:2 * dim]
        v = act[:, 2 * dim:]
        q = q * lax.rsqrt(_group_sum(q * q, ones_bd) + EPS) * (hd ** -0.5)
        k = k * lax.rsqrt(_group_sum(k * k, ones_bd) + EPS)
        act_ref[0] = jnp.concatenate([q, k, v], axis=-1)
    ab = ab_ref[0]
    gl = a_ref[...] * jax.nn.softplus(ab + dtb_ref[...])
    beta = jax.nn.sigmoid(ab)

    row = lax.broadcasted_iota(jnp.int32, (L, L), 0)
    col = lax.broadcasted_iota(jnp.int32, (L, L), 1)
    incl = (row <= col) if rev else (row >= col)
    strict = (row < col) if rev else (row > col)
    last = 0 if rev else L - 1
    cs_col, cs_row = _cumsum_both(gl, incl.astype(BF16))

    hs = range(heads)
    cis = [(heads if rev else 0) + h for h in hs]
    qh = [q[:, h * hd:(h + 1) * hd] for h in hs]
    kh = [k[:, h * hd:(h + 1) * hd] for h in hs]
    vh = [v[:, h * hd:(h + 1) * hd] for h in hs]
    csc = [cs_col[:, ci:ci + 1] for ci in cis]
    bcol = [beta[:, 2 * heads + ci:2 * heads + ci + 1] for ci in cis]
    qk_kk = [lax.dot_general(jnp.concatenate([qh[h], kh[h]], axis=0).astype(BF16), kh[h].astype(BF16),
                             (((1,), (1,)), ((), ())), preferred_element_type=F32) for h in hs]
    dec = [jnp.exp(jnp.where(incl, csc[h] - cs_row[cis[h]:cis[h] + 1, :], NEG)) for h in hs]
    attn = [(qk_kk[h][:L] * dec[h]).astype(BF16) for h in hs]
    xm = [jnp.where(strict, bcol[h] * qk_kk[h][L:] * dec[h], 0.0) for h in hs]
    egc = [jnp.exp(cc) for cc in csc]
    rhs = [jnp.concatenate([vh[h] * bcol[h], kh[h] * (bcol[h] * egc[h])], axis=1).astype(BF16) for h in hs]
    if L >= 2 * LANE:
        hb = L // 2
        tm = _unit_tri_inverse_many(xm, rev, hb)
        (f0, f1), (g0, g1) = ((hb, L), (0, hb)) if rev else ((0, hb), (hb, L))
        u1 = [jnp.dot(tm[h][f0:f1, f0:f1].astype(BF16), rhs[h][f0:f1], preferred_element_type=F32) for h in hs]
        r2 = [rhs[h][g0:g1].astype(F32) - jnp.dot(xm[h][g0:g1, f0:f1].astype(BF16), u1[h].astype(BF16),
                                                    preferred_element_type=F32) for h in hs]
        u2 = [jnp.dot(tm[h][g0:g1, g0:g1].astype(BF16), r2[h].astype(BF16), preferred_element_type=F32) for h in hs]
        uw = [jnp.concatenate([u2[h], u1[h]] if rev else [u1[h], u2[h]], axis=0) for h in hs]
    else:
        tm = _unit_tri_inverse_many(xm, rev)
        uw = [jnp.dot(tm[h].astype(BF16), rhs[h], preferred_element_type=F32) for h in hs]
    s_old = [s_ref[h] for h in hs]
    rs = [None] * heads
    zblk = jnp.zeros((hd, hd), F32)
    lane2 = lax.broadcasted_iota(jnp.int32, (1, 2 * hd), 1)
    for p in range(heads // 2):
        a, b = 2 * p, 2 * p + 1
        w_pair = jnp.concatenate([uw[a][:, hd:], uw[b][:, hd:]], axis=1)
        q_pair = q[:, a * hd:(b + 1) * hd] * jnp.where(lane2 < hd, egc[a], egc[b])
        s_bd = jnp.concatenate([jnp.concatenate([s_old[a], zblk], axis=1),
                                jnp.concatenate([zblk, s_old[b]], axis=1)], axis=0).astype(BF16)
        r = jnp.dot(jnp.concatenate([w_pair, q_pair], axis=0).astype(BF16), s_bd, preferred_element_type=F32)
        rs[a] = r[:, :hd]
        rs[b] = r[:, hd:]
    vnb = [(uw[h][:, :hd] - rs[h][:L]).astype(BF16) for h in hs]
    outs = [rs[h][L:] + jnp.dot(attn[h], vnb[h], preferred_element_type=F32) for h in hs]
    for h in hs:
        tot = csc[h][last:last + 1, :]
        kend = (kh[h] * jnp.exp(tot - csc[h])).astype(BF16)
        s_ref[h] = s_old[h] * jnp.exp(tot) + lax.dot_general(kend, vnb[h], (((0,), (0,)), ((), ())),
                                                            preferred_element_type=F32)
    o = jnp.concatenate(outs, axis=-1)
    if final:
        o = o + ob_ref[0]
        ms = _group_sum(o * o, ones_bd) * (1.0 / hd)
        gate = gate_ref[0]
        o = o * lax.rsqrt(ms + EPS) * ng_ref[...] * (gate * jax.nn.sigmoid(gate))
    o_ref[0] = o

    @pl.when(i == nb - 1)
    def _():
        sfin_ref[0] = s_ref[...]


def _gdn_pass(qkv, ab, cw, dtb, a, s0, *, rev, block, final_inputs=None, raster_out=False):
    b, t, width = qkv.shape
    heads = s0.shape[1]
    assert heads % 2 == 0
    dim = heads * GDN_HEAD_DIM
    TB = block
    nb = t // TB
    hb = TB // HALO
    final = final_inputs is not None

    def bidx(i):
        return nb - 1 - i if rev else i

    in_specs = [
        pl.BlockSpec((1, TB, width), lambda bi, i: (bi, bidx(i), 0)),
        pl.BlockSpec((1, HALO, width), lambda bi, i: (bi, jnp.maximum(bidx(i) * hb - 1, 0), 0)),
        pl.BlockSpec((1, HALO, width), lambda bi, i: (bi, jnp.minimum((bidx(i) + 1) * hb, t // HALO - 1), 0)),
        pl.BlockSpec((1, TB, LANE), lambda bi, i: (bi, bidx(i), 0)),
        pl.BlockSpec((8, width), lambda bi, i: (0, 0)),
        pl.BlockSpec((1, LANE), lambda bi, i: (0, 0)),
        pl.BlockSpec((1, LANE), lambda bi, i: (0, 0)),
        pl.BlockSpec((1, heads, GDN_HEAD_DIM, GDN_HEAD_DIM), lambda bi, i: (bi, 0, 0, 0)),
    ]
    args = [qkv, qkv, qkv, ab, cw, dtb, a, s0]
    if final:
        ob, gate, ng = final_inputs
        in_specs += [
            pl.BlockSpec((1, TB, dim), lambda bi, i: (bi, bidx(i), 0)),
            pl.BlockSpec((1, TB, dim), lambda bi, i: (bi, bidx(i), 0)),
            pl.BlockSpec((1, dim), lambda bi, i: (0, 0)),
        ]
        args += [ob, gate, ng]
    if raster_out:
        assert nb == GRID_W
        o_spec = pl.BlockSpec((1, TB, dim), lambda bi, i: (bi, 0, bidx(i)))
        o_shape = jax.ShapeDtypeStruct((b, TB, GRID_W * dim), F32)
    else:
        o_spec = pl.BlockSpec((1, TB, dim), lambda bi, i: (bi, bidx(i), 0))
        o_shape = jax.ShapeDtypeStruct((b, t, dim), F32)
    out_specs = [o_spec, pl.BlockSpec((1, heads, GDN_HEAD_DIM, GDN_HEAD_DIM), lambda bi, i: (bi, 0, 0, 0))]
    out_shape = [o_shape, jax.ShapeDtypeStruct((b, heads, GDN_HEAD_DIM, GDN_HEAD_DIM), F32)]
    if not final:
        out_specs.append(pl.BlockSpec((1, TB, width), lambda bi, i: (bi, bidx(i), 0)))
        out_shape.append(jax.ShapeDtypeStruct((b, t, width), F32))
    outs = pl.pallas_call(
        functools.partial(_gdn_kernel, block=TB, rev=rev, final=final, heads=heads),
        grid=(b, nb),
        in_specs=in_specs,
        out_specs=out_specs,
        out_shape=out_shape,
        scratch_shapes=[
            pltpu.VMEM((heads, GDN_HEAD_DIM, GDN_HEAD_DIM), F32),
            pltpu.VMEM((TB + 2 * HALO, width), F32),
        ],
        compiler_params=_params(),
        name="gdn_bwd" if rev else "gdn_fwd",
    )(*args)
    return (outs[0].reshape(b, t, dim),) + tuple(outs[1:])


def _gdn_stream(qkv, gate, ab, conv_w, a_log, dt_bias, norm_g, s0_f, s0_b, *, column_major):
    t = qkv.shape[1]
    heads = s0_f.shape[1]
    cw = jnp.pad(conv_w, ((0, 8 - CONV_W), (0, 0)))
    dtb = jnp.pad(dt_bias.reshape(1, -1), ((0, 0), (0, LANE - 2 * heads)))
    a = jnp.pad(-jnp.exp(a_log).reshape(1, -1), ((0, 0), (0, LANE - 2 * heads)))
    ng = jnp.tile(norm_g, heads)[None]
    rows = t // GRID_W
    fused_raster = column_major and rows <= SCAN_CHUNK and rows >= 2 * HALO and rows & (rows - 1) == 0
    block = rows if fused_raster else min(t, SCAN_CHUNK)
    ob, s_b, act = _gdn_pass(qkv, ab, cw, dtb, a, s0_b, rev=True, block=block)
    o, s_f = _gdn_pass(act, ab, cw, dtb, a, s0_f, rev=False, block=block, final_inputs=(ob, gate, ng),
                       raster_out=fused_raster)
    if column_major and not fused_raster:
        o = _to_raster(o)
    return o, s_f, s_b


def _router_kernel(x_ref, sc_ref, sh_ref, g_ref, rw_ref, h_ref, a_ref):
    x = x_ref[0]
    ms = jnp.mean(x * x, axis=-1, keepdims=True)
    h = x * lax.rsqrt(ms + EPS) * g_ref[...]
    h = h * (1.0 + sc_ref[0]) + sh_ref[0]
    hh, hm, _ = _split3(h)
    h_ref[0] = hh
    rw = rw_ref[...]
    rh = rw.astype(BF16)
    rm = (rw - rh.astype(F32)).astype(BF16)
    nt = (((1,), (1,)), ((), ()))
    lg = (lax.dot_general(rh, hh, nt, preferred_element_type=F32)
          + lax.dot_general(rh, hm, nt, preferred_element_type=F32)
          + lax.dot_general(rm, hh, nt, preferred_element_type=F32))
    ex = jnp.exp(lg - jnp.max(lg, axis=0, keepdims=True))
    aff = ex / jnp.sum(ex, axis=0, keepdims=True)
    for k in range(a_ref.shape[1]):
        a_ref[0, k] = aff[:, k * LANE:(k + 1) * LANE]


def _router(x, scale, shift, g, rw_t):
    b, t, d = x.shape
    e = rw_t.shape[0]
    tm = min(t, WIDE_TILE)
    return pl.pallas_call(
        _router_kernel,
        grid=(b, t // tm),
        in_specs=[
            pl.BlockSpec((1, tm, d), lambda i, j: (i, j, 0)),
            pl.BlockSpec((1, 1, d), lambda i, j: (i, 0, 0)),
            pl.BlockSpec((1, 1, d), lambda i, j: (i, 0, 0)),
            pl.BlockSpec((1, d), lambda i, j: (0, 0)),
            pl.BlockSpec((e, d), lambda i, j: (0, 0)),
        ],
        out_specs=[
            pl.BlockSpec((1, tm, d), lambda i, j: (i, j, 0)),
            pl.BlockSpec((1, tm // LANE, e, LANE), lambda i, j: (i, j, 0, 0)),
        ],
        out_shape=[
            jax.ShapeDtypeStruct((b, t, d), BF16),
            jax.ShapeDtypeStruct((b, t // LANE, e, LANE), F32),
        ],
        compiler_params=_params(),
        name="router",
    )(x, scale, shift, g, rw_t)


def _token_prefix(m3, ut, ones, lt):
    e, nb, _ = m3.shape
    m2 = m3.reshape(e * nb, LANE).astype(BF16)
    inb = jnp.dot(m2, ut, preferred_element_type=F32).reshape(e, nb, LANE)
    tot = jnp.dot(m2, ones, preferred_element_type=F32).reshape(e, nb, LANE)
    offs = jnp.stack([jnp.dot(lt, tot[i].astype(BF16), preferred_element_type=F32) for i in range(e)], axis=0)
    return inb, tot, offs


def _select_kernel(a_ref, idx_ref, gate_ref, srow_ref, st8_ref, npc_ref, cs_ref, cum_ref, cnt_ref, *, cap):
    a = a_ref[0]
    e_n, nb, _ = a.shape
    bits = lax.bitcast_convert_type(a, I32)

    def radix(i, prefix):
        cand = prefix | jnp.left_shift(jnp.int32(1), 30 - i)
        cnt = jnp.sum(jnp.sum((bits >= cand).astype(F32), axis=2, keepdims=True), axis=1, keepdims=True)
        return jnp.where(cnt >= cap, cand, prefix)

    thr = lax.fori_loop(0, 31, radix, jnp.zeros((e_n, 1, 1), I32))
    li = lax.broadcasted_iota(I32, (LANE, LANE), 0)
    lj = lax.broadcasted_iota(I32, (LANE, LANE), 1)
    ut = (li < lj).astype(BF16)
    ones = jnp.ones((LANE, LANE), BF16)
    bi = lax.broadcasted_iota(I32, (nb, nb), 0)
    bj = lax.broadcasted_iota(I32, (nb, nb), 1)
    lt = (bj < bi).astype(BF16)

    gt = bits > thr
    eq = bits == thr
    n_gt = jnp.sum(jnp.sum(gt.astype(F32), axis=2, keepdims=True), axis=1, keepdims=True)
    tie_in, _, tie_offs = _token_prefix(eq.astype(F32), ut, ones, lt)
    sel = gt | (eq & (tie_in + tie_offs < cap - n_gt))
    pos_in, cnt, offs = _token_prefix(sel.astype(F32), ut, ones, lt)
    cs_ref[...] = jnp.where(sel, pos_in + 1.0, 0.0)
    cum_ref[...] = offs + cnt
    cnt_ref[...] = cnt

    offs_i = offs.astype(I32)
    cnt_i = cnt.astype(I32)
    st8 = (offs_i >> 3) << 3
    npc = jnp.where(cnt_i > 0, (offs_i + cnt_i - st8 + (PIECE - 1)) >> 3, 0)
    rbase = []
    run = jnp.zeros((nb, LANE), I32)
    for i in range(e_n):
        rbase.append(run)
        run = run + PIECE * npc[i]
    rbase = jnp.stack(rbase, axis=0)
    srow_ref[0] = jnp.where(sel, rbase + pos_in.astype(I32) + offs_i - st8, -1)
    st8_ref[0] = st8
    npc_ref[0] = npc

    jrow = lax.broadcasted_iota(I32, (1, cap), 1).astype(F32)
    sub_nb = lax.broadcasted_iota(I32, (nb, cap), 0).astype(F32)
    sub_l = lax.broadcasted_iota(I32, (LANE, cap), 0).astype(F32)
    tn = (((0,), (0,)), ((), ()))

    def compact(ei, carry):
        cum_col = cum_ref[ei][:, 0:1]
        cnt_col = cnt_ref[ei][:, 0:1]
        ge = cum_col <= jrow
        blk_j = jnp.sum(ge.astype(F32), axis=0, keepdims=True)
        offs_j = jnp.sum(jnp.where(ge, cnt_col, 0.0), axis=0, keepdims=True)
        rank1 = jrow - offs_j + 1.0
        g_t = (sub_nb == blk_j).astype(BF16)
        row_t = lax.dot_general(cs_ref[ei].astype(BF16), g_t, tn, preferred_element_type=F32)
        match = row_t == rank1
        lane_j = jnp.sum(jnp.where(match, sub_l, 0.0), axis=0, keepdims=True)
        idx_ref[0, pl.ds(ei, 1), :] = (blk_j * LANE + lane_j).astype(I32)
        parts = _split3(a_ref[0, ei])
        aff_t = sum(lax.dot_general(p, g_t, tn, preferred_element_type=F32) for p in parts)
        gate_ref[0, pl.ds(ei, 1), :] = jnp.sum(jnp.where(match, aff_t, 0.0), axis=0, keepdims=True)
        return carry

    lax.fori_loop(0, e_n, compact, 0)


def _ec_select(aff_em, cap):
    b, e, nb, _ = aff_em.shape
    big = lambda dt: jax.ShapeDtypeStruct((b, e, nb, LANE), dt)
    spec4 = pl.BlockSpec((1, e, nb, LANE), lambda i: (i, 0, 0, 0))
    spec3 = pl.BlockSpec((1, e, cap), lambda i: (i, 0, 0))
    return pl.pallas_call(
        functools.partial(_select_kernel, cap=cap),
        grid=(b,),
        in_specs=[spec4],
        out_specs=[spec3, spec3, spec4, spec4, spec4],
        out_shape=[jax.ShapeDtypeStruct((b, e, cap), I32), jax.ShapeDtypeStruct((b, e, cap), F32),
                   big(I32), big(I32), big(I32)],
        scratch_shapes=[pltpu.VMEM((e, nb, LANE), F32)] * 3,
        compiler_params=pltpu.CompilerParams(dimension_semantics=("arbitrary",), vmem_limit_bytes=VMEM_LIMIT),
        name="ec_select",
    )(aff_em)


def _combine_kernel(st8_sm, npc_sm, ye_hbm, srow_ref, x_ref, g_ref, fn_ref, o_ref, stage, acc_ref, sem,
                    *, final, n_exp):
    b = pl.program_id(0)
    k = pl.program_id(1)
    nb = pl.num_programs(1)
    step = b * nb + k
    nsteps = pl.num_programs(0) * nb
    slot = step % 2

    def run_copy(bb, e, src_row, sl, dst_row, pieces):
        size = pieces * PIECE
        return pltpu.make_async_copy(ye_hbm.at[bb, e, pl.ds(pl.multiple_of(src_row, PIECE), size), :],
                                     stage.at[sl, pl.ds(pl.multiple_of(dst_row, PIECE), size), :], sem.at[sl])

    def issue(st, sl):
        bb = st // nb

        def per_e(e, r):
            s8 = st8_sm[st * n_exp + e]
            n = npc_sm[st * n_exp + e]

            def quad(p, r2):
                run_copy(bb, e, s8 + 4 * PIECE * p, sl, r2, 4).start()
                return r2 + 4 * PIECE

            r = lax.fori_loop(0, n >> 2, quad, r)
            done = (n >> 2) << 2

            @pl.when((n & 2) != 0)
            def _():
                run_copy(bb, e, s8 + PIECE * done, sl, r, 2).start()

            r = r + PIECE * (n & 2)
            done = done + (n & 2)

            @pl.when((n & 1) != 0)
            def _():
                run_copy(bb, e, s8 + PIECE * done, sl, r, 1).start()

            return r + PIECE * (n & 1)

        lax.fori_loop(0, n_exp, per_e, 0)

    @pl.when(step == 0)
    def _():
        issue(step, slot)

    @pl.when(step + 1 < nsteps)
    def _():
        issue(step + 1, 1 - slot)

    npieces = lax.fori_loop(0, n_exp, lambda e, s: s + npc_sm[step * n_exp + e], 0)
    rows = npieces * PIECE

    @pl.when(npieces > 0)
    def _():
        pltpu.make_async_copy(ye_hbm.at[0, 0, pl.ds(0, rows), :], stage.at[slot, pl.ds(0, rows), :],
                              sem.at[slot]).wait()
    stage[slot, pl.ds(pl.multiple_of(rows, PIECE), LANE), :] = jnp.zeros((LANE, stage.shape[-1]), F32)

    srow = srow_ref[0, 0]
    acc_ref[...] = jnp.zeros_like(acc_ref)
    riota = lax.broadcasted_iota(I32, (LANE, LANE), 0)
    tn = (((0,), (0,)), ((), ()))

    def chunk(c, carry):
        r0 = pl.multiple_of(c * LANE, LANE)
        rid = riota + r0
        pt = (srow[0:1, :] == rid).astype(F32)
        for e in range(1, n_exp):
            pt = pt + (srow[e:e + 1, :] == rid).astype(F32)
        ptb = pt.astype(BF16)
        st = stage[slot, pl.ds(r0, LANE), :]
        hi = st.astype(BF16)
        lo = (st - hi.astype(F32)).astype(BF16)
        acc_ref[...] += lax.dot_general(jnp.concatenate([ptb, ptb], axis=0), jnp.concatenate([hi, lo], axis=0), tn,
                                        preferred_element_type=F32)
        return carry

    lax.fori_loop(0, (rows + LANE - 1) // LANE, chunk, 0)
    y = x_ref[0] + g_ref[0] * acc_ref[...]
    if final:
        y = y * lax.rsqrt(jnp.mean(y * y, axis=-1, keepdims=True) + EPS) * fn_ref[...]
    o_ref[0] = y


def _ec_combine(ye, srow_bm, st8, npc, x, gate, final_g=None):
    b, t, d = x.shape
    e = ye.shape[1]
    nb = t // LANE
    final = final_g is not None
    fn = final_g if final else jnp.ones((1, d), F32)
    max_rows = e * (LANE + 2 * PIECE) + LANE
    grid_spec = pltpu.PrefetchScalarGridSpec(
        num_scalar_prefetch=2,
        grid=(b, nb),
        in_specs=[
            pl.BlockSpec(memory_space=pl.ANY),
            pl.BlockSpec((1, 1, e, LANE), lambda i, j, *_: (i, j, 0, 0)),
            pl.BlockSpec((1, LANE, d), lambda i, j, *_: (i, j, 0)),
            pl.BlockSpec((1, 1, d), lambda i, j, *_: (i, 0, 0)),
            pl.BlockSpec((1, d), lambda i, j, *_: (0, 0)),
        ],
        out_specs=pl.BlockSpec((1, LANE, d), lambda i, j, *_: (i, j, 0)),
        scratch_shapes=[
            pltpu.VMEM((2, max_rows, d), F32),
            pltpu.VMEM((LANE, d), F32),
            pltpu.SemaphoreType.DMA((2,)),
        ],
    )
    return pl.pallas_call(
        functools.partial(_combine_kernel, final=final, n_exp=e),
        grid_spec=grid_spec,
        out_shape=jax.ShapeDtypeStruct((b, t, d), F32),
        compiler_params=_params(),
        name="ec_combine",
    )(st8, npc, ye, srow_bm, x, gate, fn)


def _expert_choice_block(x, scale, shift, g2, gate2, router_w, ew, final_g):
    b, t, d = x.shape
    e = router_w.shape[-1]
    cap = EC_CAPACITY * t // e
    h2, aff_bm = _router(x, scale, shift, g2, router_w.T)
    idx, gate, srow, st8, npc = _ec_select(jnp.transpose(aff_bm, (0, 2, 1, 3)), cap)
    xe = jax.vmap(lambda hb, ib: hb[ib])(h2, idx)
    ye = _expert_ffn(xe.reshape(b * e, cap, d), gate.reshape(b * e, cap, 1), *ew).reshape(b, e, cap, d)
    srow_bm = jnp.transpose(srow, (0, 2, 1, 3))
    st8_f = jnp.transpose(st8[..., 0], (0, 2, 1)).reshape(-1)
    npc_f = jnp.transpose(npc[..., 0], (0, 2, 1)).reshape(-1)
    return _ec_combine(ye, srow_bm, st8_f, npc_f, x, gate2, final_g)


def _rms_norm(x, g):
    return x * lax.rsqrt(jnp.mean(x * x, axis=-1, keepdims=True) + EPS) * g


def _to_raster(u):
    b, t, c = u.shape
    rows = t // GRID_W
    return u.reshape(b, GRID_W, rows, c).transpose(0, 2, 1, 3).reshape(b, t, c)


def _expert_choice_ffn(h, router_w, ew):
    b, t, d = h.shape
    cap = EC_CAPACITY * t // N_EXPERTS
    aff = jax.nn.softmax(jnp.einsum('btd,de->bte', h, router_w, precision=lax.Precision.HIGHEST), axis=-1)
    gate, idx = lax.top_k(jnp.swapaxes(aff, 1, 2), cap)
    xe = jax.vmap(lambda hb, ib: hb[ib])(h, idx)
    ye = _expert_ffn(xe.reshape(b * N_EXPERTS, cap, d), gate.reshape(b * N_EXPERTS, cap, 1), *ew)
    ye = ye.reshape(b, N_EXPERTS, cap, d)
    return jax.vmap(lambda ib, yb: jnp.zeros((t, d), yb.dtype).at[ib.reshape(-1)].add(yb.reshape(-1, d)))(idx, ye)


def kernel(x, c, ctx, c_ctx, norm1_g, norm2_g, ada_w, ada_b, w_in, w_out, pool_w, pool_scale, ssd_conv_w, ssd_conv_b, ssd_a_log, ssd_dt_bias, ssd_d, ssd_norm_g, gdn_conv_w, gdn_a_log, gdn_dt_bias, gdn_norm_g, router_w, exp_w_gate, exp_w_up, exp_w_down, final_norm_g):
    depth, d, _ = w_in.shape
    b, t, _ = x.shape
    pool_dim = pool_scale.shape[-1]
    ssd_dim = ssd_norm_g.shape[-1]
    ssd_heads = ssd_dim // SSD_HEAD_DIM
    ssd_bc = SSD_GROUPS * SSD_STATE
    gdn_dim = gdn_conv_w.shape[-1] // 3
    gdn_heads = gdn_dim // GDN_HEAD_DIM
    splits = (pool_dim, ssd_dim, ssd_dim + 2 * ssd_bc, 2 * ssd_heads, 3 * gdn_dim, gdn_dim, 2 * gdn_heads,
              2 * gdn_heads)
    cut = [0] + np.cumsum(splits).tolist()
    r_widths = (ssd_dim, ssd_dim + 2 * ssd_bc, pool_dim, LANE)
    g_widths = (3 * gdn_dim, gdn_dim, LANE)

    assert depth >= 1
    sc = jax.nn.silu(c)
    scc = jax.nn.silu(c_ctx)[None]
    for l in range(depth):
        last = l == depth - 1
        wl = w_in[l]
        seg = [wl[:, cut[i]:cut[i + 1]] for i in range(8)]
        w_r = jnp.concatenate(
            [seg[1], seg[2], seg[0], jnp.pad(seg[3], ((0, 0), (0, LANE - 2 * ssd_heads)))], axis=1).astype(BF16)
        w_g = jnp.concatenate(
            [seg[4], seg[5], jnp.pad(jnp.concatenate([seg[6], seg[7]], axis=1), ((0, 0), (0, LANE - 4 * gdn_heads)))],
            axis=1).astype(BF16)
        w_o = w_out[l].astype(BF16)
        ew = (exp_w_gate, exp_w_up, exp_w_down, l)
        m_lat = jnp.split(sc @ ada_w[l] + ada_b[l], 6, axis=-1)
        m_ctx = [jnp.broadcast_to(m, (b, d)) for m in jnp.split(scc @ ada_w[l] + ada_b[l], 6, axis=-1)]
        g1 = norm1_g[l][None]

        def project(xx, mm, column_major):
            pr = _inproj(xx, mm[1][:, None], mm[0][:, None], g1, w_r, r_widths, column_major=False)
            pg = _inproj(xx, mm[1][:, None], mm[0][:, None], g1, w_g, g_widths, column_major=column_major)
            return pr, pg

        ssd_p = (ssd_conv_w[l], ssd_conv_b[l], ssd_a_log[l], ssd_dt_bias[l], ssd_d[l], ssd_norm_g[l])
        gdn_p = (gdn_conv_w[l], gdn_a_log[l], gdn_dt_bias[l], gdn_norm_g[l])
        zs = jnp.zeros((b, ssd_heads, SSD_STATE, SSD_HEAD_DIM), F32)
        zg = jnp.zeros((b, gdn_heads, GDN_HEAD_DIM, GDN_HEAD_DIM), F32)

        (c_z, c_xbc, c_pool, c_dt), (c_qkv, c_gate, c_ab) = project(ctx, m_ctx, False)
        (l_z, l_xbc, l_pool, l_dt), (l_qkv, l_gate, l_ab) = project(x, m_lat, True)

        s_ctx, ssd_sf, ssd_sb = _ssd_stream(c_z, c_xbc, c_dt, *ssd_p, zs, zs)
        g_ctx, gdn_sf, gdn_sb = _gdn_stream(c_qkv, c_gate, c_ab, *gdn_p, zg, zg, column_major=False)
        s_lat, _, _ = _ssd_stream(l_z, l_xbc, l_dt, *ssd_p, ssd_sf, ssd_sb)
        g_lat, _, _ = _gdn_stream(l_qkv, l_gate, l_ab, *gdn_p, gdn_sf, gdn_sb, column_major=True)
        x = _outproj([_pool_branch(l_pool, pool_w[l], pool_scale[l]), s_lat, g_lat], x, m_lat[2][:, None], w_o)
        fin = final_norm_g[None] if last else None
        if t % WIDE_TILE == 0:
            x = _expert_choice_block(x, m_lat[4][:, None], m_lat[3][:, None], norm2_g[l][None], m_lat[5][:, None],
                                     router_w[l], ew, fin)
        else:
            h2 = _rms_norm(x, norm2_g[l]) * (1 + m_lat[4][:, None]) + m_lat[3][:, None]
            x = x + m_lat[5][:, None] * _expert_choice_ffn(h2, router_w[l], ew)
            if last:
                x = _rms_norm(x, final_norm_g)
        if not last:
            ctx = _outproj([_pool_branch(c_pool, pool_w[l], pool_scale[l]), s_ctx, g_ctx], ctx, m_ctx[2][:, None],
                           w_o)
            h2c = _rms_norm(ctx, norm2_g[l]) * (1 + m_ctx[4][:, None]) + m_ctx[3][:, None]
            ctx = ctx + m_ctx[5][:, None] * _expert_choice_ffn(h2c, router_w[l], ew)
    return x
```

```python
import functools

import numpy as np
import jax
import jax.numpy as jnp
from jax import lax
from jax.experimental import pallas as pl
from jax.experimental.pallas import tpu as pltpu

F32 = jnp.float32
BF16 = jnp.bfloat16

I32 = jnp.int32

GRID_W = 64
CONV_W = 5
POOL_WINDOWS = (2, 4, 8, 16)
SSD_HEAD_DIM = 64
SSD_GROUPS = 2
SSD_STATE = 128
GDN_HEAD_DIM = 64
N_EXPERTS = 16
EC_CAPACITY = 2
EPS = 1e-6

LANE = 128
SUBLANES = 8
VMEM_LIMIT = 48 * 1024 * 1024

ROW_TILE = 512
WIDE_TILE = 1024
SCAN_CHUNK = 256
HALO = SUBLANES
PIECE = SUBLANES
NEG = -1e30


def _params():
    return pltpu.CompilerParams(dimension_semantics=("arbitrary", "arbitrary"), vmem_limit_bytes=VMEM_LIMIT)


def _inproj_kernel(x_ref, sc_ref, sh_ref, g_ref, w_ref, *o_refs, transpose_grid):
    x = x_ref[0]
    tm = x.shape[0]
    ms = jnp.mean(x * x, axis=-1, keepdims=True)
    h = x * lax.rsqrt(ms + EPS) * g_ref[...]
    h = (h * (1.0 + sc_ref[0]) + sh_ref[0]).astype(BF16)
    if transpose_grid:
        p = lax.broadcasted_iota(jnp.int32, (tm, tm), 0)
        q = lax.broadcasted_iota(jnp.int32, (tm, tm), 1)
        perm = (q == (p % SUBLANES) * GRID_W + p // SUBLANES).astype(BF16)
        h = jnp.dot(perm, h, preferred_element_type=F32).astype(BF16)
    y = jnp.dot(h, w_ref[...], preferred_element_type=F32)
    off = 0
    for o_ref in o_refs:
        n = o_ref.shape[-1]
        if transpose_grid:
            o_ref[0] = y[:, off:off + n].reshape(GRID_W, SUBLANES, n)
        else:
            o_ref[0] = y[:, off:off + n]
        off += n


def _inproj(x, scale, shift, g, w, widths, *, column_major):
    b, t, d = x.shape
    rows = t // GRID_W
    if column_major:
        assert rows % SUBLANES == 0
        tm = SUBLANES * GRID_W
        out_specs = [pl.BlockSpec((1, GRID_W, SUBLANES, n), lambda i, j: (i, 0, j, 0)) for n in widths]
        out_shape = [jax.ShapeDtypeStruct((b, GRID_W, rows, n), F32) for n in widths]
    else:
        tm = min(t, WIDE_TILE)
        out_specs = [pl.BlockSpec((1, tm, n), lambda i, j: (i, j, 0)) for n in widths]
        out_shape = [jax.ShapeDtypeStruct((b, t, n), F32) for n in widths]
    outs = pl.pallas_call(
        functools.partial(_inproj_kernel, transpose_grid=column_major),
        grid=(b, t // tm),
        in_specs=[
            pl.BlockSpec((1, tm, d), lambda i, j: (i, j, 0)),
            pl.BlockSpec((1, 1, d), lambda i, j: (i, 0, 0)),
            pl.BlockSpec((1, 1, d), lambda i, j: (i, 0, 0)),
            pl.BlockSpec((1, d), lambda i, j: (0, 0)),
            pl.BlockSpec((d, w.shape[1]), lambda i, j: (0, 0)),
        ],
        out_specs=out_specs,
        out_shape=out_shape,
        compiler_params=_params(),
        name="inproj",
    )(x, scale, shift, g, w)
    return [o.reshape(b, t, n) for o, n in zip(outs, widths)]


def _outproj_kernel(*refs, n_in):
    a_refs = refs[:n_in]
    x_ref, gate_ref, w_ref, o_ref = refs[n_in:]
    a = jnp.concatenate([a_ref[0].astype(BF16) for a_ref in a_refs], axis=-1)
    y = jnp.dot(a, w_ref[...], preferred_element_type=F32)
    o_ref[0] = x_ref[0] + gate_ref[0] * y


def _outproj(parts, x, gate, w):
    b, t, d = x.shape
    tm = min(t, ROW_TILE)
    n_in = len(parts)
    return pl.pallas_call(
        functools.partial(_outproj_kernel, n_in=n_in),
        grid=(b, t // tm),
        in_specs=(
            [pl.BlockSpec((1, tm, p.shape[-1]), lambda i, j: (i, j, 0)) for p in parts]
            + [pl.BlockSpec((1, tm, d), lambda i, j: (i, j, 0)), pl.BlockSpec((1, 1, d), lambda i, j: (i, 0, 0)),
               pl.BlockSpec(w.shape, lambda i, j: (0, 0))]),
        out_specs=pl.BlockSpec((1, tm, d), lambda i, j: (i, j, 0)),
        out_shape=jax.ShapeDtypeStruct((b, t, d), F32),
        compiler_params=_params(),
        name="outproj",
    )(*parts, x, gate, w)


def _pool_kernel(u_ref, prev_ref, next_ref, w_ref, sc_ref, o_ref, e0_ref, e1_ref, *, block, seq, group):
    TM = block
    j = pl.program_id(1)
    nb = pl.num_programs(1)
    u = u_ref[0]
    n_ext = TM + 2 * HALO
    e0_ref[0:HALO, :] = jnp.where(j > 0, prev_ref[0], 0.0)
    e0_ref[HALO:HALO + TM, :] = u
    e0_ref[HALO + TM:, :] = jnp.where(j < nb - 1, next_ref[0], 0.0)
    tok = j * TM + lax.broadcasted_iota(jnp.int32, (TM, 1), 0)
    lane = lax.broadcasted_iota(jnp.int32, (1, u.shape[-1]), 1)
    src, dst = e0_ref, e1_ref
    pooled = jnp.zeros_like(u)
    half = 1
    for gi, win in enumerate(POOL_WINDOWS):
        assert win == 2 * half
        lo_r, hi_r = half, n_ext - half
        if half == 1:
            dst[lo_r:hi_r, :] = src[lo_r - 1:hi_r - 1, :] + src[lo_r:hi_r, :]
        else:
            q = half // 2
            dst[lo_r:hi_r, :] = src[lo_r - q:hi_r - q, :] + src[lo_r + q:hi_r + q, :]
        cnt = (jnp.minimum(tok + half, seq) - jnp.maximum(tok - half, 0)).astype(F32)
        mean = dst[HALO:HALO + TM, :] / cnt
        pooled = jnp.where((lane >= gi * group) & (lane < (gi + 1) * group), mean, pooled)
        src, dst = dst, src
        half *= 2
    dd = pooled - u
    y = jnp.dot(dd.astype(BF16), w_ref[...], preferred_element_type=F32)
    o_ref[0] = y * sc_ref[...]


def _pool_branch(u, pool_w, pool_scale):
    b, t, c = u.shape
    ng, pg, _ = pool_w.shape
    tm = min(t, ROW_TILE)
    hb = tm // HALO
    w_bd = jnp.zeros((c, c), F32)
    for gi in range(ng):
        w_bd = w_bd.at[gi * pg:(gi + 1) * pg, gi * pg:(gi + 1) * pg].set(pool_w[gi])
    return pl.pallas_call(
        functools.partial(_pool_kernel, block=tm, seq=t, group=pg),
        grid=(b, t // tm),
        in_specs=[
            pl.BlockSpec((1, tm, c), lambda i, j: (i, j, 0)),
            pl.BlockSpec((1, HALO, c), lambda i, j: (i, jnp.maximum(j * hb - 1, 0), 0)),
            pl.BlockSpec((1, HALO, c), lambda i, j: (i, jnp.minimum((j + 1) * hb, t // HALO - 1), 0)),
            pl.BlockSpec((c, c), lambda i, j: (0, 0)),
            pl.BlockSpec((1, c), lambda i, j: (0, 0)),
        ],
        out_specs=pl.BlockSpec((1, tm, c), lambda i, j: (i, j, 0)),
        out_shape=jax.ShapeDtypeStruct((b, t, c), F32),
        scratch_shapes=[pltpu.VMEM((tm + 2 * HALO, c), F32), pltpu.VMEM((tm + 2 * HALO, c), F32)],
        compiler_params=_params(),
        name="pool",
    )(u, u, u, w_bd.astype(BF16), pool_scale[None])


def _ffn_kernel(x_ref, gate_ref, wg_ref, wu_ref, wd_ref, o_ref, wgb_ref, wub_ref, wdb_ref):
    @pl.when(pl.program_id(1) == 0)
    def _():
        wgb_ref[...] = wg_ref[0].astype(BF16)
        wub_ref[...] = wu_ref[0].astype(BF16)
        wdb_ref[...] = wd_ref[0].astype(BF16)

    x = x_ref[0].astype(BF16)
    hg = jnp.dot(x, wgb_ref[...], preferred_element_type=F32)
    hu = jnp.dot(x, wub_ref[...], preferred_element_type=F32)
    hid = (hg * jax.nn.sigmoid(hg)) * hu
    y = jnp.dot(hid.astype(BF16), wdb_ref[...], preferred_element_type=F32)
    o_ref[0] = y * gate_ref[0]


def _expert_ffn(xe, gate, wg, wu, wd, layer):
    be, c, d = xe.shape
    _, e, _, f = wg.shape
    tc = min(c, WIDE_TILE)
    return pl.pallas_call(
        _ffn_kernel,
        grid=(be, c // tc),
        in_specs=[
            pl.BlockSpec((1, tc, d), lambda i, j: (i, j, 0)),
            pl.BlockSpec((1, tc, 1), lambda i, j: (i, j, 0)),
            pl.BlockSpec((None, 1, d, f), lambda i, j: (layer, i % e, 0, 0)),
            pl.BlockSpec((None, 1, d, f), lambda i, j: (layer, i % e, 0, 0)),
            pl.BlockSpec((None, 1, f, d), lambda i, j: (layer, i % e, 0, 0)),
        ],
        out_specs=pl.BlockSpec((1, tc, d), lambda i, j: (i, j, 0)),
        out_shape=jax.ShapeDtypeStruct((be, c, d), F32),
        scratch_shapes=[pltpu.VMEM((d, f), BF16), pltpu.VMEM((d, f), BF16), pltpu.VMEM((f, d), BF16)],
        compiler_params=_params(),
        name="expert_ffn",
    )(xe, gate, wg, wu, wd)


def _split3(a):
    hi = a.astype(BF16)
    r1 = a - hi.astype(F32)
    mid = r1.astype(BF16)
    lo = (r1 - mid.astype(F32)).astype(BF16)
    return hi, mid, lo


def _cumsum_both(la, incl):
    parts = _split3(la)
    cs_col = sum(jnp.dot(incl, p, preferred_element_type=F32) for p in parts)
    cs_row = sum(lax.dot_general(p, incl, (((0,), (1,)), ((), ())), preferred_element_type=F32) for p in parts)
    return cs_col, cs_row


def _ssd_kernel(*refs, chunk, rev, final, heads):
    if final:
        (xbc_ref, prev_ref, next_ref, dt_ref, cw_ref, cb_ref, dtb_ref, a_ref, dsk_ref, s0_ref,
         yb_ref, z_ref, ng_ref, y_ref, sfin_ref, s_ref, ext_ref) = refs
    else:
        (xbc_ref, prev_ref, next_ref, dt_ref, cw_ref, cb_ref, dtb_ref, a_ref, dsk_ref, s0_ref,
         y_ref, sfin_ref, act_ref, s_ref, ext_ref) = refs
    L = chunk
    i = pl.program_id(1)
    nc = pl.num_programs(1)
    j = nc - 1 - i if rev else i
    hd = SSD_HEAD_DIM
    ssd_dim = heads * hd
    rep = heads // SSD_GROUPS

    @pl.when(i == 0)
    def _():
        s_ref[...] = s0_ref[0]

    if final:
        act = xbc_ref[0]
    else:
        ext_ref[0:HALO, :] = jnp.where(j > 0, prev_ref[0], 0.0)
        ext_ref[HALO:HALO + L, :] = xbc_ref[0]
        ext_ref[HALO + L:, :] = jnp.where(j < nc - 1, next_ref[0], 0.0)
        base = HALO - CONV_W // 2
        acc = cb_ref[...] + cw_ref[0:1, :] * ext_ref[base:base + L, :]
        for k in range(1, CONV_W):
            acc = acc + cw_ref[k:k + 1, :] * ext_ref[base + k:base + k + L, :]
        act = acc * jax.nn.sigmoid(acc)
        act_ref[0] = act
    xs = act[:, :ssd_dim]
    bmat = act[:, ssd_dim:ssd_dim + SSD_GROUPS * SSD_STATE]
    cmat = act[:, ssd_dim + SSD_GROUPS * SSD_STATE:]

    dtv = jax.nn.softplus(dt_ref[0] + dtb_ref[...])
    la = dtv * a_ref[...]
    row = lax.broadcasted_iota(jnp.int32, (L, L), 0)
    col = lax.broadcasted_iota(jnp.int32, (L, L), 1)
    mask = (row <= col) if rev else (row >= col)
    cs_col, cs_row = _cumsum_both(la, mask.astype(BF16))
    last = 0 if rev else L - 1

    gmats = []
    for g in range(SSD_GROUPS):
        cg = cmat[:, g * SSD_STATE:(g + 1) * SSD_STATE].astype(BF16)
        bg = bmat[:, g * SSD_STATE:(g + 1) * SSD_STATE].astype(BF16)
        gmats.append(lax.dot_general(cg, bg, (((1,), (1,)), ((), ())), preferred_element_type=F32))

    hs = range(heads)
    cis = [(heads if rev else 0) + h for h in hs]
    b_g = [bmat[:, g * SSD_STATE:(g + 1) * SSD_STATE] for g in range(SSD_GROUPS)]
    c_g = [cmat[:, g * SSD_STATE:(g + 1) * SSD_STATE] for g in range(SSD_GROUPS)]
    csc = [cs_col[:, ci:ci + 1] for ci in cis]
    tot = [cc[last:last + 1, :] for cc in csc]
    dec = [jnp.exp(jnp.where(mask, csc[h] - cs_row[cis[h]:cis[h] + 1, :], NEG)) for h in hs]
    xs_h = [xs[:, h * hd:(h + 1) * hd] for h in hs]
    xdt = [(xs_h[h] * dtv[:, cis[h]:cis[h] + 1]).astype(BF16) for h in hs]
    s_old = [s_ref[h] for h in hs]
    y_intra = [jnp.dot((gmats[h // rep] * dec[h]).astype(BF16), xdt[h], preferred_element_type=F32) for h in hs]
    y_inter = [jnp.dot((c_g[h // rep] * jnp.exp(csc[h])).astype(BF16), s_old[h].astype(BF16),
                       preferred_element_type=F32) for h in hs]
    local = [lax.dot_general((b_g[h // rep] * jnp.exp(tot[h] - csc[h])).astype(BF16), xdt[h],
                             (((0,), (0,)), ((), ())), preferred_element_type=F32) for h in hs]
    for h in hs:
        s_ref[h] = s_old[h] * jnp.exp(tot[h]) + local[h]
    ys = [y_intra[h] + y_inter[h] for h in hs]
    if final:
        ys = [ys[h] + dsk_ref[:, h * hd:(h + 1) * hd] * xs_h[h] for h in hs]
    y = jnp.concatenate(ys, axis=-1)
    if final:
        y = y + yb_ref[0]
        z = z_ref[0]
        y = y * (z * jax.nn.sigmoid(z))
        y = y * lax.rsqrt(jnp.mean(y * y, axis=-1, keepdims=True) + EPS) * ng_ref[...]
    y_ref[0] = y

    @pl.when(i == nc - 1)
    def _():
        sfin_ref[0] = s_ref[...]


def _ssd_pass(xbc, dt, cw, cb, dtb, a, dsk, s0, *, rev, final_inputs=None):
    b, t, width = xbc.shape
    heads = s0.shape[1]
    ssd_dim = heads * SSD_HEAD_DIM
    L = min(t, SCAN_CHUNK)
    nc = t // L
    hb = L // HALO
    final = final_inputs is not None

    def cidx(i):
        return nc - 1 - i if rev else i

    in_specs = [
        pl.BlockSpec((1, L, width), lambda bi, i: (bi, cidx(i), 0)),
        pl.BlockSpec((1, HALO, width), lambda bi, i: (bi, jnp.maximum(cidx(i) * hb - 1, 0), 0)),
        pl.BlockSpec((1, HALO, width), lambda bi, i: (bi, jnp.minimum((cidx(i) + 1) * hb, t // HALO - 1), 0)),
        pl.BlockSpec((1, L, LANE), lambda bi, i: (bi, cidx(i), 0)),
        pl.BlockSpec((8, width), lambda bi, i: (0, 0)),
        pl.BlockSpec((1, width), lambda bi, i: (0, 0)),
        pl.BlockSpec((1, LANE), lambda bi, i: (0, 0)),
        pl.BlockSpec((1, LANE), lambda bi, i: (0, 0)),
        pl.BlockSpec((1, ssd_dim), lambda bi, i: (0, 0)),
        pl.BlockSpec((1, heads, SSD_STATE, SSD_HEAD_DIM), lambda bi, i: (bi, 0, 0, 0)),
    ]
    args = [xbc, xbc, xbc, dt, cw, cb, dtb, a, dsk, s0]
    if final:
        yb, z, ng = final_inputs
        in_specs += [
            pl.BlockSpec((1, L, ssd_dim), lambda bi, i: (bi, cidx(i), 0)),
            pl.BlockSpec((1, L, ssd_dim), lambda bi, i: (bi, cidx(i), 0)),
            pl.BlockSpec((1, ssd_dim), lambda bi, i: (0, 0)),
        ]
        args += [yb, z, ng]
    out_specs = [
        pl.BlockSpec((1, L, ssd_dim), lambda bi, i: (bi, cidx(i), 0)),
        pl.BlockSpec((1, heads, SSD_STATE, SSD_HEAD_DIM), lambda bi, i: (bi, 0, 0, 0)),
    ]
    out_shape = [
        jax.ShapeDtypeStruct((b, t, ssd_dim), F32),
        jax.ShapeDtypeStruct((b, heads, SSD_STATE, SSD_HEAD_DIM), F32),
    ]
    if not final:
        out_specs.append(pl.BlockSpec((1, L, width), lambda bi, i: (bi, cidx(i), 0)))
        out_shape.append(jax.ShapeDtypeStruct((b, t, width), F32))
    return pl.pallas_call(
        functools.partial(_ssd_kernel, chunk=L, rev=rev, final=final, heads=heads),
        grid=(b, nc),
        in_specs=in_specs,
        out_specs=out_specs,
        out_shape=out_shape,
        scratch_shapes=[
            pltpu.VMEM((heads, SSD_STATE, SSD_HEAD_DIM), F32),
            pltpu.VMEM((L + 2 * HALO, width), F32),
        ],
        compiler_params=_params(),
        name="ssd_bwd" if rev else "ssd_fwd",
    )(*args)


def _ssd_stream(z, xbc, dt, conv_w, conv_b, a_log, dt_bias, d_skip, norm_g, s0_f, s0_b):
    heads = s0_f.shape[1]
    cw = jnp.pad(conv_w, ((0, 8 - CONV_W), (0, 0)))
    cb = conv_b[None]
    dtb = jnp.pad(dt_bias.reshape(1, -1), ((0, 0), (0, LANE - 2 * heads)))
    a = jnp.pad(-jnp.exp(a_log).reshape(1, -1), ((0, 0), (0, LANE - 2 * heads)))
    dsk = jnp.repeat(d_skip, SSD_HEAD_DIM)[None]
    yb, s_b, act = _ssd_pass(xbc, dt, cw, cb, dtb, a, dsk, s0_b, rev=True)
    y, s_f = _ssd_pass(act, dt, cw, cb, dtb, a, dsk, s0_f, rev=False, final_inputs=(yb, z, norm_g[None]))
    return y, s_f, s_b


def _group_sum(a, ones_bd):
    hi = a.astype(BF16)
    lo = (a - hi.astype(F32)).astype(BF16)
    outs = []
    for g in range(a.shape[-1] // LANE):
        sl = slice(g * LANE, (g + 1) * LANE)
        outs.append(jnp.dot(hi[:, sl], ones_bd, preferred_element_type=F32)
                    + jnp.dot(lo[:, sl], ones_bd, preferred_element_type=F32))
    return jnp.concatenate(outs, axis=-1)


def _mm_bf16(a, b):
    return jnp.dot(a.astype(BF16), b.astype(BF16), preferred_element_type=F32)


def _unit_tri_inverse_many(xms, rev, size=None):
    n = xms[0].shape[0]
    row = lax.broadcasted_iota(jnp.int32, (n, n), 0)
    col = lax.broadcasted_iota(jnp.int32, (n, n), 1)
    eye = (row == col).astype(F32)
    ds = None
    m, sh = 1, 0
    while m < (size or n):
        same = (row >> (sh + 1)) == (col >> (sh + 1))
        rbit = (row >> sh) & 1
        cbit = (col >> sh) & 1
        sel = same & ((rbit == 0) & (cbit == 1) if rev else (rbit == 1) & (cbit == 0))
        cs = [jnp.where(sel, x, 0.0) for x in xms]
        if ds is None:
            ds = [eye - c for c in cs]
        elif m < SUBLANES:
            es = [_mm_bf16(c, d) for c, d in zip(cs, ds)]
            ds = [d - _mm_bf16(d, e) for d, e in zip(ds, es)]
        else:
            act = 0 if rev else 1

            def rows(a, which):
                return a.reshape(n // (2 * m), 2, m, n)[:, which].reshape(n // 2, n)

            def merge(keep, new):
                pair = (new, keep) if rev else (keep, new)
                return jnp.stack([p.reshape(n // (2 * m), m, n) for p in pair], axis=1).reshape(n, n)

            es = [_mm_bf16(rows(c, act), d) for c, d in zip(cs, ds)]
            zero = jnp.zeros((n // 2, n), F32)
            ds = [merge(rows(d, 1 - act), rows(d, act) - _mm_bf16(rows(d, act), merge(zero, e)))
                  for d, e in zip(ds, es)]
        m, sh = 2 * m, sh + 1
    return ds


def _gdn_kernel(*refs, block, rev, final, heads):
    if final:
        (qkv_ref, prev_ref, next_ref, ab_ref, cw_ref, dtb_ref, a_ref, s0_ref, ob_ref, gate_ref, ng_ref,
         o_ref, sfin_ref, s_ref, ext_ref) = refs
    else:
        (qkv_ref, prev_ref, next_ref, ab_ref, cw_ref, dtb_ref, a_ref, s0_ref,
         o_ref, sfin_ref, act_ref, s_ref, ext_ref) = refs
    TB = block
    L = block
    hd = GDN_HEAD_DIM
    dim = heads * hd
    i = pl.program_id(1)
    nb = pl.num_programs(1)
    j = nb - 1 - i if rev else i

    @pl.when(i == 0)
    def _():
        s_ref[...] = s0_ref[0]

    ri = lax.broadcasted_iota(jnp.int32, (LANE, LANE), 0) // hd
    ci_ = lax.broadcasted_iota(jnp.int32, (LANE, LANE), 1) // hd
    ones_bd = (ri == ci_).astype(BF16)
    if final:
        act = qkv_ref[0]
        q = act[:, :dim]
        k = act[:, dim:2 * dim]
        v = act[:, 2 * dim:]
    else:
        ext_ref[0:HALO, :] = jnp.where(j > 0, prev_ref[0], 0.0)
        ext_ref[HALO:HALO + TB, :] = qkv_ref[0]
        ext_ref[HALO + TB:, :] = jnp.where(j < nb - 1, next_ref[0], 0.0)
        base = HALO - CONV_W // 2
        acc = cw_ref[0:1, :] * ext_ref[base:base + TB, :]
        for kk in range(1, CONV_W):
            acc = acc + cw_ref[kk:kk + 1, :] * ext_ref[base + kk:base + kk + TB, :]
        act = acc * jax.nn.sigmoid(acc)
        q = act[:, :dim]
        k = act[:, dim:2 * dim]
        v = act[:, 2 * dim:]
        q = q * lax.rsqrt(_group_sum(q * q, ones_bd) + EPS) * (hd ** -0.5)
        k = k * lax.rsqrt(_group_sum(k * k, ones_bd) + EPS)
        act_ref[0] = jnp.concatenate([q, k, v], axis=-1)
    ab = ab_ref[0]
    gl = a_ref[...] * jax.nn.softplus(ab + dtb_ref[...])
    beta = jax.nn.sigmoid(ab)

    row = lax.broadcasted_iota(jnp.int32, (L, L), 0)
    col = lax.broadcasted_iota(jnp.int32, (L, L), 1)
    incl = (row <= col) if rev else (row >= col)
    strict = (row < col) if rev else (row > col)
    last = 0 if rev else L - 1
    cs_col, cs_row = _cumsum_both(gl, incl.astype(BF16))

    hs = range(heads)
    cis = [(heads if rev else 0) + h for h in hs]
    qh = [q[:, h * hd:(h + 1) * hd] for h in hs]
    kh = [k[:, h * hd:(h + 1) * hd] for h in hs]
    vh = [v[:, h * hd:(h + 1) * hd] for h in hs]
    csc = [cs_col[:, ci:ci + 1] for ci in cis]
    bcol = [beta[:, 2 * heads + ci:2 * heads + ci + 1] for ci in cis]
    qk_kk = [lax.dot_general(jnp.concatenate([qh[h], kh[h]], axis=0).astype(BF16), kh[h].astype(BF16),
                             (((1,), (1,)), ((), ())), preferred_element_type=F32) for h in hs]
    dec = [jnp.exp(jnp.where(incl, csc[h] - cs_row[cis[h]:cis[h] + 1, :], NEG)) for h in hs]
    attn = [(qk_kk[h][:L] * dec[h]).astype(BF16) for h in hs]
    xm = [jnp.where(strict, bcol[h] * qk_kk[h][L:] * dec[h], 0.0) for h in hs]
    egc = [jnp.exp(cc) for cc in csc]
    rhs = [jnp.concatenate([vh[h] * bcol[h], kh[h] * (bcol[h] * egc[h])], axis=1).astype(BF16) for h in hs]
    if L >= 2 * LANE:
        hb = L // 2
        tm = _unit_tri_inverse_many(xm, rev, hb)
        (f0, f1), (g0, g1) = ((hb, L), (0, hb)) if rev else ((0, hb), (hb, L))
        u1 = [jnp.dot(tm[h][f0:f1, f0:f1].astype(BF16), rhs[h][f0:f1], preferred_element_type=F32) for h in hs]
        r2 = [rhs[h][g0:g1].astype(F32) - jnp.dot(xm[h][g0:g1, f0:f1].astype(BF16), u1[h].astype(BF16),
                                                    preferred_element_type=F32) for h in hs]
        u2 = [jnp.dot(tm[h][g0:g1, g0:g1].astype(BF16), r2[h].astype(BF16), preferred_element_type=F32) for h in hs]
        uw = [jnp.concatenate([u2[h], u1[h]] if rev else [u1[h], u2[h]], axis=0) for h in hs]
    else:
        tm = _unit_tri_inverse_many(xm, rev)
        uw = [jnp.dot(tm[h].astype(BF16), rhs[h], preferred_element_type=F32) for h in hs]
    s_old = [s_ref[h] for h in hs]
    rs = [None] * heads
    zblk = jnp.zeros((hd, hd), F32)
    lane2 = lax.broadcasted_iota(jnp.int32, (1, 2 * hd), 1)
    for p in range(heads // 2):
        a, b = 2 * p, 2 * p + 1
        w_pair = jnp.concatenate([uw[a][:, hd:], uw[b][:, hd:]], axis=1)
        q_pair = q[:, a * hd:(b + 1) * hd] * jnp.where(lane2 < hd, egc[a], egc[b])
        s_bd = jnp.concatenate([jnp.concatenate([s_old[a], zblk], axis=1),
                                jnp.concatenate([zblk, s_old[b]], axis=1)], axis=0).astype(BF16)
        r = jnp.dot(jnp.concatenate([w_pair, q_pair], axis=0).astype(BF16), s_bd, preferred_element_type=F32)
        rs[a] = r[:, :hd]
        rs[b] = r[:, hd:]
    vnb = [(uw[h][:, :hd] - rs[h][:L]).astype(BF16) for h in hs]
    outs = [rs[h][L:] + jnp.dot(attn[h], vnb[h], preferred_element_type=F32) for h in hs]
    for h in hs:
        tot = csc[h][last:last + 1, :]
        kend = (kh[h] * jnp.exp(tot - csc[h])).astype(BF16)
        s_ref[h] = s_old[h] * jnp.exp(tot) + lax.dot_general(kend, vnb[h], (((0,), (0,)), ((), ())),
                                                            preferred_element_type=F32)
    o = jnp.concatenate(outs, axis=-1)
    if final:
        o = o + ob_ref[0]
        ms = _group_sum(o * o, ones_bd) * (1.0 / hd)
        gate = gate_ref[0]
        o = o * lax.rsqrt(ms + EPS) * ng_ref[...] * (gate * jax.nn.sigmoid(gate))
    o_ref[0] = o

    @pl.when(i == nb - 1)
    def _():
        sfin_ref[0] = s_ref[...]


def _gdn_pass(qkv, ab, cw, dtb, a, s0, *, rev, block, final_inputs=None, raster_out=False):
    b, t, width = qkv.shape
    heads = s0.shape[1]
    assert heads % 2 == 0
    dim = heads * GDN_HEAD_DIM
    TB = block
    nb = t // TB
    hb = TB // HALO
    final = final_inputs is not None

    def bidx(i):
        return nb - 1 - i if rev else i

    in_specs = [
        pl.BlockSpec((1, TB, width), lambda bi, i: (bi, bidx(i), 0)),
        pl.BlockSpec((1, HALO, width), lambda bi, i: (bi, jnp.maximum(bidx(i) * hb - 1, 0), 0)),
        pl.BlockSpec((1, HALO, width), lambda bi, i: (bi, jnp.minimum((bidx(i) + 1) * hb, t // HALO - 1), 0)),
        pl.BlockSpec((1, TB, LANE), lambda bi, i: (bi, bidx(i), 0)),
        pl.BlockSpec((8, width), lambda bi, i: (0, 0)),
        pl.BlockSpec((1, LANE), lambda bi, i: (0, 0)),
        pl.BlockSpec((1, LANE), lambda bi, i: (0, 0)),
        pl.BlockSpec((1, heads, GDN_HEAD_DIM, GDN_HEAD_DIM), lambda bi, i: (bi, 0, 0, 0)),
    ]
    args = [qkv, qkv, qkv, ab, cw, dtb, a, s0]
    if final:
        ob, gate, ng = final_inputs
        in_specs += [
            pl.BlockSpec((1, TB, dim), lambda bi, i: (bi, bidx(i), 0)),
            pl.BlockSpec((1, TB, dim), lambda bi, i: (bi, bidx(i), 0)),
            pl.BlockSpec((1, dim), lambda bi, i: (0, 0)),
        ]
        args += [ob, gate, ng]
    if raster_out:
        assert nb == GRID_W
        o_spec = pl.BlockSpec((1, TB, dim), lambda bi, i: (bi, 0, bidx(i)))
        o_shape = jax.ShapeDtypeStruct((b, TB, GRID_W * dim), F32)
    else:
        o_spec = pl.BlockSpec((1, TB, dim), lambda bi, i: (bi, bidx(i), 0))
        o_shape = jax.ShapeDtypeStruct((b, t, dim), F32)
    out_specs = [o_spec, pl.BlockSpec((1, heads, GDN_HEAD_DIM, GDN_HEAD_DIM), lambda bi, i: (bi, 0, 0, 0))]
    out_shape = [o_shape, jax.ShapeDtypeStruct((b, heads, GDN_HEAD_DIM, GDN_HEAD_DIM), F32)]
    if not final:
        out_specs.append(pl.BlockSpec((1, TB, width), lambda bi, i: (bi, bidx(i), 0)))
        out_shape.append(jax.ShapeDtypeStruct((b, t, width), F32))
    outs = pl.pallas_call(
        functools.partial(_gdn_kernel, block=TB, rev=rev, final=final, heads=heads),
        grid=(b, nb),
        in_specs=in_specs,
        out_specs=out_specs,
        out_shape=out_shape,
        scratch_shapes=[
            pltpu.VMEM((heads, GDN_HEAD_DIM, GDN_HEAD_DIM), F32),
            pltpu.VMEM((TB + 2 * HALO, width), F32),
        ],
        compiler_params=_params(),
        name="gdn_bwd" if rev else "gdn_fwd",
    )(*args)
    return (outs[0].reshape(b, t, dim),) + tuple(outs[1:])


def _gdn_stream(qkv, gate, ab, conv_w, a_log, dt_bias, norm_g, s0_f, s0_b, *, column_major):
    t = qkv.shape[1]
    heads = s0_f.shape[1]
    cw = jnp.pad(conv_w, ((0, 8 - CONV_W), (0, 0)))
    dtb = jnp.pad(dt_bias.reshape(1, -1), ((0, 0), (0, LANE - 2 * heads)))
    a = jnp.pad(-jnp.exp(a_log).reshape(1, -1), ((0, 0), (0, LANE - 2 * heads)))
    ng = jnp.tile(norm_g, heads)[None]
    rows = t // GRID_W
    fused_raster = column_major and rows <= SCAN_CHUNK and rows >= 2 * HALO and rows & (rows - 1) == 0
    block = rows if fused_raster else min(t, SCAN_CHUNK)
    ob, s_b, act = _gdn_pass(qkv, ab, cw, dtb, a, s0_b, rev=True, block=block)
    o, s_f = _gdn_pass(act, ab, cw, dtb, a, s0_f, rev=False, block=block, final_inputs=(ob, gate, ng),
                       raster_out=fused_raster)
    if column_major and not fused_raster:
        o = _to_raster(o)
    return o, s_f, s_b


def _router_kernel(x_ref, sc_ref, sh_ref, g_ref, rw_ref, h_ref, a_ref):
    x = x_ref[0]
    ms = jnp.mean(x * x, axis=-1, keepdims=True)
    h = x * lax.rsqrt(ms + EPS) * g_ref[...]
    h = h * (1.0 + sc_ref[0]) + sh_ref[0]
    hh, hm, _ = _split3(h)
    h_ref[0] = hh
    rw = rw_ref[...]
    rh = rw.astype(BF16)
    rm = (rw - rh.astype(F32)).astype(BF16)
    nt = (((1,), (1,)), ((), ()))
    lg = (lax.dot_general(rh, hh, nt, preferred_element_type=F32)
          + lax.dot_general(rh, hm, nt, preferred_element_type=F32)
          + lax.dot_general(rm, hh, nt, preferred_element_type=F32))
    ex = jnp.exp(lg - jnp.max(lg, axis=0, keepdims=True))
    aff = ex / jnp.sum(ex, axis=0, keepdims=True)
    for k in range(a_ref.shape[1]):
        a_ref[0, k] = aff[:, k * LANE:(k + 1) * LANE]


def _router(x, scale, shift, g, rw_t):
    b, t, d = x.shape
    e = rw_t.shape[0]
    tm = min(t, WIDE_TILE)
    return pl.pallas_call(
        _router_kernel,
        grid=(b, t // tm),
        in_specs=[
            pl.BlockSpec((1, tm, d), lambda i, j: (i, j, 0)),
            pl.BlockSpec((1, 1, d), lambda i, j: (i, 0, 0)),
            pl.BlockSpec((1, 1, d), lambda i, j: (i, 0, 0)),
            pl.BlockSpec((1, d), lambda i, j: (0, 0)),
            pl.BlockSpec((e, d), lambda i, j: (0, 0)),
        ],
        out_specs=[
            pl.BlockSpec((1, tm, d), lambda i, j: (i, j, 0)),
            pl.BlockSpec((1, tm // LANE, e, LANE), lambda i, j: (i, j, 0, 0)),
        ],
        out_shape=[
            jax.ShapeDtypeStruct((b, t, d), BF16),
            jax.ShapeDtypeStruct((b, t // LANE, e, LANE), F32),
        ],
        compiler_params=_params(),
        name="router",
    )(x, scale, shift, g, rw_t)


def _token_prefix(m3, ut, ones, lt):
    e, nb, _ = m3.shape
    m2 = m3.reshape(e * nb, LANE).astype(BF16)
    inb = jnp.dot(m2, ut, preferred_element_type=F32).reshape(e, nb, LANE)
    tot = jnp.dot(m2, ones, preferred_element_type=F32).reshape(e, nb, LANE)
    offs = jnp.stack([jnp.dot(lt, tot[i].astype(BF16), preferred_element_type=F32) for i in range(e)], axis=0)
    return inb, tot, offs


def _select_kernel(a_ref, idx_ref, gate_ref, srow_ref, st8_ref, npc_ref, cs_ref, cum_ref, cnt_ref, *, cap):
    a = a_ref[0]
    e_n, nb, _ = a.shape
    bits = lax.bitcast_convert_type(a, I32)

    def radix(i, prefix):
        cand = prefix | jnp.left_shift(jnp.int32(1), 30 - i)
        cnt = jnp.sum(jnp.sum((bits >= cand).astype(F32), axis=2, keepdims=True), axis=1, keepdims=True)
        return jnp.where(cnt >= cap, cand, prefix)

    thr = lax.fori_loop(0, 31, radix, jnp.zeros((e_n, 1, 1), I32))
    li = lax.broadcasted_iota(I32, (LANE, LANE), 0)
    lj = lax.broadcasted_iota(I32, (LANE, LANE), 1)
    ut = (li < lj).astype(BF16)
    ones = jnp.ones((LANE, LANE), BF16)
    bi = lax.broadcasted_iota(I32, (nb, nb), 0)
    bj = lax.broadcasted_iota(I32, (nb, nb), 1)
    lt = (bj < bi).astype(BF16)

    gt = bits > thr
    eq = bits == thr
    n_gt = jnp.sum(jnp.sum(gt.astype(F32), axis=2, keepdims=True), axis=1, keepdims=True)
    tie_in, _, tie_offs = _token_prefix(eq.astype(F32), ut, ones, lt)
    sel = gt | (eq & (tie_in + tie_offs < cap - n_gt))
    pos_in, cnt, offs = _token_prefix(sel.astype(F32), ut, ones, lt)
    cs_ref[...] = jnp.where(sel, pos_in + 1.0, 0.0)
    cum_ref[...] = offs + cnt
    cnt_ref[...] = cnt

    offs_i = offs.astype(I32)
    cnt_i = cnt.astype(I32)
    st8 = (offs_i >> 3) << 3
    npc = jnp.where(cnt_i > 0, (offs_i + cnt_i - st8 + (PIECE - 1)) >> 3, 0)
    rbase = []
    run = jnp.zeros((nb, LANE), I32)
    for i in range(e_n):
        rbase.append(run)
        run = run + PIECE * npc[i]
    rbase = jnp.stack(rbase, axis=0)
    srow_ref[0] = jnp.where(sel, rbase + pos_in.astype(I32) + offs_i - st8, -1)
    st8_ref[0] = st8
    npc_ref[0] = npc

    jrow = lax.broadcasted_iota(I32, (1, cap), 1).astype(F32)
    sub_nb = lax.broadcasted_iota(I32, (nb, cap), 0).astype(F32)
    sub_l = lax.broadcasted_iota(I32, (LANE, cap), 0).astype(F32)
    tn = (((0,), (0,)), ((), ()))

    def compact(ei, carry):
        cum_col = cum_ref[ei][:, 0:1]
        cnt_col = cnt_ref[ei][:, 0:1]
        ge = cum_col <= jrow
        blk_j = jnp.sum(ge.astype(F32), axis=0, keepdims=True)
        offs_j = jnp.sum(jnp.where(ge, cnt_col, 0.0), axis=0, keepdims=True)
        rank1 = jrow - offs_j + 1.0
        g_t = (sub_nb == blk_j).astype(BF16)
        row_t = lax.dot_general(cs_ref[ei].astype(BF16), g_t, tn, preferred_element_type=F32)
        match = row_t == rank1
        lane_j = jnp.sum(jnp.where(match, sub_l, 0.0), axis=0, keepdims=True)
        idx_ref[0, pl.ds(ei, 1), :] = (blk_j * LANE + lane_j).astype(I32)
        parts = _split3(a_ref[0, ei])
        aff_t = sum(lax.dot_general(p, g_t, tn, preferred_element_type=F32) for p in parts)
        gate_ref[0, pl.ds(ei, 1), :] = jnp.sum(jnp.where(match, aff_t, 0.0), axis=0, keepdims=True)
        return carry

    lax.fori_loop(0, e_n, compact, 0)


def _ec_select(aff_em, cap):
    b, e, nb, _ = aff_em.shape
    big = lambda dt: jax.ShapeDtypeStruct((b, e, nb, LANE), dt)
    spec4 = pl.BlockSpec((1, e, nb, LANE), lambda i: (i, 0, 0, 0))
    spec3 = pl.BlockSpec((1, e, cap), lambda i: (i, 0, 0))
    return pl.pallas_call(
        functools.partial(_select_kernel, cap=cap),
        grid=(b,),
        in_specs=[spec4],
        out_specs=[spec3, spec3, spec4, spec4, spec4],
        out_shape=[jax.ShapeDtypeStruct((b, e, cap), I32), jax.ShapeDtypeStruct((b, e, cap), F32),
                   big(I32), big(I32), big(I32)],
        scratch_shapes=[pltpu.VMEM((e, nb, LANE), F32)] * 3,
        compiler_params=pltpu.CompilerParams(dimension_semantics=("arbitrary",), vmem_limit_bytes=VMEM_LIMIT),
        name="ec_select",
    )(aff_em)


def _combine_kernel(st8_sm, npc_sm, ye_hbm, srow_ref, x_ref, g_ref, fn_ref, o_ref, stage, acc_ref, sem,
                    *, final, n_exp):
    b = pl.program_id(0)
    k = pl.program_id(1)
    nb = pl.num_programs(1)
    step = b * nb + k
    nsteps = pl.num_programs(0) * nb
    slot = step % 2

    def run_copy(bb, e, src_row, sl, dst_row, pieces):
        size = pieces * PIECE
        return pltpu.make_async_copy(ye_hbm.at[bb, e, pl.ds(pl.multiple_of(src_row, PIECE), size), :],
                                     stage.at[sl, pl.ds(pl.multiple_of(dst_row, PIECE), size), :], sem.at[sl])

    def issue(st, sl):
        bb = st // nb

        def per_e(e, r):
            s8 = st8_sm[st * n_exp + e]
            n = npc_sm[st * n_exp + e]

            def quad(p, r2):
                run_copy(bb, e, s8 + 4 * PIECE * p, sl, r2, 4).start()
                return r2 + 4 * PIECE

            r = lax.fori_loop(0, n >> 2, quad, r)
            done = (n >> 2) << 2

            @pl.when((n & 2) != 0)
            def _():
                run_copy(bb, e, s8 + PIECE * done, sl, r, 2).start()

            r = r + PIECE * (n & 2)
            done = done + (n & 2)

            @pl.when((n & 1) != 0)
            def _():
                run_copy(bb, e, s8 + PIECE * done, sl, r, 1).start()

            return r + PIECE * (n & 1)

        lax.fori_loop(0, n_exp, per_e, 0)

    @pl.when(step == 0)
    def _():
        issue(step, slot)

    @pl.when(step + 1 < nsteps)
    def _():
        issue(step + 1, 1 - slot)

    npieces = lax.fori_loop(0, n_exp, lambda e, s: s + npc_sm[step * n_exp + e], 0)
    rows = npieces * PIECE

    @pl.when(npieces > 0)
    def _():
        pltpu.make_async_copy(ye_hbm.at[0, 0, pl.ds(0, rows), :], stage.at[slot, pl.ds(0, rows), :],
                              sem.at[slot]).wait()
    stage[slot, pl.ds(pl.multiple_of(rows, PIECE), LANE), :] = jnp.zeros((LANE, stage.shape[-1]), F32)

    srow = srow_ref[0, 0]
    acc_ref[...] = jnp.zeros_like(acc_ref)
    riota = lax.broadcasted_iota(I32, (LANE, LANE), 0)
    tn = (((0,), (0,)), ((), ()))

    def chunk(c, carry):
        r0 = pl.multiple_of(c * LANE, LANE)
        rid = riota + r0
        pt = (srow[0:1, :] == rid).astype(F32)
        for e in range(1, n_exp):
            pt = pt + (srow[e:e + 1, :] == rid).astype(F32)
        ptb = pt.astype(BF16)
        st = stage[slot, pl.ds(r0, LANE), :]
        hi = st.astype(BF16)
        lo = (st - hi.astype(F32)).astype(BF16)
        acc_ref[...] += lax.dot_general(jnp.concatenate([ptb, ptb], axis=0), jnp.concatenate([hi, lo], axis=0), tn,
                                        preferred_element_type=F32)
        return carry

    lax.fori_loop(0, (rows + LANE - 1) // LANE, chunk, 0)
    y = x_ref[0] + g_ref[0] * acc_ref[...]
    if final:
        y = y * lax.rsqrt(jnp.mean(y * y, axis=-1, keepdims=True) + EPS) * fn_ref[...]
    o_ref[0] = y


def _ec_combine(ye, srow_bm, st8, npc, x, gate, final_g=None):
    b, t, d = x.shape
    e = ye.shape[1]
    nb = t // LANE
    final = final_g is not None
    fn = final_g if final else jnp.ones((1, d), F32)
    max_rows = e * (LANE + 2 * PIECE) + LANE
    grid_spec = pltpu.PrefetchScalarGridSpec(
        num_scalar_prefetch=2,
        grid=(b, nb),
        in_specs=[
            pl.BlockSpec(memory_space=pl.ANY),
            pl.BlockSpec((1, 1, e, LANE), lambda i, j, *_: (i, j, 0, 0)),
            pl.BlockSpec((1, LANE, d), lambda i, j, *_: (i, j, 0)),
            pl.BlockSpec((1, 1, d), lambda i, j, *_: (i, 0, 0)),
            pl.BlockSpec((1, d), lambda i, j, *_: (0, 0)),
        ],
        out_specs=pl.BlockSpec((1, LANE, d), lambda i, j, *_: (i, j, 0)),
        scratch_shapes=[
            pltpu.VMEM((2, max_rows, d), F32),
            pltpu.VMEM((LANE, d), F32),
            pltpu.SemaphoreType.DMA((2,)),
        ],
    )
    return pl.pallas_call(
        functools.partial(_combine_kernel, final=final, n_exp=e),
        grid_spec=grid_spec,
        out_shape=jax.ShapeDtypeStruct((b, t, d), F32),
        compiler_params=_params(),
        name="ec_combine",
    )(st8, npc, ye, srow_bm, x, gate, fn)


def _expert_choice_block(x, scale, shift, g2, gate2, router_w, ew, final_g):
    b, t, d = x.shape
    e = router_w.shape[-1]
    cap = EC_CAPACITY * t // e
    h2, aff_bm = _router(x, scale, shift, g2, router_w.T)
    idx, gate, srow, st8, npc = _ec_select(jnp.transpose(aff_bm, (0, 2, 1, 3)), cap)
    xe = jax.vmap(lambda hb, ib: hb[ib])(h2, idx)
    ye = _expert_ffn(xe.reshape(b * e, cap, d), gate.reshape(b * e, cap, 1), *ew).reshape(b, e, cap, d)
    srow_bm = jnp.transpose(srow, (0, 2, 1, 3))
    st8_f = jnp.transpose(st8[..., 0], (0, 2, 1)).reshape(-1)
    npc_f = jnp.transpose(npc[..., 0], (0, 2, 1)).reshape(-1)
    return _ec_combine(ye, srow_bm, st8_f, npc_f, x, gate2, final_g)


def _rms_norm(x, g):
    return x * lax.rsqrt(jnp.mean(x * x, axis=-1, keepdims=True) + EPS) * g


def _to_raster(u):
    b, t, c = u.shape
    rows = t // GRID_W
    return u.reshape(b, GRID_W, rows, c).transpose(0, 2, 1, 3).reshape(b, t, c)


def _expert_choice_ffn(h, router_w, ew):
    b, t, d = h.shape
    cap = EC_CAPACITY * t // N_EXPERTS
    aff = jax.nn.softmax(jnp.einsum('btd,de->bte', h, router_w, precision=lax.Precision.HIGHEST), axis=-1)
    gate, idx = lax.top_k(jnp.swapaxes(aff, 1, 2), cap)
    xe = jax.vmap(lambda hb, ib: hb[ib])(h, idx)
    ye = _expert_ffn(xe.reshape(b * N_EXPERTS, cap, d), gate.reshape(b * N_EXPERTS, cap, 1), *ew)
    ye = ye.reshape(b, N_EXPERTS, cap, d)
    return jax.vmap(lambda ib, yb: jnp.zeros((t, d), yb.dtype).at[ib.reshape(-1)].add(yb.reshape(-1, d)))(idx, ye)


def kernel(x, c, ctx, c_ctx, norm1_g, norm2_g, ada_w, ada_b, w_in, w_out, pool_w, pool_scale, ssd_conv_w, ssd_conv_b, ssd_a_log, ssd_dt_bias, ssd_d, ssd_norm_g, gdn_conv_w, gdn_a_log, gdn_dt_bias, gdn_norm_g, router_w, exp_w_gate, exp_w_up, exp_w_down, final_norm_g):
    depth, d, _ = w_in.shape
    b, t, _ = x.shape
    pool_dim = pool_scale.shape[-1]
    ssd_dim = ssd_norm_g.shape[-1]
    ssd_heads = ssd_dim // SSD_HEAD_DIM
    ssd_bc = SSD_GROUPS * SSD_STATE
    gdn_dim = gdn_conv_w.shape[-1] // 3
    gdn_heads = gdn_dim // GDN_HEAD_DIM
    splits = (pool_dim, ssd_dim, ssd_dim + 2 * ssd_bc, 2 * ssd_heads, 3 * gdn_dim, gdn_dim, 2 * gdn_heads,
              2 * gdn_heads)
    cut = [0] + np.cumsum(splits).tolist()
    r_widths = (ssd_dim, ssd_dim + 2 * ssd_bc, pool_dim, LANE)
    g_widths = (3 * gdn_dim, gdn_dim, LANE)

    assert depth >= 1
    sc = jax.nn.silu(c)
    scc = jax.nn.silu(c_ctx)[None]
    for l in range(depth):
        last = l == depth - 1
        wl = w_in[l]
        seg = [wl[:, cut[i]:cut[i + 1]] for i in range(8)]
        w_r = jnp.concatenate(
            [seg[1], seg[2], seg[0], jnp.pad(seg[3], ((0, 0), (0, LANE - 2 * ssd_heads)))], axis=1).astype(BF16)
        w_g = jnp.concatenate(
            [seg[4], seg[5], jnp.pad(jnp.concatenate([seg[6], seg[7]], axis=1), ((0, 0), (0, LANE - 4 * gdn_heads)))],
            axis=1).astype(BF16)
        w_o = w_out[l].astype(BF16)
        ew = (exp_w_gate, exp_w_up, exp_w_down, l)
        m_lat = jnp.split(sc @ ada_w[l] + ada_b[l], 6, axis=-1)
        m_ctx = [jnp.broadcast_to(m, (b, d)) for m in jnp.split(scc @ ada_w[l] + ada_b[l], 6, axis=-1)]
        g1 = norm1_g[l][None]

        def project(xx, mm, column_major):
            pr = _inproj(xx, mm[1][:, None], mm[0][:, None], g1, w_r, r_widths, column_major=False)
            pg = _inproj(xx, mm[1][:, None], mm[0][:, None], g1, w_g, g_widths, column_major=column_major)
            return pr, pg

        ssd_p = (ssd_conv_w[l], ssd_conv_b[l], ssd_a_log[l], ssd_dt_bias[l], ssd_d[l], ssd_norm_g[l])
        gdn_p = (gdn_conv_w[l], gdn_a_log[l], gdn_dt_bias[l], gdn_norm_g[l])
        zs = jnp.zeros((b, ssd_heads, SSD_STATE, SSD_HEAD_DIM), F32)
        zg = jnp.zeros((b, gdn_heads, GDN_HEAD_DIM, GDN_HEAD_DIM), F32)

        (c_z, c_xbc, c_pool, c_dt), (c_qkv, c_gate, c_ab) = project(ctx, m_ctx, False)
        (l_z, l_xbc, l_pool, l_dt), (l_qkv, l_gate, l_ab) = project(x, m_lat, True)

        s_ctx, ssd_sf, ssd_sb = _ssd_stream(c_z, c_xbc, c_dt, *ssd_p, zs, zs)
        g_ctx, gdn_sf, gdn_sb = _gdn_stream(c_qkv, c_gate, c_ab, *gdn_p, zg, zg, column_major=False)
        s_lat, _, _ = _ssd_stream(l_z, l_xbc, l_dt, *ssd_p, ssd_sf, ssd_sb)
        g_lat, _, _ = _gdn_stream(l_qkv, l_gate, l_ab, *gdn_p, gdn_sf, gdn_sb, column_major=True)
        x = _outproj([_pool_branch(l_pool, pool_w[l], pool_scale[l]), s_lat, g_lat], x, m_lat[2][:, None], w_o)
        fin = final_norm_g[None] if last else None
        if t % WIDE_TILE == 0:
            x = _expert_choice_block(x, m_lat[4][:, None], m_lat[3][:, None], norm2_g[l][None], m_lat[5][:, None],
                                     router_w[l], ew, fin)
        else:
            h2 = _rms_norm(x, norm2_g[l]) * (1 + m_lat[4][:, None]) + m_lat[3][:, None]
            x = x + m_lat[5][:, None] * _expert_choice_ffn(h2, router_w[l], ew)
            if last:
                x = _rms_norm(x, final_norm_g)
        if not last:
            ctx = _outproj([_pool_branch(c_pool, pool_w[l], pool_scale[l]), s_ctx, g_ctx], ctx, m_ctx[2][:, None],
                           w_o)
            h2c = _rms_norm(ctx, norm2_g[l]) * (1 + m_ctx[4][:, None]) + m_ctx[3][:, None]
            ctx = ctx + m_ctx[5][:, None] * _expert_choice_ffn(h2c, router_w[l], ew)
    return x
```

```python
import functools

import numpy as np
import jax
import jax.numpy as jnp
from jax import lax
from jax.experimental import pallas as pl
from jax.experimental.pallas import tpu as pltpu

F32 = jnp.float32
BF16 = jnp.bfloat16

I32 = jnp.int32

GRID_W = 64
CONV_W = 5
POOL_WINDOWS = (2, 4, 8, 16)
SSD_HEAD_DIM = 64
SSD_GROUPS = 2
SSD_STATE = 128
GDN_HEAD_DIM = 64
N_EXPERTS = 16
EC_CAPACITY = 2
EPS = 1e-6

LANE = 128
SUBLANES = 8
VMEM_LIMIT = 48 * 1024 * 1024

ROW_TILE = 512
WIDE_TILE = 1024
SCAN_CHUNK = 256
HALO = SUBLANES
PIECE = SUBLANES
NEG = -1e30


def _params():
    return pltpu.CompilerParams(dimension_semantics=("arbitrary", "arbitrary"), vmem_limit_bytes=VMEM_LIMIT)


def _inproj_kernel(x_ref, sc_ref, sh_ref, g_ref, w_ref, *o_refs, transpose_grid):
    x = x_ref[0]
    tm = x.shape[0]
    ms = jnp.mean(x * x, axis=-1, keepdims=True)
    h = x * lax.rsqrt(ms + EPS) * g_ref[...]
    h = (h * (1.0 + sc_ref[0]) + sh_ref[0]).astype(BF16)
    if transpose_grid:
        p = lax.broadcasted_iota(jnp.int32, (tm, tm), 0)
        q = lax.broadcasted_iota(jnp.int32, (tm, tm), 1)
        perm = (q == (p % SUBLANES) * GRID_W + p // SUBLANES).astype(BF16)
        h = jnp.dot(perm, h, preferred_element_type=F32).astype(BF16)
    y = jnp.dot(h, w_ref[...], preferred_element_type=F32)
    off = 0
    for o_ref in o_refs:
        n = o_ref.shape[-1]
        if transpose_grid:
            o_ref[0] = y[:, off:off + n].reshape(GRID_W, SUBLANES, n)
        else:
            o_ref[0] = y[:, off:off + n]
        off += n


def _inproj(x, scale, shift, g, w, widths, *, column_major):
    b, t, d = x.shape
    rows = t // GRID_W
    if column_major:
        assert rows % SUBLANES == 0
        tm = SUBLANES * GRID_W
        out_specs = [pl.BlockSpec((1, GRID_W, SUBLANES, n), lambda i, j: (i, 0, j, 0)) for n in widths]
        out_shape = [jax.ShapeDtypeStruct((b, GRID_W, rows, n), F32) for n in widths]
    else:
        tm = min(t, WIDE_TILE)
        out_specs = [pl.BlockSpec((1, tm, n), lambda i, j: (i, j, 0)) for n in widths]
        out_shape = [jax.ShapeDtypeStruct((b, t, n), F32) for n in widths]
    outs = pl.pallas_call(
        functools.partial(_inproj_kernel, transpose_grid=column_major),
        grid=(b, t // tm),
        in_specs=[
            pl.BlockSpec((1, tm, d), lambda i, j: (i, j, 0)),
            pl.BlockSpec((1, 1, d), lambda i, j: (i, 0, 0)),
            pl.BlockSpec((1, 1, d), lambda i, j: (i, 0, 0)),
            pl.BlockSpec((1, d), lambda i, j: (0, 0)),
            pl.BlockSpec((d, w.shape[1]), lambda i, j: (0, 0)),
        ],
        out_specs=out_specs,
        out_shape=out_shape,
        compiler_params=_params(),
        name="inproj",
    )(x, scale, shift, g, w)
    return [o.reshape(b, t, n) for o, n in zip(outs, widths)]


def _outproj_kernel(*refs, n_in):
    a_refs = refs[:n_in]
    x_ref, gate_ref, w_ref, o_ref = refs[n_in:]
    a = jnp.concatenate([a_ref[0].astype(BF16) for a_ref in a_refs], axis=-1)
    y = jnp.dot(a, w_ref[...], preferred_element_type=F32)
    o_ref[0] = x_ref[0] + gate_ref[0] * y


def _outproj(parts, x, gate, w):
    b, t, d = x.shape
    tm = min(t, ROW_TILE)
    n_in = len(parts)
    return pl.pallas_call(
        functools.partial(_outproj_kernel, n_in=n_in),
        grid=(b, t // tm),
        in_specs=(
            [pl.BlockSpec((1, tm, p.shape[-1]), lambda i, j: (i, j, 0)) for p in parts]
            + [pl.BlockSpec((1, tm, d), lambda i, j: (i, j, 0)), pl.BlockSpec((1, 1, d), lambda i, j: (i, 0, 0)),
               pl.BlockSpec(w.shape, lambda i, j: (0, 0))]),
        out_specs=pl.BlockSpec((1, tm, d), lambda i, j: (i, j, 0)),
        out_shape=jax.ShapeDtypeStruct((b, t, d), F32),
        compiler_params=_params(),
        name="outproj",
    )(*parts, x, gate, w)


def _pool_kernel(u_ref, prev_ref, next_ref, w_ref, sc_ref, o_ref, e0_ref, e1_ref, *, block, seq, group):
    TM = block
    j = pl.program_id(1)
    nb = pl.num_programs(1)
    u = u_ref[0]
    n_ext = TM + 2 * HALO
    e0_ref[0:HALO, :] = jnp.where(j > 0, prev_ref[0], 0.0)
    e0_ref[HALO:HALO + TM, :] = u
    e0_ref[HALO + TM:, :] = jnp.where(j < nb - 1, next_ref[0], 0.0)
    tok = j * TM + lax.broadcasted_iota(jnp.int32, (TM, 1), 0)
    lane = lax.broadcasted_iota(jnp.int32, (1, u.shape[-1]), 1)
    src, dst = e0_ref, e1_ref
    pooled = jnp.zeros_like(u)
    half = 1
    for gi, win in enumerate(POOL_WINDOWS):
        assert win == 2 * half
        lo_r, hi_r = half, n_ext - half
        if half == 1:
            dst[lo_r:hi_r, :] = src[lo_r - 1:hi_r - 1, :] + src[lo_r:hi_r, :]
        else:
            q = half // 2
            dst[lo_r:hi_r, :] = src[lo_r - q:hi_r - q, :] + src[lo_r + q:hi_r + q, :]
        cnt = (jnp.minimum(tok + half, seq) - jnp.maximum(tok - half, 0)).astype(F32)
        mean = dst[HALO:HALO + TM, :] / cnt
        pooled = jnp.where((lane >= gi * group) & (lane < (gi + 1) * group), mean, pooled)
        src, dst = dst, src
        half *= 2
    dd = pooled - u
    y = jnp.dot(dd.astype(BF16), w_ref[...], preferred_element_type=F32)
    o_ref[0] = y * sc_ref[...]


def _pool_branch(u, pool_w, pool_scale):
    b, t, c = u.shape
    ng, pg, _ = pool_w.shape
    tm = min(t, ROW_TILE)
    hb = tm // HALO
    w_bd = jnp.zeros((c, c), F32)
    for gi in range(ng):
        w_bd = w_bd.at[gi * pg:(gi + 1) * pg, gi * pg:(gi + 1) * pg].set(pool_w[gi])
    return pl.pallas_call(
        functools.partial(_pool_kernel, block=tm, seq=t, group=pg),
        grid=(b, t // tm),
        in_specs=[
            pl.BlockSpec((1, tm, c), lambda i, j: (i, j, 0)),
            pl.BlockSpec((1, HALO, c), lambda i, j: (i, jnp.maximum(j * hb - 1, 0), 0)),
            pl.BlockSpec((1, HALO, c), lambda i, j: (i, jnp.minimum((j + 1) * hb, t // HALO - 1), 0)),
            pl.BlockSpec((c, c), lambda i, j: (0, 0)),
            pl.BlockSpec((1, c), lambda i, j: (0, 0)),
        ],
        out_specs=pl.BlockSpec((1, tm, c), lambda i, j: (i, j, 0)),
        out_shape=jax.ShapeDtypeStruct((b, t, c), F32),
        scratch_shapes=[pltpu.VMEM((tm + 2 * HALO, c), F32), pltpu.VMEM((tm + 2 * HALO, c), F32)],
        compiler_params=_params(),
        name="pool",
    )(u, u, u, w_bd.astype(BF16), pool_scale[None])


def _ffn_kernel(x_ref, gate_ref, wg_ref, wu_ref, wd_ref, o_ref, wgb_ref, wub_ref, wdb_ref):
    @pl.when(pl.program_id(1) == 0)
    def _():
        wgb_ref[...] = wg_ref[0].astype(BF16)
        wub_ref[...] = wu_ref[0].astype(BF16)
        wdb_ref[...] = wd_ref[0].astype(BF16)

    x = x_ref[0].astype(BF16)
    hg = jnp.dot(x, wgb_ref[...], preferred_element_type=F32)
    hu = jnp.dot(x, wub_ref[...], preferred_element_type=F32)
    hid = (hg * jax.nn.sigmoid(hg)) * hu
    y = jnp.dot(hid.astype(BF16), wdb_ref[...], preferred_element_type=F32)
    o_ref[0] = y * gate_ref[0]


def _expert_ffn(xe, gate, wg, wu, wd, layer):
    be, c, d = xe.shape
    _, e, _, f = wg.shape
    tc = min(c, WIDE_TILE)
    return pl.pallas_call(
        _ffn_kernel,
        grid=(be, c // tc),
        in_specs=[
            pl.BlockSpec((1, tc, d), lambda i, j: (i, j, 0)),
            pl.BlockSpec((1, tc, 1), lambda i, j: (i, j, 0)),
            pl.BlockSpec((None, 1, d, f), lambda i, j: (layer, i % e, 0, 0)),
            pl.BlockSpec((None, 1, d, f), lambda i, j: (layer, i % e, 0, 0)),
            pl.BlockSpec((None, 1, f, d), lambda i, j: (layer, i % e, 0, 0)),
        ],
        out_specs=pl.BlockSpec((1, tc, d), lambda i, j: (i, j, 0)),
        out_shape=jax.ShapeDtypeStruct((be, c, d), F32),
        scratch_shapes=[pltpu.VMEM((d, f), BF16), pltpu.VMEM((d, f), BF16), pltpu.VMEM((f, d), BF16)],
        compiler_params=_params(),
        name="expert_ffn",
    )(xe, gate, wg, wu, wd)


def _split3(a):
    hi = a.astype(BF16)
    r1 = a - hi.astype(F32)
    mid = r1.astype(BF16)
    lo = (r1 - mid.astype(F32)).astype(BF16)
    return hi, mid, lo


def _cumsum_both(la, incl):
    parts = _split3(la)
    cs_col = sum(jnp.dot(incl, p, preferred_element_type=F32) for p in parts)
    cs_row = sum(lax.dot_general(p, incl, (((0,), (1,)), ((), ())), preferred_element_type=F32) for p in parts)
    return cs_col, cs_row


def _ssd_kernel(*refs, chunk, rev, final, heads):
    if final:
        (xbc_ref, prev_ref, next_ref, dt_ref, cw_ref, cb_ref, dtb_ref, a_ref, dsk_ref, s0_ref,
         yb_ref, z_ref, ng_ref, y_ref, sfin_ref, s_ref, ext_ref) = refs
    else:
        (xbc_ref, prev_ref, next_ref, dt_ref, cw_ref, cb_ref, dtb_ref, a_ref, dsk_ref, s0_ref,
         y_ref, sfin_ref, act_ref, s_ref, ext_ref) = refs
    L = chunk
    i = pl.program_id(1)
    nc = pl.num_programs(1)
    j = nc - 1 - i if rev else i
    hd = SSD_HEAD_DIM
    ssd_dim = heads * hd
    rep = heads // SSD_GROUPS

    @pl.when(i == 0)
    def _():
        s_ref[...] = s0_ref[0]

    if final:
        act = xbc_ref[0]
    else:
        ext_ref[0:HALO, :] = jnp.where(j > 0, prev_ref[0], 0.0)
        ext_ref[HALO:HALO + L, :] = xbc_ref[0]
        ext_ref[HALO + L:, :] = jnp.where(j < nc - 1, next_ref[0], 0.0)
        base = HALO - CONV_W // 2
        acc = cb_ref[...] + cw_ref[0:1, :] * ext_ref[base:base + L, :]
        for k in range(1, CONV_W):
            acc = acc + cw_ref[k:k + 1, :] * ext_ref[base + k:base + k + L, :]
        act = acc * jax.nn.sigmoid(acc)
        act_ref[0] = act
    xs = act[:, :ssd_dim]
    bmat = act[:, ssd_dim:ssd_dim + SSD_GROUPS * SSD_STATE]
    cmat = act[:, ssd_dim + SSD_GROUPS * SSD_STATE:]

    dtv = jax.nn.softplus(dt_ref[0] + dtb_ref[...])
    la = dtv * a_ref[...]
    row = lax.broadcasted_iota(jnp.int32, (L, L), 0)
    col = lax.broadcasted_iota(jnp.int32, (L, L), 1)
    mask = (row <= col) if rev else (row >= col)
    cs_col, cs_row = _cumsum_both(la, mask.astype(BF16))
    last = 0 if rev else L - 1

    gmats = []
    for g in range(SSD_GROUPS):
        cg = cmat[:, g * SSD_STATE:(g + 1) * SSD_STATE].astype(BF16)
        bg = bmat[:, g * SSD_STATE:(g + 1) * SSD_STATE].astype(BF16)
        gmats.append(lax.dot_general(cg, bg, (((1,), (1,)), ((), ())), preferred_element_type=F32))

    hs = range(heads)
    cis = [(heads if rev else 0) + h for h in hs]
    b_g = [bmat[:, g * SSD_STATE:(g + 1) * SSD_STATE] for g in range(SSD_GROUPS)]
    c_g = [cmat[:, g * SSD_STATE:(g + 1) * SSD_STATE] for g in range(SSD_GROUPS)]
    csc = [cs_col[:, ci:ci + 1] for ci in cis]
    tot = [cc[last:last + 1, :] for cc in csc]
    dec = [jnp.exp(jnp.where(mask, csc[h] - cs_row[cis[h]:cis[h] + 1, :], NEG)) for h in hs]
    xs_h = [xs[:, h * hd:(h + 1) * hd] for h in hs]
    xdt = [(xs_h[h] * dtv[:, cis[h]:cis[h] + 1]).astype(BF16) for h in hs]
    s_old = [s_ref[h] for h in hs]
    y_intra = [jnp.dot((gmats[h // rep] * dec[h]).astype(BF16), xdt[h], preferred_element_type=F32) for h in hs]
    y_inter = [jnp.dot((c_g[h // rep] * jnp.exp(csc[h])).astype(BF16), s_old[h].astype(BF16),
                       preferred_element_type=F32) for h in hs]
    local = [lax.dot_general((b_g[h // rep] * jnp.exp(tot[h] - csc[h])).astype(BF16), xdt[h],
                             (((0,), (0,)), ((), ())), preferred_element_type=F32) for h in hs]
    for h in hs:
        s_ref[h] = s_old[h] * jnp.exp(tot[h]) + local[h]
    ys = [y_intra[h] + y_inter[h] for h in hs]
    if final:
        ys = [ys[h] + dsk_ref[:, h * hd:(h + 1) * hd] * xs_h[h] for h in hs]
    y = jnp.concatenate(ys, axis=-1)
    if final:
        y = y + yb_ref[0]
        z = z_ref[0]
        y = y * (z * jax.nn.sigmoid(z))
        y = y * lax.rsqrt(jnp.mean(y * y, axis=-1, keepdims=True) + EPS) * ng_ref[...]
    y_ref[0] = y

    @pl.when(i == nc - 1)
    def _():
        sfin_ref[0] = s_ref[...]


def _ssd_pass(xbc, dt, cw, cb, dtb, a, dsk, s0, *, rev, final_inputs=None):
    b, t, width = xbc.shape
    heads = s0.shape[1]
    ssd_dim = heads * SSD_HEAD_DIM
    L = min(t, SCAN_CHUNK)
    nc = t // L
    hb = L // HALO
    final = final_inputs is not None

    def cidx(i):
        return nc - 1 - i if rev else i

    in_specs = [
        pl.BlockSpec((1, L, width), lambda bi, i: (bi, cidx(i), 0)),
        pl.BlockSpec((1, HALO, width), lambda bi, i: (bi, jnp.maximum(cidx(i) * hb - 1, 0), 0)),
        pl.BlockSpec((1, HALO, width), lambda bi, i: (bi, jnp.minimum((cidx(i) + 1) * hb, t // HALO - 1), 0)),
        pl.BlockSpec((1, L, LANE), lambda bi, i: (bi, cidx(i), 0)),
        pl.BlockSpec((8, width), lambda bi, i: (0, 0)),
        pl.BlockSpec((1, width), lambda bi, i: (0, 0)),
        pl.BlockSpec((1, LANE), lambda bi, i: (0, 0)),
        pl.BlockSpec((1, LANE), lambda bi, i: (0, 0)),
        pl.BlockSpec((1, ssd_dim), lambda bi, i: (0, 0)),
        pl.BlockSpec((1, heads, SSD_STATE, SSD_HEAD_DIM), lambda bi, i: (bi, 0, 0, 0)),
    ]
    args = [xbc, xbc, xbc, dt, cw, cb, dtb, a, dsk, s0]
    if final:
        yb, z, ng = final_inputs
        in_specs += [
            pl.BlockSpec((1, L, ssd_dim), lambda bi, i: (bi, cidx(i), 0)),
            pl.BlockSpec((1, L, ssd_dim), lambda bi, i: (bi, cidx(i), 0)),
            pl.BlockSpec((1, ssd_dim), lambda bi, i: (0, 0)),
        ]
        args += [yb, z, ng]
    out_specs = [
        pl.BlockSpec((1, L, ssd_dim), lambda bi, i: (bi, cidx(i), 0)),
        pl.BlockSpec((1, heads, SSD_STATE, SSD_HEAD_DIM), lambda bi, i: (bi, 0, 0, 0)),
    ]
    out_shape = [
        jax.ShapeDtypeStruct((b, t, ssd_dim), F32),
        jax.ShapeDtypeStruct((b, heads, SSD_STATE, SSD_HEAD_DIM), F32),
    ]
    if not final:
        out_specs.append(pl.BlockSpec((1, L, width), lambda bi, i: (bi, cidx(i), 0)))
        out_shape.append(jax.ShapeDtypeStruct((b, t, width), F32))
    return pl.pallas_call(
        functools.partial(_ssd_kernel, chunk=L, rev=rev, final=final, heads=heads),
        grid=(b, nc),
        in_specs=in_specs,
        out_specs=out_specs,
        out_shape=out_shape,
        scratch_shapes=[
            pltpu.VMEM((heads, SSD_STATE, SSD_HEAD_DIM), F32),
            pltpu.VMEM((L + 2 * HALO, width), F32),
        ],
        compiler_params=_params(),
        name="ssd_bwd" if rev else "ssd_fwd",
    )(*args)


def _ssd_stream(z, xbc, dt, conv_w, conv_b, a_log, dt_bias, d_skip, norm_g, s0_f, s0_b):
    heads = s0_f.shape[1]
    cw = jnp.pad(conv_w, ((0, 8 - CONV_W), (0, 0)))
    cb = conv_b[None]
    dtb = jnp.pad(dt_bias.reshape(1, -1), ((0, 0), (0, LANE - 2 * heads)))
    a = jnp.pad(-jnp.exp(a_log).reshape(1, -1), ((0, 0), (0, LANE - 2 * heads)))
    dsk = jnp.repeat(d_skip, SSD_HEAD_DIM)[None]
    yb, s_b, act = _ssd_pass(xbc, dt, cw, cb, dtb, a, dsk, s0_b, rev=True)
    y, s_f = _ssd_pass(act, dt, cw, cb, dtb, a, dsk, s0_f, rev=False, final_inputs=(yb, z, norm_g[None]))
    return y, s_f, s_b


def _group_sum(a, ones_bd):
    hi = a.astype(BF16)
    lo = (a - hi.astype(F32)).astype(BF16)
    outs = []
    for g in range(a.shape[-1] // LANE):
        sl = slice(g * LANE, (g + 1) * LANE)
        outs.append(jnp.dot(hi[:, sl], ones_bd, preferred_element_type=F32)
                    + jnp.dot(lo[:, sl], ones_bd, preferred_element_type=F32))
    return jnp.concatenate(outs, axis=-1)


def _mm_bf16(a, b):
    return jnp.dot(a.astype(BF16), b.astype(BF16), preferred_element_type=F32)


def _unit_tri_inverse_many(xms, rev, size=None):
    n = xms[0].shape[0]
    row = lax.broadcasted_iota(jnp.int32, (n, n), 0)
    col = lax.broadcasted_iota(jnp.int32, (n, n), 1)
    eye = (row == col).astype(F32)
    ds = None
    m, sh = 1, 0
    while m < (size or n):
        same = (row >> (sh + 1)) == (col >> (sh + 1))
        rbit = (row >> sh) & 1
        cbit = (col >> sh) & 1
        sel = same & ((rbit == 0) & (cbit == 1) if rev else (rbit == 1) & (cbit == 0))
        cs = [jnp.where(sel, x, 0.0) for x in xms]
        if ds is None:
            ds = [eye - c for c in cs]
        elif m < SUBLANES:
            es = [_mm_bf16(c, d) for c, d in zip(cs, ds)]
            ds = [d - _mm_bf16(d, e) for d, e in zip(ds, es)]
        else:
            act = 0 if rev else 1

            def rows(a, which):
                return a.reshape(n // (2 * m), 2, m, n)[:, which].reshape(n // 2, n)

            def merge(keep, new):
                pair = (new, keep) if rev else (keep, new)
                return jnp.stack([p.reshape(n // (2 * m), m, n) for p in pair], axis=1).reshape(n, n)

            es = [_mm_bf16(rows(c, act), d) for c, d in zip(cs, ds)]
            zero = jnp.zeros((n // 2, n), F32)
            ds = [merge(rows(d, 1 - act), rows(d, act) - _mm_bf16(rows(d, act), merge(zero, e)))
                  for d, e in zip(ds, es)]
        m, sh = 2 * m, sh + 1
    return ds


def _gdn_kernel(*refs, block, rev, final, heads):
    if final:
        (qkv_ref, prev_ref, next_ref, ab_ref, cw_ref, dtb_ref, a_ref, s0_ref, ob_ref, gate_ref, ng_ref,
         o_ref, sfin_ref, s_ref, ext_ref) = refs
    else:
        (qkv_ref, prev_ref, next_ref, ab_ref, cw_ref, dtb_ref, a_ref, s0_ref,
         o_ref, sfin_ref, act_ref, s_ref, ext_ref) = refs
    TB = block
    L = block
    hd = GDN_HEAD_DIM
    dim = heads * hd
    i = pl.program_id(1)
    nb = pl.num_programs(1)
    j = nb - 1 - i if rev else i

    @pl.when(i == 0)
    def _():
        s_ref[...] = s0_ref[0]

    ri = lax.broadcasted_iota(jnp.int32, (LANE, LANE), 0) // hd
    ci_ = lax.broadcasted_iota(jnp.int32, (LANE, LANE), 1) // hd
    ones_bd = (ri == ci_).astype(BF16)
    if final:
        act = qkv_ref[0]
        q = act[:, :dim]
        k = act[:, dim:2 * dim]
        v = act[:, 2 * dim:]
    else:
        ext_ref[0:HALO, :] = jnp.where(j > 0, prev_ref[0], 0.0)
        ext_ref[HALO:HALO + TB, :] = qkv_ref[0]
        ext_ref[HALO + TB:, :] = jnp.where(j < nb - 1, next_ref[0], 0.0)
        base = HALO - CONV_W // 2
        acc = cw_ref[0:1, :] * ext_ref[base:base + TB, :]
        for kk in range(1, CONV_W):
            acc = acc + cw_ref[kk:kk + 1, :] * ext_ref[base + kk:base + kk + TB, :]
        act = acc * jax.nn.sigmoid(acc)
        q = act[:, :dim]
        k = act[:, dim:2 * dim]
        v = act[:, 2 * dim:]
        q = q * lax.rsqrt(_group_sum(q * q, ones_bd) + EPS) * (hd ** -0.5)
        k = k * lax.rsqrt(_group_sum(k * k, ones_bd) + EPS)
        act_ref[0] = jnp.concatenate([q, k, v], axis=-1)
    ab = ab_ref[0]
    gl = a_ref[...] * jax.nn.softplus(ab + dtb_ref[...])
    beta = jax.nn.sigmoid(ab)

    row = lax.broadcasted_iota(jnp.int32, (L, L), 0)
    col = lax.broadcasted_iota(jnp.int32, (L, L), 1)
    incl = (row <= col) if rev else (row >= col)
    strict = (row < col) if rev else (row > col)
    last = 0 if rev else L - 1
    cs_col, cs_row = _cumsum_both(gl, incl.astype(BF16))

    hs = range(heads)
    cis = [(heads if rev else 0) + h for h in hs]
    qh = [q[:, h * hd:(h + 1) * hd] for h in hs]
    kh = [k[:, h * hd:(h + 1) * hd] for h in hs]
    vh = [v[:, h * hd:(h + 1) * hd] for h in hs]
    csc = [cs_col[:, ci:ci + 1] for ci in cis]
    bcol = [beta[:, 2 * heads + ci:2 * heads + ci + 1] for ci in cis]
    qk_kk = [lax.dot_general(jnp.concatenate([qh[h], kh[h]], axis=0).astype(BF16), kh[h].astype(BF16),
                             (((1,), (1,)), ((), ())), preferred_element_type=F32) for h in hs]
    dec = [jnp.exp(jnp.where(incl, csc[h] - cs_row[cis[h]:cis[h] + 1, :], NEG)) for h in hs]
    attn = [(qk_kk[h][:L] * dec[h]).astype(BF16) for h in hs]
    xm = [jnp.where(strict, bcol[h] * qk_kk[h][L:] * dec[h], 0.0) for h in hs]
    egc = [jnp.exp(cc) for cc in csc]
    rhs = [jnp.concatenate([vh[h] * bcol[h], kh[h] * (bcol[h] * egc[h])], axis=1).astype(BF16) for h in hs]
    if L >= 2 * LANE:
        hb = L // 2
        tm = _unit_tri_inverse_many(xm, rev, hb)
        (f0, f1), (g0, g1) = ((hb, L), (0, hb)) if rev else ((0, hb), (hb, L))
        u1 = [jnp.dot(tm[h][f0:f1, f0:f1].astype(BF16), rhs[h][f0:f1], preferred_element_type=F32) for h in hs]
        r2 = [rhs[h][g0:g1].astype(F32) - jnp.dot(xm[h][g0:g1, f0:f1].astype(BF16), u1[h].astype(BF16),
                                                    preferred_element_type=F32) for h in hs]
        u2 = [jnp.dot(tm[h][g0:g1, g0:g1].astype(BF16), r2[h].astype(BF16), preferred_element_type=F32) for h in hs]
        uw = [jnp.concatenate([u2[h], u1[h]] if rev else [u1[h], u2[h]], axis=0) for h in hs]
    else:
        tm = _unit_tri_inverse_many(xm, rev)
        uw = [jnp.dot(tm[h].astype(BF16), rhs[h], preferred_element_type=F32) for h in hs]
    s_old = [s_ref[h] for h in hs]
    rs = [None] * heads
    zblk = jnp.zeros((hd, hd), F32)
    lane2 = lax.broadcasted_iota(jnp.int32, (1, 2 * hd), 1)
    for p in range(heads // 2):
        a, b = 2 * p, 2 * p + 1
        w_pair = jnp.concatenate([uw[a][:, hd:], uw[b][:, hd:]], axis=1)
        q_pair = q[:, a * hd:(b + 1) * hd] * jnp.where(lane2 < hd, egc[a], egc[b])
        s_bd = jnp.concatenate([jnp.concatenate([s_old[a], zblk], axis=1),
                                jnp.concatenate([zblk, s_old[b]], axis=1)], axis=0).astype(BF16)
        r = jnp.dot(jnp.concatenate([w_pair, q_pair], axis=0).astype(BF16), s_bd, preferred_element_type=F32)
        rs[a] = r[:, :hd]
        rs[b] = r[:, hd:]
    vnb = [(uw[h][:, :hd] - rs[h][:L]).astype(BF16) for h in hs]
    outs = [rs[h][L:] + jnp.dot(attn[h], vnb[h], preferred_element_type=F32) for h in hs]
    for h in hs:
        tot = csc[h][last:last + 1, :]
        kend = (kh[h] * jnp.exp(tot - csc[h])).astype(BF16)
        s_ref[h] = s_old[h] * jnp.exp(tot) + lax.dot_general(kend, vnb[h], (((0,), (0,)), ((), ())),
                                                            preferred_element_type=F32)
    o = jnp.concatenate(outs, axis=-1)
    if final:
        o = o + ob_ref[0]
        ms = _group_sum(o * o, ones_bd) * (1.0 / hd)
        gate = gate_ref[0]
        o = o * lax.rsqrt(ms + EPS) * ng_ref[...] * (gate * jax.nn.sigmoid(gate))
    o_ref[0] = o

    @pl.when(i == nb - 1)
    def _():
        sfin_ref[0] = s_ref[...]


def _gdn_pass(qkv, ab, cw, dtb, a, s0, *, rev, block, final_inputs=None, raster_out=False):
    b, t, width = qkv.shape
    heads = s0.shape[1]
    assert heads % 2 == 0
    dim = heads * GDN_HEAD_DIM
    TB = block
    nb = t // TB
    hb = TB // HALO
    final = final_inputs is not None

    def bidx(i):
        return nb - 1 - i if rev else i

    in_specs = [
        pl.BlockSpec((1, TB, width), lambda bi, i: (bi, bidx(i), 0)),
        pl.BlockSpec((1, HALO, width), lambda bi, i: (bi, jnp.maximum(bidx(i) * hb - 1, 0), 0)),
        pl.BlockSpec((1, HALO, width), lambda bi, i: (bi, jnp.minimum((bidx(i) + 1) * hb, t // HALO - 1), 0)),
        pl.BlockSpec((1, TB, LANE), lambda bi, i: (bi, bidx(i), 0)),
        pl.BlockSpec((8, width), lambda bi, i: (0, 0)),
        pl.BlockSpec((1, LANE), lambda bi, i: (0, 0)),
        pl.BlockSpec((1, LANE), lambda bi, i: (0, 0)),
        pl.BlockSpec((1, heads, GDN_HEAD_DIM, GDN_HEAD_DIM), lambda bi, i: (bi, 0, 0, 0)),
    ]
    args = [qkv, qkv, qkv, ab, cw, dtb, a, s0]
    if final:
        ob, gate, ng = final_inputs
        in_specs += [
            pl.BlockSpec((1, TB, dim), lambda bi, i: (bi, bidx(i), 0)),
            pl.BlockSpec((1, TB, dim), lambda bi, i: (bi, bidx(i), 0)),
            pl.BlockSpec((1, dim), lambda bi, i: (0, 0)),
        ]
        args += [ob, gate, ng]
    if raster_out:
        assert nb == GRID_W
        o_spec = pl.BlockSpec((1, TB, dim), lambda bi, i: (bi, 0, bidx(i)))
        o_shape = jax.ShapeDtypeStruct((b, TB, GRID_W * dim), F32)
    else:
        o_spec = pl.BlockSpec((1, TB, dim), lambda bi, i: (bi, bidx(i), 0))
        o_shape = jax.ShapeDtypeStruct((b, t, dim), F32)
    out_specs = [o_spec, pl.BlockSpec((1, heads, GDN_HEAD_DIM, GDN_HEAD_DIM), lambda bi, i: (bi, 0, 0, 0))]
    out_shape = [o_shape, jax.ShapeDtypeStruct((b, heads, GDN_HEAD_DIM, GDN_HEAD_DIM), F32)]
    if not final:
        out_specs.append(pl.BlockSpec((1, TB, width), lambda bi, i: (bi, bidx(i), 0)))
        out_shape.append(jax.ShapeDtypeStruct((b, t, width), F32))
    outs = pl.pallas_call(
        functools.partial(_gdn_kernel, block=TB, rev=rev, final=final, heads=heads),
        grid=(b, nb),
        in_specs=in_specs,
        out_specs=out_specs,
        out_shape=out_shape,
        scratch_shapes=[
            pltpu.VMEM((heads, GDN_HEAD_DIM, GDN_HEAD_DIM), F32),
            pltpu.VMEM((TB + 2 * HALO, width), F32),
        ],
        compiler_params=_params(),
        name="gdn_bwd" if rev else "gdn_fwd",
    )(*args)
    return (outs[0].reshape(b, t, dim),) + tuple(outs[1:])


def _gdn_stream(qkv, gate, ab, conv_w, a_log, dt_bias, norm_g, s0_f, s0_b, *, column_major):
    t = qkv.shape[1]
    heads = s0_f.shape[1]
    cw = jnp.pad(conv_w, ((0, 8 - CONV_W), (0, 0)))
    dtb = jnp.pad(dt_bias.reshape(1, -1), ((0, 0), (0, LANE - 2 * heads)))
    a = jnp.pad(-jnp.exp(a_log).reshape(1, -1), ((0, 0), (0, LANE - 2 * heads)))
    ng = jnp.tile(norm_g, heads)[None]
    rows = t // GRID_W
    fused_raster = column_major and rows <= SCAN_CHUNK and rows >= 2 * HALO and rows & (rows - 1) == 0
    block = rows if fused_raster else min(t, SCAN_CHUNK)
    ob, s_b, act = _gdn_pass(qkv, ab, cw, dtb, a, s0_b, rev=True, block=block)
    o, s_f = _gdn_pass(act, ab, cw, dtb, a, s0_f, rev=False, block=block, final_inputs=(ob, gate, ng),
                       raster_out=fused_raster)
    if column_major and not fused_raster:
        o = _to_raster(o)
    return o, s_f, s_b


def _router_kernel(x_ref, sc_ref, sh_ref, g_ref, rw_ref, h_ref, a_ref):
    x = x_ref[0]
    ms = jnp.mean(x * x, axis=-1, keepdims=True)
    h = x * lax.rsqrt(ms + EPS) * g_ref[...]
    h = h * (1.0 + sc_ref[0]) + sh_ref[0]
    hh, hm, _ = _split3(h)
    h_ref[0] = hh
    rw = rw_ref[...]
    rh = rw.astype(BF16)
    rm = (rw - rh.astype(F32)).astype(BF16)
    nt = (((1,), (1,)), ((), ()))
    lg = (lax.dot_general(rh, hh, nt, preferred_element_type=F32)
          + lax.dot_general(rh, hm, nt, preferred_element_type=F32)
          + lax.dot_general(rm, hh, nt, preferred_element_type=F32))
    ex = jnp.exp(lg - jnp.max(lg, axis=0, keepdims=True))
    aff = ex / jnp.sum(ex, axis=0, keepdims=True)
    for k in range(a_ref.shape[1]):
        a_ref[0, k] = aff[:, k * LANE:(k + 1) * LANE]


def _router(x, scale, shift, g, rw_t):
    b, t, d = x.shape
    e = rw_t.shape[0]
    tm = min(t, WIDE_TILE)
    return pl.pallas_call(
        _router_kernel,
        grid=(b, t // tm),
        in_specs=[
            pl.BlockSpec((1, tm, d), lambda i, j: (i, j, 0)),
            pl.BlockSpec((1, 1, d), lambda i, j: (i, 0, 0)),
            pl.BlockSpec((1, 1, d), lambda i, j: (i, 0, 0)),
            pl.BlockSpec((1, d), lambda i, j: (0, 0)),
            pl.BlockSpec((e, d), lambda i, j: (0, 0)),
        ],
        out_specs=[
            pl.BlockSpec((1, tm, d), lambda i, j: (i, j, 0)),
            pl.BlockSpec((1, tm // LANE, e, LANE), lambda i, j: (i, j, 0, 0)),
        ],
        out_shape=[
            jax.ShapeDtypeStruct((b, t, d), BF16),
            jax.ShapeDtypeStruct((b, t // LANE, e, LANE), F32),
        ],
        compiler_params=_params(),
        name="router",
    )(x, scale, shift, g, rw_t)


def _token_prefix(m3, ut, ones, lt):
    e, nb, _ = m3.shape
    m2 = m3.reshape(e * nb, LANE).astype(BF16)
    inb = jnp.dot(m2, ut, preferred_element_type=F32).reshape(e, nb, LANE)
    tot = jnp.dot(m2, ones, preferred_element_type=F32).reshape(e, nb, LANE)
    offs = jnp.stack([jnp.dot(lt, tot[i].astype(BF16), preferred_element_type=F32) for i in range(e)], axis=0)
    return inb, tot, offs


def _select_kernel(a_ref, idx_ref, gate_ref, srow_ref, st8_ref, npc_ref, cs_ref, cum_ref, cnt_ref, *, cap):
    a = a_ref[0]
    e_n, nb, _ = a.shape
    bits = lax.bitcast_convert_type(a, I32)

    def radix(i, prefix):
        cand = prefix | jnp.left_shift(jnp.int32(1), 30 - i)
        cnt = jnp.sum(jnp.sum((bits >= cand).astype(F32), axis=2, keepdims=True), axis=1, keepdims=True)
        return jnp.where(cnt >= cap, cand, prefix)

    thr = lax.fori_loop(0, 31, radix, jnp.zeros((e_n, 1, 1), I32))
    li = lax.broadcasted_iota(I32, (LANE, LANE), 0)
    lj = lax.broadcasted_iota(I32, (LANE, LANE), 1)
    ut = (li < lj).astype(BF16)
    ones = jnp.ones((LANE, LANE), BF16)
    bi = lax.broadcasted_iota(I32, (nb, nb), 0)
    bj = lax.broadcasted_iota(I32, (nb, nb), 1)
    lt = (bj < bi).astype(BF16)

    gt = bits > thr
    eq = bits == thr
    n_gt = jnp.sum(jnp.sum(gt.astype(F32), axis=2, keepdims=True), axis=1, keepdims=True)
    tie_in, _, tie_offs = _token_prefix(eq.astype(F32), ut, ones, lt)
    sel = gt | (eq & (tie_in + tie_offs < cap - n_gt))
    pos_in, cnt, offs = _token_prefix(sel.astype(F32), ut, ones, lt)
    cs_ref[...] = jnp.where(sel, pos_in + 1.0, 0.0)
    cum_ref[...] = offs + cnt
    cnt_ref[...] = cnt

    offs_i = offs.astype(I32)
    cnt_i = cnt.astype(I32)
    st8 = (offs_i >> 3) << 3
    npc = jnp.where(cnt_i > 0, (offs_i + cnt_i - st8 + (PIECE - 1)) >> 3, 0)
    rbase = []
    run = jnp.zeros((nb, LANE), I32)
    for i in range(e_n):
        rbase.append(run)
        run = run + PIECE * npc[i]
    rbase = jnp.stack(rbase, axis=0)
    srow_ref[0] = jnp.where(sel, rbase + pos_in.astype(I32) + offs_i - st8, -1)
    st8_ref[0] = st8
    npc_ref[0] = npc

    jrow = lax.broadcasted_iota(I32, (1, cap), 1).astype(F32)
    sub_nb = lax.broadcasted_iota(I32, (nb, cap), 0).astype(F32)
    sub_l = lax.broadcasted_iota(I32, (LANE, cap), 0).astype(F32)
    tn = (((0,), (0,)), ((), ()))

    def compact(ei, carry):
        cum_col = cum_ref[ei][:, 0:1]
        cnt_col = cnt_ref[ei][:, 0:1]
        ge = cum_col <= jrow
        blk_j = jnp.sum(ge.astype(F32), axis=0, keepdims=True)
        offs_j = jnp.sum(jnp.where(ge, cnt_col, 0.0), axis=0, keepdims=True)
        rank1 = jrow - offs_j + 1.0
        g_t = (sub_nb == blk_j).astype(BF16)
        row_t = lax.dot_general(cs_ref[ei].astype(BF16), g_t, tn, preferred_element_type=F32)
        match = row_t == rank1
        lane_j = jnp.sum(jnp.where(match, sub_l, 0.0), axis=0, keepdims=True)
        idx_ref[0, pl.ds(ei, 1), :] = (blk_j * LANE + lane_j).astype(I32)
        parts = _split3(a_ref[0, ei])
        aff_t = sum(lax.dot_general(p, g_t, tn, preferred_element_type=F32) for p in parts)
        gate_ref[0, pl.ds(ei, 1), :] = jnp.sum(jnp.where(match, aff_t, 0.0), axis=0, keepdims=True)
        return carry

    lax.fori_loop(0, e_n, compact, 0)


def _ec_select(aff_em, cap):
    b, e, nb, _ = aff_em.shape
    big = lambda dt: jax.ShapeDtypeStruct((b, e, nb, LANE), dt)
    spec4 = pl.BlockSpec((1, e, nb, LANE), lambda i: (i, 0, 0, 0))
    spec3 = pl.BlockSpec((1, e, cap), lambda i: (i, 0, 0))
    return pl.pallas_call(
        functools.partial(_select_kernel, cap=cap),
        grid=(b,),
        in_specs=[spec4],
        out_specs=[spec3, spec3, spec4, spec4, spec4],
        out_shape=[jax.ShapeDtypeStruct((b, e, cap), I32), jax.ShapeDtypeStruct((b, e, cap), F32),
                   big(I32), big(I32), big(I32)],
        scratch_shapes=[pltpu.VMEM((e, nb, LANE), F32)] * 3,
        compiler_params=pltpu.CompilerParams(dimension_semantics=("arbitrary",), vmem_limit_bytes=VMEM_LIMIT),
        name="ec_select",
    )(aff_em)


def _combine_kernel(st8_sm, npc_sm, ye_hbm, srow_ref, x_ref, g_ref, fn_ref, o_ref, stage, acc_ref, sem,
                    *, final, n_exp):
    b = pl.program_id(0)
    k = pl.program_id(1)
    nb = pl.num_programs(1)
    step = b * nb + k
    nsteps = pl.num_programs(0) * nb
    slot = step % 2

    def run_copy(bb, e, src_row, sl, dst_row, pieces):
        size = pieces * PIECE
        return pltpu.make_async_copy(ye_hbm.at[bb, e, pl.ds(pl.multiple_of(src_row, PIECE), size), :],
                                     stage.at[sl, pl.ds(pl.multiple_of(dst_row, PIECE), size), :], sem.at[sl])

    def issue(st, sl):
        bb = st // nb

        def per_e(e, r):
            s8 = st8_sm[st * n_exp + e]
            n = npc_sm[st * n_exp + e]

            def quad(p, r2):
                run_copy(bb, e, s8 + 4 * PIECE * p, sl, r2, 4).start()
                return r2 + 4 * PIECE

            r = lax.fori_loop(0, n >> 2, quad, r)
            done = (n >> 2) << 2

            @pl.when((n & 2) != 0)
            def _():
                run_copy(bb, e, s8 + PIECE * done, sl, r, 2).start(priority=1)

            r = r + PIECE * (n & 2)
            done = done + (n & 2)

            @pl.when((n & 1) != 0)
            def _():
                run_copy(bb, e, s8 + PIECE * done, sl, r, 1).start(priority=1)

            return r + PIECE * (n & 1)

        lax.fori_loop(0, n_exp, per_e, 0)

    @pl.when(step == 0)
    def _():
        issue(step, slot)

    @pl.when(step + 1 < nsteps)
    def _():
        issue(step + 1, 1 - slot)

    npieces = lax.fori_loop(0, n_exp, lambda e, s: s + npc_sm[step * n_exp + e], 0)
    rows = npieces * PIECE

    @pl.when(npieces > 0)
    def _():
        pltpu.make_async_copy(ye_hbm.at[0, 0, pl.ds(0, rows), :], stage.at[slot, pl.ds(0, rows), :],
                              sem.at[slot]).wait()
    stage[slot, pl.ds(pl.multiple_of(rows, PIECE), LANE), :] = jnp.zeros((LANE, stage.shape[-1]), F32)

    srow = srow_ref[0, 0]
    acc_ref[...] = jnp.zeros_like(acc_ref)
    riota = lax.broadcasted_iota(I32, (LANE, LANE), 0)
    tn = (((0,), (0,)), ((), ()))

    def chunk(c, carry):
        r0 = pl.multiple_of(c * LANE, LANE)
        rid = riota + r0
        pt = (srow[0:1, :] == rid).astype(F32)
        for e in range(1, n_exp):
            pt = pt + (srow[e:e + 1, :] == rid).astype(F32)
        ptb = pt.astype(BF16)
        st = stage[slot, pl.ds(r0, LANE), :]
        hi = st.astype(BF16)
        lo = (st - hi.astype(F32)).astype(BF16)
        acc_ref[...] += lax.dot_general(jnp.concatenate([ptb, ptb], axis=0), jnp.concatenate([hi, lo], axis=0), tn,
                                        preferred_element_type=F32)
        return carry

    lax.fori_loop(0, (rows + LANE - 1) // LANE, chunk, 0)
    y = x_ref[0] + g_ref[0] * acc_ref[...]
    if final:
        y = y * lax.rsqrt(jnp.mean(y * y, axis=-1, keepdims=True) + EPS) * fn_ref[...]
    o_ref[0] = y


def _ec_combine(ye, srow_bm, st8, npc, x, gate, final_g=None):
    b, t, d = x.shape
    e = ye.shape[1]
    nb = t // LANE
    final = final_g is not None
    fn = final_g if final else jnp.ones((1, d), F32)
    max_rows = e * (LANE + 2 * PIECE) + LANE
    grid_spec = pltpu.PrefetchScalarGridSpec(
        num_scalar_prefetch=2,
        grid=(b, nb),
        in_specs=[
            pl.BlockSpec(memory_space=pl.ANY),
            pl.BlockSpec((1, 1, e, LANE), lambda i, j, *_: (i, j, 0, 0)),
            pl.BlockSpec((1, LANE, d), lambda i, j, *_: (i, j, 0)),
            pl.BlockSpec((1, 1, d), lambda i, j, *_: (i, 0, 0)),
            pl.BlockSpec((1, d), lambda i, j, *_: (0, 0)),
        ],
        out_specs=pl.BlockSpec((1, LANE, d), lambda i, j, *_: (i, j, 0)),
        scratch_shapes=[
            pltpu.VMEM((2, max_rows, d), F32),
            pltpu.VMEM((LANE, d), F32),
            pltpu.SemaphoreType.DMA((2,)),
        ],
    )
    return pl.pallas_call(
        functools.partial(_combine_kernel, final=final, n_exp=e),
        grid_spec=grid_spec,
        out_shape=jax.ShapeDtypeStruct((b, t, d), F32),
        compiler_params=_params(),
        name="ec_combine",
    )(st8, npc, ye, srow_bm, x, gate, fn)


def _expert_choice_block(x, scale, shift, g2, gate2, router_w, ew, final_g):
    b, t, d = x.shape
    e = router_w.shape[-1]
    cap = EC_CAPACITY * t // e
    h2, aff_bm = _router(x, scale, shift, g2, router_w.T)
    idx, gate, srow, st8, npc = _ec_select(jnp.transpose(aff_bm, (0, 2, 1, 3)), cap)
    xe = jax.vmap(lambda hb, ib: hb[ib])(h2, idx)
    ye = _expert_ffn(xe.reshape(b * e, cap, d), gate.reshape(b * e, cap, 1), *ew).reshape(b, e, cap, d)
    srow_bm = jnp.transpose(srow, (0, 2, 1, 3))
    st8_f = jnp.transpose(st8[..., 0], (0, 2, 1)).reshape(-1)
    npc_f = jnp.transpose(npc[..., 0], (0, 2, 1)).reshape(-1)
    return _ec_combine(ye, srow_bm, st8_f, npc_f, x, gate2, final_g)


def _rms_norm(x, g):
    return x * lax.rsqrt(jnp.mean(x * x, axis=-1, keepdims=True) + EPS) * g


def _to_raster(u):
    b, t, c = u.shape
    rows = t // GRID_W
    return u.reshape(b, GRID_W, rows, c).transpose(0, 2, 1, 3).reshape(b, t, c)


def _expert_choice_ffn(h, router_w, ew):
    b, t, d = h.shape
    cap = EC_CAPACITY * t // N_EXPERTS
    aff = jax.nn.softmax(jnp.einsum('btd,de->bte', h, router_w, precision=lax.Precision.HIGHEST), axis=-1)
    gate, idx = lax.top_k(jnp.swapaxes(aff, 1, 2), cap)
    xe = jax.vmap(lambda hb, ib: hb[ib])(h, idx)
    ye = _expert_ffn(xe.reshape(b * N_EXPERTS, cap, d), gate.reshape(b * N_EXPERTS, cap, 1), *ew)
    ye = ye.reshape(b, N_EXPERTS, cap, d)
    return jax.vmap(lambda ib, yb: jnp.zeros((t, d), yb.dtype).at[ib.reshape(-1)].add(yb.reshape(-1, d)))(idx, ye)


def kernel(x, c, ctx, c_ctx, norm1_g, norm2_g, ada_w, ada_b, w_in, w_out, pool_w, pool_scale, ssd_conv_w, ssd_conv_b, ssd_a_log, ssd_dt_bias, ssd_d, ssd_norm_g, gdn_conv_w, gdn_a_log, gdn_dt_bias, gdn_norm_g, router_w, exp_w_gate, exp_w_up, exp_w_down, final_norm_g):
    depth, d, _ = w_in.shape
    b, t, _ = x.shape
    pool_dim = pool_scale.shape[-1]
    ssd_dim = ssd_norm_g.shape[-1]
    ssd_heads = ssd_dim // SSD_HEAD_DIM
    ssd_bc = SSD_GROUPS * SSD_STATE
    gdn_dim = gdn_conv_w.shape[-1] // 3
    gdn_heads = gdn_dim // GDN_HEAD_DIM
    splits = (pool_dim, ssd_dim, ssd_dim + 2 * ssd_bc, 2 * ssd_heads, 3 * gdn_dim, gdn_dim, 2 * gdn_heads,
              2 * gdn_heads)
    cut = [0] + np.cumsum(splits).tolist()
    r_widths = (ssd_dim, ssd_dim + 2 * ssd_bc, pool_dim, LANE)
    g_widths = (3 * gdn_dim, gdn_dim, LANE)

    assert depth >= 1
    sc = jax.nn.silu(c)
    scc = jax.nn.silu(c_ctx)[None]
    for l in range(depth):
        last = l == depth - 1
        wl = w_in[l]
        seg = [wl[:, cut[i]:cut[i + 1]] for i in range(8)]
        w_r = jnp.concatenate(
            [seg[1], seg[2], seg[0], jnp.pad(seg[3], ((0, 0), (0, LANE - 2 * ssd_heads)))], axis=1).astype(BF16)
        w_g = jnp.concatenate(
            [seg[4], seg[5], jnp.pad(jnp.concatenate([seg[6], seg[7]], axis=1), ((0, 0), (0, LANE - 4 * gdn_heads)))],
            axis=1).astype(BF16)
        w_o = w_out[l].astype(BF16)
        ew = (exp_w_gate, exp_w_up, exp_w_down, l)
        m_lat = jnp.split(sc @ ada_w[l] + ada_b[l], 6, axis=-1)
        m_ctx = [jnp.broadcast_to(m, (b, d)) for m in jnp.split(scc @ ada_w[l] + ada_b[l], 6, axis=-1)]
        g1 = norm1_g[l][None]

        def project(xx, mm, column_major):
            pr = _inproj(xx, mm[1][:, None], mm[0][:, None], g1, w_r, r_widths, column_major=False)
            pg = _inproj(xx, mm[1][:, None], mm[0][:, None], g1, w_g, g_widths, column_major=column_major)
            return pr, pg

        ssd_p = (ssd_conv_w[l], ssd_conv_b[l], ssd_a_log[l], ssd_dt_bias[l], ssd_d[l], ssd_norm_g[l])
        gdn_p = (gdn_conv_w[l], gdn_a_log[l], gdn_dt_bias[l], gdn_norm_g[l])
        zs = jnp.zeros((b, ssd_heads, SSD_STATE, SSD_HEAD_DIM), F32)
        zg = jnp.zeros((b, gdn_heads, GDN_HEAD_DIM, GDN_HEAD_DIM), F32)

        (c_z, c_xbc, c_pool, c_dt), (c_qkv, c_gate, c_ab) = project(ctx, m_ctx, False)
        (l_z, l_xbc, l_pool, l_dt), (l_qkv, l_gate, l_ab) = project(x, m_lat, True)

        s_ctx, ssd_sf, ssd_sb = _ssd_stream(c_z, c_xbc, c_dt, *ssd_p, zs, zs)
        g_ctx, gdn_sf, gdn_sb = _gdn_stream(c_qkv, c_gate, c_ab, *gdn_p, zg, zg, column_major=False)
        s_lat, _, _ = _ssd_stream(l_z, l_xbc, l_dt, *ssd_p, ssd_sf, ssd_sb)
        g_lat, _, _ = _gdn_stream(l_qkv, l_gate, l_ab, *gdn_p, gdn_sf, gdn_sb, column_major=True)
        x = _outproj([_pool_branch(l_pool, pool_w[l], pool_scale[l]), s_lat, g_lat], x, m_lat[2][:, None], w_o)
        fin = final_norm_g[None] if last else None
        if t % WIDE_TILE == 0:
            x = _expert_choice_block(x, m_lat[4][:, None], m_lat[3][:, None], norm2_g[l][None], m_lat[5][:, None],
                                     router_w[l], ew, fin)
        else:
            h2 = _rms_norm(x, norm2_g[l]) * (1 + m_lat[4][:, None]) + m_lat[3][:, None]
            x = x + m_lat[5][:, None] * _expert_choice_ffn(h2, router_w[l], ew)
            if last:
                x = _rms_norm(x, final_norm_g)
        if not last:
            ctx = _outproj([_pool_branch(c_pool, pool_w[l], pool_scale[l]), s_ctx, g_ctx], ctx, m_ctx[2][:, None],
                           w_o)
            h2c = _rms_norm(ctx, norm2_g[l]) * (1 + m_ctx[4][:, None]) + m_ctx[3][:, None]
            ctx = ctx + m_ctx[5][:, None] * _expert_choice_ffn(h2c, router_w[l], ew)
    return x
```
